```python
import jax, jax.numpy as jnp
from jax import lax
import numpy as np

D_MODEL = 2048
BATCH = 4
SEQ = 4096
DEPTH = 4

ATTN_WIDTH = D_MODEL // 2
HEAD_DIM = 64
N_Q_HEADS = ATTN_WIDTH // HEAD_DIM
N_KV_HEADS = 2
GQA_GROUP = N_Q_HEADS // N_KV_HEADS
WINDOW = 128
BLOCK = 128
ROT_DIM = HEAD_DIM // 4
ROPE_THETA = 500000.0
GMLP_WIDTH = D_MODEL - ATTN_WIDTH
GMLP_GROUPS = 8
GMLP_GROUP_DIM = GMLP_WIDTH // GMLP_GROUPS
CHUNK = 128
Q_COLS = N_Q_HEADS * HEAD_DIM
KV_COLS = N_KV_HEADS * HEAD_DIM
QKV_COLS = Q_COLS + 2 * KV_COLS
IN_COLS = QKV_COLS + 2 * GMLP_WIDTH
MIX_WIDTH = ATTN_WIDTH + GMLP_WIDTH
N_GROUPS = 4
EXPERTS_PER_GROUP = 8
N_EXPERTS = N_GROUPS * EXPERTS_PER_GROUP
TOP_K = 2
D_EXPERT = D_MODEL // 4
NORM_EPS = 1e-6
N_MOD = 6

kernel_name = "hybrid_swa_gmlp_hmoe_adaln"


def rmsnorm(x, g):
    xf = x.astype(jnp.float32)
    y = xf * lax.rsqrt(jnp.mean(xf * xf, axis=-1, keepdims=True) + NORM_EPS)
    return (y * g.astype(jnp.float32)).astype(x.dtype)


def layernorm(x, g, b):
    xf = x.astype(jnp.float32)
    mu = jnp.mean(xf, axis=-1, keepdims=True)
    var = jnp.mean(jnp.square(xf - mu), axis=-1, keepdims=True)
    y = (xf - mu) * lax.rsqrt(var + NORM_EPS)
    return (y * g.astype(jnp.float32) + b.astype(jnp.float32)).astype(x.dtype)


def partial_rotary(t, positions):
    half = ROT_DIM // 2
    inv_freq = ROPE_THETA ** (-(jnp.arange(0, ROT_DIM, 2, dtype=jnp.float32) / ROT_DIM))
    ang = positions.astype(jnp.float32)[..., None] * inv_freq
    cos = jnp.cos(ang)[:, :, None, :]
    sin = jnp.sin(ang)[:, :, None, :]
    tr = t[..., :ROT_DIM].astype(jnp.float32)
    t1, t2 = tr[..., :half], tr[..., half:]
    rot = jnp.concatenate([t1 * cos - t2 * sin, t2 * cos + t1 * sin], axis=-1)
    return jnp.concatenate([rot.astype(t.dtype), t[..., ROT_DIM:]], axis=-1)


def sliding_window_attention(q, k, v, sinks):
    B, S = q.shape[0], q.shape[1]
    nb = S // BLOCK
    qb = q.reshape(B, nb, BLOCK, N_KV_HEADS, GQA_GROUP, HEAD_DIM)
    kb = k.reshape(B, nb, BLOCK, N_KV_HEADS, HEAD_DIM)
    vb = v.reshape(B, nb, BLOCK, N_KV_HEADS, HEAD_DIM)

    def with_prev(t):
        prev = jnp.pad(t[:, :-1], ((0, 0), (1, 0), (0, 0), (0, 0), (0, 0)))
        return jnp.concatenate([prev, t], axis=2)

    kw, vw = with_prev(kb), with_prev(vb)
    scale = HEAD_DIM ** -0.5
    s = jnp.einsum('bnqhgd,bnkhd->bnhgqk', qb, kw,
                   preferred_element_type=jnp.float32) * scale
    qi = jnp.arange(BLOCK)[:, None]
    kj = jnp.arange(2 * BLOCK)[None, :]
    rel = qi + BLOCK - kj
    band = (rel >= 0) & (rel < WINDOW)
    not_pad = (jnp.arange(nb)[:, None, None] > 0) | (kj >= BLOCK)[None]
    valid = band[None] & not_pad
    s = jnp.where(valid[None, :, None, None], s, -jnp.inf)
    sink = sinks.astype(jnp.float32).reshape(N_KV_HEADS, GQA_GROUP)[None, None, :, :, None, None]
    m = jnp.maximum(jnp.max(s, axis=-1, keepdims=True), sink)
    p = jnp.exp(s - m)
    denom = jnp.sum(p, axis=-1, keepdims=True) + jnp.exp(sink - m)
    o = jnp.einsum('bnhgqk,bnkhd->bnqhgd', (p / denom).astype(v.dtype), vw)
    return o.reshape(B, S, N_Q_HEADS * HEAD_DIM)


def chunked_spatial_gating(z, w_s, b_s, ln_g, ln_b):
    B, S = z.shape[0], z.shape[1]
    nc = S // CHUNK
    u, v = z[..., :GMLP_WIDTH], z[..., GMLP_WIDTH:]
    v = layernorm(v, ln_g, ln_b)
    vc = v.reshape(B, nc, CHUNK, GMLP_GROUPS, GMLP_GROUP_DIM)
    w = w_s * jnp.tril(jnp.ones((CHUNK, CHUNK), w_s.dtype))[None]
    sp = jnp.einsum('hts,bcshd->bcthd', w, vc) + b_s.T[None, None, :, :, None]
    return u * sp.reshape(B, S, GMLP_WIDTH)


def hierarchical_moe(h, w_rg, b_rg, w_re, b_re, w_gate, w_up, w_down):
    B, S, D = h.shape
    T = B * S
    xt = h.reshape(T, D)
    g_prob = jax.nn.softmax((xt @ w_rg + b_rg).astype(jnp.float32), axis=-1)
    g_p, g_idx = lax.top_k(g_prob, 1)
    e_log = (xt @ w_re + b_re).astype(jnp.float32).reshape(T, N_GROUPS, EXPERTS_PER_GROUP)
    sel = jnp.broadcast_to(g_idx[:, :, None], (T, 1, EXPERTS_PER_GROUP))
    e_log = jnp.take_along_axis(e_log, sel, axis=1)[:, 0]
    e_p, e_idx = lax.top_k(jax.nn.softmax(e_log, axis=-1), TOP_K)
    e_p = e_p / jnp.sum(e_p, axis=-1, keepdims=True)
    weights = g_p * e_p
    expert = (g_idx * EXPERTS_PER_GROUP + e_idx).reshape(-1)
    order = jnp.argsort(expert)
    xs = xt[order // TOP_K]
    sizes = jnp.bincount(expert, length=N_EXPERTS).astype(jnp.int32)
    hg = lax.ragged_dot(xs, w_gate, sizes)
    hu = lax.ragged_dot(xs, w_up, sizes)
    ys = lax.ragged_dot(jax.nn.silu(hg) * hu, w_down, sizes)
    y = jnp.zeros_like(ys).at[order].set(ys).reshape(T, TOP_K, D)
    out = jnp.einsum('tk,tkd->td', weights.astype(y.dtype), y)
    return out.reshape(B, S, D)


def setup_inputs(seed: int = 0) -> dict:
    key = jax.random.key(seed)
    ks = jax.random.split(key, 24)
    f32 = jnp.float32
    nrm = lambda k, shape, s: jax.random.normal(k, shape, f32) * s
    x = nrm(ks[0], (BATCH, SEQ, D_MODEL), 1.0)
    c = nrm(ks[1], (BATCH, D_MODEL), 1.0)
    offset = jax.random.randint(ks[2], (BATCH, 1), 0, 1024, dtype=jnp.int32)
    positions = offset + jnp.arange(SEQ, dtype=jnp.int32)[None, :]
    return {
        "x": x,
        "c": c,
        "positions": positions,
        "w_ada": nrm(ks[3], (DEPTH, D_MODEL, N_MOD * D_MODEL), 0.1 * D_MODEL ** -0.5),
        "b_ada": nrm(ks[4], (DEPTH, N_MOD * D_MODEL), 0.01),
        "g_norm1": 1.0 + nrm(ks[5], (DEPTH, D_MODEL), 0.02),
        "w_in": nrm(ks[6], (DEPTH, D_MODEL, IN_COLS), D_MODEL ** -0.5),
        "b_in": nrm(ks[7], (DEPTH, IN_COLS), 0.01),
        "sinks": nrm(ks[8], (DEPTH, N_Q_HEADS), 0.5),
        "ln_v_g": 1.0 + nrm(ks[9], (DEPTH, GMLP_WIDTH), 0.02),
        "ln_v_b": nrm(ks[10], (DEPTH, GMLP_WIDTH), 0.01),
        "w_spatial": nrm(ks[11], (DEPTH, GMLP_GROUPS, CHUNK, CHUNK), CHUNK ** -0.5),
        "b_spatial": 1.0 + nrm(ks[12], (DEPTH, GMLP_GROUPS, CHUNK), 0.01),
        "g_attn_out": 1.0 + nrm(ks[13], (DEPTH, ATTN_WIDTH), 0.02),
        "g_gmlp_out": 1.0 + nrm(ks[14], (DEPTH, GMLP_WIDTH), 0.02),
        "w_out": nrm(ks[15], (DEPTH, MIX_WIDTH, D_MODEL), MIX_WIDTH ** -0.5),
        "g_norm2": 1.0 + nrm(ks[16], (DEPTH, D_MODEL), 0.02),
        "w_router_group": nrm(ks[17], (DEPTH, D_MODEL, N_GROUPS), D_MODEL ** -0.5),
        "b_router_group": nrm(ks[18], (DEPTH, N_GROUPS), 0.01),
        "w_router_expert": nrm(ks[19], (DEPTH, D_MODEL, N_EXPERTS), D_MODEL ** -0.5),
        "b_router_expert": nrm(ks[20], (DEPTH, N_EXPERTS), 0.01),
        "w_e_gate": nrm(ks[21], (DEPTH, N_EXPERTS, D_MODEL, D_EXPERT), D_MODEL ** -0.5),
        "w_e_up": nrm(ks[22], (DEPTH, N_EXPERTS, D_MODEL, D_EXPERT), D_MODEL ** -0.5),
        "w_e_down": nrm(ks[23], (DEPTH, N_EXPERTS, D_EXPERT, D_MODEL), D_EXPERT ** -0.5),
        "g_final": 1.0 + nrm(jax.random.fold_in(key, 99), (D_MODEL,), 0.02),
    }


def reference(x, c, positions, w_ada, b_ada, g_norm1, w_in, b_in, sinks, ln_v_g, ln_v_b,
              w_spatial, b_spatial, g_attn_out, g_gmlp_out, w_out, g_norm2,
              w_router_group, b_router_group, w_router_expert, b_router_expert,
              w_e_gate, w_e_up, w_e_down, g_final):
    B, S = x.shape[0], x.shape[1]
    c_act = jax.nn.silu(c)
    for l in range(DEPTH):
        mod = (c_act @ w_ada[l] + b_ada[l])[:, None, :]
        sh1, sc1, gt1, sh2, sc2, gt2 = jnp.split(mod, N_MOD, axis=-1)

        h = rmsnorm(x, g_norm1[l]) * (1.0 + sc1) + sh1
        proj = h @ w_in[l] + b_in[l]
        q = proj[..., :Q_COLS].reshape(B, S, N_Q_HEADS, HEAD_DIM)
        k = proj[..., Q_COLS:Q_COLS + KV_COLS].reshape(B, S, N_KV_HEADS, HEAD_DIM)
        v = proj[..., Q_COLS + KV_COLS:QKV_COLS].reshape(B, S, N_KV_HEADS, HEAD_DIM)
        z = jax.nn.gelu(proj[..., QKV_COLS:], approximate=False)
        q = partial_rotary(q, positions)
        k = partial_rotary(k, positions)
        attn = sliding_window_attention(q, k, v, sinks[l])
        gm = chunked_spatial_gating(z, w_spatial[l], b_spatial[l], ln_v_g[l], ln_v_b[l])
        merged = jnp.concatenate([rmsnorm(attn, g_attn_out[l]), rmsnorm(gm, g_gmlp_out[l])], axis=-1)
        x = x + (1.0 + gt1) * (merged @ w_out[l])

        h = rmsnorm(x, g_norm2[l]) * (1.0 + sc2) + sh2
        y = hierarchical_moe(h, w_router_group[l], b_router_group[l], w_router_expert[l],
                             b_router_expert[l], w_e_gate[l], w_e_up[l], w_e_down[l])
        x = x + (1.0 + gt2) * y
    return rmsnorm(x, g_final)
```

```python
import functools

import jax
import jax.numpy as jnp
from jax import lax
from jax.experimental import pallas as pl
from jax.experimental.pallas import tpu as pltpu

F32 = jnp.float32
BF16 = jnp.bfloat16

D_MODEL = 2048
ATTN_WIDTH = D_MODEL // 2
HEAD_DIM = 64
N_Q_HEADS = ATTN_WIDTH // HEAD_DIM
N_KV_HEADS = 2
GQA_GROUP = N_Q_HEADS // N_KV_HEADS
WINDOW = 128
BLOCK = 128
ROT_DIM = HEAD_DIM // 4
ROPE_THETA = 500000.0
GMLP_WIDTH = D_MODEL - ATTN_WIDTH
GMLP_GROUPS = 8
GMLP_GROUP_DIM = GMLP_WIDTH // GMLP_GROUPS
Q_COLS = N_Q_HEADS * HEAD_DIM
KV_COLS = N_KV_HEADS * HEAD_DIM
QKV_COLS = Q_COLS + 2 * KV_COLS
IN_COLS = QKV_COLS + 2 * GMLP_WIDTH
N_GROUPS = 4
EXPERTS_PER_GROUP = 8
N_EXPERTS = N_GROUPS * EXPERTS_PER_GROUP
D_EXPERT = D_MODEL // 4
NORM_EPS = 1e-6
N_MOD = 6

LANES = 128
SUBLANES = 8
VMEM_LIMIT_BYTES = 56 * 1024 * 1024

ADA_TN = 1024
PROJ_TM = 256
OUT_TM = 256
MOE_TM = 256
FINAL_TM = 512


def _cparams(sem):
    return pltpu.CompilerParams(dimension_semantics=sem, vmem_limit_bytes=VMEM_LIMIT_BYTES)


def _adaln_body(c_ref, w_ref, b_ref, o_ref):
    c = c_ref[...]
    ca = (c * jax.nn.sigmoid(c)).astype(BF16)
    w = w_ref[0].astype(BF16)
    o_ref[0] = jnp.dot(ca, w, preferred_element_type=F32) + b_ref[0]


def _adaln(c_pad, w_ada, b_ada):
    depth, d, n = w_ada.shape
    rows = c_pad.shape[0]
    return pl.pallas_call(
        _adaln_body,
        grid=(depth, n // ADA_TN),
        in_specs=[
            pl.BlockSpec((rows, d), lambda l, j: (0, 0)),
            pl.BlockSpec((1, d, ADA_TN), lambda l, j: (l, 0, j)),
            pl.BlockSpec((1, 1, ADA_TN), lambda l, j: (l, 0, j)),
        ],
        out_specs=pl.BlockSpec((1, rows, ADA_TN), lambda l, j: (l, 0, j)),
        out_shape=jax.ShapeDtypeStruct((depth, rows, n), F32),
        compiler_params=_cparams(("arbitrary", "arbitrary")),
        name="adaln",
    )(c_pad, w_ada, b_ada.reshape(depth, 1, n))


def _rms_mod(x, g, sc, sh):
    ms = jnp.mean(x * x, axis=-1, keepdims=True)
    return (x * lax.rsqrt(ms + NORM_EPS) * g) * (1.0 + sc) + sh


def _combine(x, y_ref, wt_ref, gate):
    wt = wt_ref[...]
    y = wt[:, 0:1] * y_ref[:, :D_MODEL] + wt[:, 1:2] * y_ref[:, D_MODEL:]
    return x + (1.0 + gate) * y


def _gelu_exact(z):
    return 0.5 * z * (1.0 + lax.erf(z * (2.0 ** -0.5)))


def _proj_body(has_prev, *refs):
    if has_prev:
        (x_ref, y_ref, wt_ref, pmod_ref, mod_ref, g_ref, w_ref, b_ref, cos_ref, sa_ref, sb_ref,
         xo_ref, qkv_ref, z_ref) = refs
        x = _combine(x_ref[...], y_ref, wt_ref, pmod_ref[0, 5:6, :])
        xo_ref[...] = x
    else:
        x_ref, mod_ref, g_ref, w_ref, b_ref, cos_ref, sa_ref, sb_ref, qkv_ref, z_ref = refs
        x = x_ref[...]
    mod = mod_ref[0]
    h = _rms_mod(x, g_ref[...], mod[1:2, :], mod[0:1, :]).astype(BF16)

    qkv = jnp.dot(h, w_ref[:, :QKV_COLS], preferred_element_type=F32) + b_ref[:, :QKV_COLS]
    cos, sa, sb = cos_ref[...], sa_ref[...], sb_ref[...]
    n_rot = (Q_COLS + KV_COLS) // LANES
    for j in range(QKV_COLS // LANES):
        ch = qkv[:, j * LANES:(j + 1) * LANES]
        if j < n_rot:
            ch = ch * cos + pltpu.roll(ch, LANES - ROT_DIM // 2, 1) * sa + pltpu.roll(ch, ROT_DIM // 2, 1) * sb
        if j < Q_COLS // LANES:
            ch = ch * (HEAD_DIM ** -0.5)
        qkv_ref[:, j * LANES:(j + 1) * LANES] = ch.astype(BF16)

    z = jnp.dot(h, w_ref[:, QKV_COLS:], preferred_element_type=F32) + b_ref[:, QKV_COLS:]
    z_ref[...] = _gelu_exact(z)


def _proj(x, prev, mod_l, g1, w_in, b_in, cos_t, sa_t, sb_t, seq):
    t, d = x.shape
    tm = PROJ_TM
    tiles_per_seq = seq // tm
    row = lambda i: (i, 0)
    const = lambda i: (0, 0)
    modmap = lambda i: (i // tiles_per_seq, 0, 0)
    in_specs = [pl.BlockSpec((tm, d), row)]
    args = [x]
    if prev is not None:
        y2, wts, pmod = prev
        in_specs += [pl.BlockSpec((tm, 2 * d), row), pl.BlockSpec((tm, LANES), row),
                     pl.BlockSpec((1, N_MOD, d), modmap)]
        args += [y2, wts, pmod]
    in_specs += [
        pl.BlockSpec((1, N_MOD, d), modmap),
        pl.BlockSpec((1, d), const),
        pl.BlockSpec((d, IN_COLS), const, pipeline_mode=pl.Buffered(1)),
        pl.BlockSpec((1, IN_COLS), const),
        pl.BlockSpec((tm, LANES), row),
        pl.BlockSpec((tm, LANES), row),
        pl.BlockSpec((tm, LANES), row),
    ]
    args += [mod_l, g1, w_in, b_in, cos_t, sa_t, sb_t]
    out_specs = [pl.BlockSpec((tm, QKV_COLS), row), pl.BlockSpec((tm, 2 * GMLP_WIDTH), row)]
    out_shape = [jax.ShapeDtypeStruct((t, QKV_COLS), BF16), jax.ShapeDtypeStruct((t, 2 * GMLP_WIDTH), F32)]
    if prev is not None:
        out_specs = [pl.BlockSpec((tm, d), row)] + out_specs
        out_shape = [jax.ShapeDtypeStruct((t, d), F32)] + out_shape
    outs = pl.pallas_call(
        functools.partial(_proj_body, prev is not None),
        grid=(t // tm,),
        in_specs=in_specs,
        out_specs=out_specs,
        out_shape=out_shape,
        compiler_params=_cparams(("arbitrary",)),
        name="proj",
    )(*args)
    if prev is None:
        return (x,) + tuple(outs)
    return tuple(outs)


def _mixer_body(sink_ref, q_ref, kvc_ref, kvp_ref, z_ref, ws_ref, bst_ref, lng_ref, lnb_ref, ga_ref, gg_ref,
                o_ref):
    n = pl.program_id(1)
    q = q_ref[...]
    kvc = kvc_ref[...]
    kvp = kvp_ref[...]
    kcat = jnp.concatenate([kvp[:, :KV_COLS], kvc[:, :KV_COLS]], axis=0)
    vcat = jnp.concatenate([kvp[:, KV_COLS:], kvc[:, KV_COLS:]], axis=0)

    rows = GQA_GROUP * BLOCK
    lane = lax.broadcasted_iota(jnp.int32, (BLOCK, LANES), 1)
    lo = lane < HEAD_DIM
    qi = lax.broadcasted_iota(jnp.int32, (rows, 2 * BLOCK), 0) & (BLOCK - 1)
    kj = lax.broadcasted_iota(jnp.int32, (rows, 2 * BLOCK), 1)
    rel = qi + BLOCK - kj
    valid = (rel >= 0) & (rel < WINDOW) & ((n > 0) | (kj >= BLOCK))
    lane2 = lax.broadcasted_iota(jnp.int32, (2 * BLOCK, LANES), 1)
    lo2 = lane2 < HEAD_DIM
    zero_q = jnp.zeros((BLOCK, LANES), BF16)
    zero_v = jnp.zeros((2 * BLOCK, LANES), BF16)

    attn = []
    chunks_per_kv = GQA_GROUP * HEAD_DIM // LANES
    for hk in range(N_KV_HEADS):
        ksl = kcat[:, hk * HEAD_DIM:(hk + 1) * HEAD_DIM]
        vsl = vcat[:, hk * HEAD_DIM:(hk + 1) * HEAD_DIM]
        kk = jnp.concatenate([ksl, ksl], axis=1)
        vd = jnp.concatenate([vsl, vsl], axis=1)
        vv = jnp.concatenate([jnp.where(lo2, vd, zero_v), jnp.where(lo2, zero_v, vd)], axis=0)
        parts, sinks = [], []
        for j in range(chunks_per_kv):
            c = hk * chunks_per_kv + j
            qc = q[:, c * LANES:(c + 1) * LANES]
            parts += [jnp.where(lo, qc, zero_q), jnp.where(lo, zero_q, qc)]
            sinks += [jnp.full((BLOCK, 1), sink_ref[2 * c], F32), jnp.full((BLOCK, 1), sink_ref[2 * c + 1], F32)]
        lhs = jnp.concatenate(parts, axis=0)
        sink = jnp.concatenate(sinks, axis=0)
        s = lax.dot_general(lhs, kk, (((1,), (1,)), ((), ())), preferred_element_type=F32)
        s = jnp.where(valid, s, -jnp.inf)
        m = jnp.maximum(jnp.max(s, axis=-1, keepdims=True), sink)
        p = jnp.exp(s - m)
        denom = jnp.sum(p, axis=-1, keepdims=True) + jnp.exp(sink - m)
        pn = (p * (1.0 / denom)).astype(BF16)
        for j in range(chunks_per_kv):
            pc = jnp.concatenate([pn[(2 * j) * BLOCK:(2 * j + 1) * BLOCK], pn[(2 * j + 1) * BLOCK:(2 * j + 2) * BLOCK]],
                                 axis=1)
            attn.append(jnp.dot(pc, vv, preferred_element_type=F32))

    ss = attn[0] * attn[0]
    for a in attn[1:]:
        ss = ss + a * a
    scale = lax.rsqrt(jnp.sum(ss, axis=-1, keepdims=True) * (1.0 / ATTN_WIDTH) + NORM_EPS)
    for c, a in enumerate(attn):
        o_ref[:, c * LANES:(c + 1) * LANES] = (a * scale * ga_ref[:, c * LANES:(c + 1) * LANES]).astype(BF16)

    u = z_ref[:, :GMLP_WIDTH]
    v = z_ref[:, GMLP_WIDTH:]
    mu = jnp.mean(v, axis=-1, keepdims=True)
    vc = v - mu
    var = jnp.mean(vc * vc, axis=-1, keepdims=True)
    vn = ((vc * lax.rsqrt(var + NORM_EPS)) * lng_ref[...] + lnb_ref[...]).astype(BF16)
    ti = lax.broadcasted_iota(jnp.int32, (BLOCK, BLOCK), 0)
    si = lax.broadcasted_iota(jnp.int32, (BLOCK, BLOCK), 1)
    tril = si <= ti
    bst = bst_ref[...]
    gm = []
    for h in range(GMLP_GROUPS):
        w = jnp.where(tril, ws_ref[h], 0.0).astype(BF16)
        sp = jnp.dot(w, vn[:, h * LANES:(h + 1) * LANES], preferred_element_type=F32) + bst[:, h:h + 1]
        gm.append(u[:, h * LANES:(h + 1) * LANES] * sp)
    ss = gm[0] * gm[0]
    for a in gm[1:]:
        ss = ss + a * a
    scale = lax.rsqrt(jnp.sum(ss, axis=-1, keepdims=True) * (1.0 / GMLP_WIDTH) + NORM_EPS)
    for c, a in enumerate(gm):
        o_ref[:, ATTN_WIDTH + c * LANES:ATTN_WIDTH + (c + 1) * LANES] = (
            a * scale * gg_ref[:, c * LANES:(c + 1) * LANES]).astype(BF16)


def _mixer(qkv, z, sinks, w_sp, bst, ln_g, ln_b, g_attn, g_gmlp, batch, seq):
    t = qkv.shape[0]
    nb = seq // BLOCK
    kv_blk = Q_COLS // (2 * KV_COLS)
    cur = lambda b, n, s: (b * nb + n, 0)
    const2 = lambda b, n, s: (0, 0)
    grid_spec = pltpu.PrefetchScalarGridSpec(
        num_scalar_prefetch=1,
        grid=(batch, nb),
        in_specs=[
            pl.BlockSpec((BLOCK, Q_COLS), cur),
            pl.BlockSpec((BLOCK, 2 * KV_COLS), lambda b, n, s: (b * nb + n, kv_blk)),
            pl.BlockSpec((BLOCK, 2 * KV_COLS), lambda b, n, s: (b * nb + jnp.maximum(n - 1, 0), kv_blk)),
            pl.BlockSpec((BLOCK, 2 * GMLP_WIDTH), cur),
            pl.BlockSpec((GMLP_GROUPS, BLOCK, BLOCK), lambda b, n, s: (0, 0, 0)),
            pl.BlockSpec((BLOCK, GMLP_GROUPS), const2),
            pl.BlockSpec((1, GMLP_WIDTH), const2),
            pl.BlockSpec((1, GMLP_WIDTH), const2),
            pl.BlockSpec((1, ATTN_WIDTH), const2),
            pl.BlockSpec((1, GMLP_WIDTH), const2),
        ],
        out_specs=pl.BlockSpec((BLOCK, D_MODEL), cur),
    )
    return pl.pallas_call(
        _mixer_body,
        grid_spec=grid_spec,
        out_shape=jax.ShapeDtypeStruct((t, D_MODEL), BF16),
        compiler_params=_cparams(("arbitrary", "arbitrary")),
        name="mixer",
    )(sinks, qkv, qkv, qkv, z, w_sp, bst, ln_g, ln_b, g_attn, g_gmlp)


def _first_index_of_max(vals, lane, width):
    mx = jnp.max(vals, axis=-1, keepdims=True)
    idx = jnp.min(jnp.where(vals == mx, lane, width), axis=-1, keepdims=True)
    return mx, idx


def _outproj_body(m_ref, x_ref, mod_ref, g_ref, w_ref, wr_ref, br_ref,
                  xo_ref, h_ref, ids_ref, wt_ref, cnt_ref, carry_ref):
    i = pl.program_id(0)
    tm = x_ref.shape[0]

    @pl.when(i == 0)
    def _():
        carry_ref[...] = jnp.zeros_like(carry_ref)

    mod = mod_ref[0]
    y = jnp.dot(m_ref[...], w_ref[...], preferred_element_type=F32)
    x = x_ref[...] + (1.0 + mod[2:3, :]) * y
    xo_ref[...] = x
    h = _rms_mod(x, g_ref[...], mod[4:5, :], mod[3:4, :])
    h_ref[...] = h

    logits = jnp.dot(h, wr_ref[...], preferred_element_type=F32, precision=lax.Precision.HIGHEST) + br_ref[...]
    lane = lax.broadcasted_iota(jnp.int32, (tm, LANES), 1)
    neg = -jnp.inf
    gl = jnp.where(lane < N_GROUPS, logits, neg)
    ge = jnp.exp(gl - jnp.max(gl, axis=-1, keepdims=True))
    gprob = ge / jnp.sum(ge, axis=-1, keepdims=True)
    g_p, g_idx = _first_index_of_max(gprob, lane, LANES)

    base = N_GROUPS + EXPERTS_PER_GROUP * g_idx
    in_grp = (lane >= base) & (lane < base + EXPERTS_PER_GROUP)
    el = jnp.where(in_grp, logits, neg)
    ee = jnp.exp(el - jnp.max(el, axis=-1, keepdims=True))
    eprob = jnp.where(in_grp, ee / jnp.sum(ee, axis=-1, keepdims=True), -1.0)
    p1, i1 = _first_index_of_max(eprob, lane, LANES)
    eprob2 = jnp.where(lane == i1, -1.0, eprob)
    p2, i2 = _first_index_of_max(eprob2, lane, LANES)
    psum = p1 + p2
    w1 = g_p * (p1 / psum)
    w2 = g_p * (p2 / psum)
    e1 = i1 - N_GROUPS
    e2 = i2 - N_GROUPS

    oh1 = lane == e1
    oh2 = lane == e2
    onehot = jnp.where(oh1 | oh2, 1.0, 0.0)
    ri = lax.broadcasted_iota(jnp.int32, (tm, tm), 0)
    ci = lax.broadcasted_iota(jnp.int32, (tm, tm), 1)
    lower = jnp.where(ci < ri, 1.0, 0.0).astype(BF16)
    before = jnp.dot(lower, onehot.astype(BF16), preferred_element_type=F32) + carry_ref[...]
    r1 = jnp.sum(jnp.where(oh1, before, 0.0), axis=-1, keepdims=True)
    r2 = jnp.sum(jnp.where(oh2, before, 0.0), axis=-1, keepdims=True)
    carry = carry_ref[...] + jnp.sum(onehot, axis=0, keepdims=True)
    carry_ref[...] = carry
    cnt_ref[...] = jnp.broadcast_to(carry, cnt_ref.shape).astype(jnp.int32)

    ids = jnp.where(lane == 0, e1, jnp.where(lane == 1, e2, 0))
    ranks = jnp.where(lane == 2, r1, jnp.where(lane == 3, r2, 0.0)).astype(jnp.int32)
    ids_ref[...] = ids + ranks
    wt_ref[...] = jnp.where(lane == 0, w1, jnp.where(lane == 1, w2, 0.0))


def _outproj(merged, x, mod_l, g2, w_out, w_router, b_router, seq):
    t, d = x.shape
    tm = OUT_TM
    tiles_per_seq = seq // tm
    row = lambda i: (i, 0)
    const = lambda i: (0, 0)
    return pl.pallas_call(
        _outproj_body,
        grid=(t // tm,),
        in_specs=[
            pl.BlockSpec((tm, d), row),
            pl.BlockSpec((tm, d), row),
            pl.BlockSpec((1, N_MOD, d), lambda i: (i // tiles_per_seq, 0, 0)),
            pl.BlockSpec((1, d), const),
            pl.BlockSpec((d, d), const, pipeline_mode=pl.Buffered(1)),
            pl.BlockSpec((d, LANES), const),
            pl.BlockSpec((1, LANES), const),
        ],
        out_specs=[
            pl.BlockSpec((tm, d), row),
            pl.BlockSpec((tm, d), row),
            pl.BlockSpec((tm, LANES), row),
            pl.BlockSpec((tm, LANES), row),
            pl.BlockSpec((SUBLANES, LANES), const),
        ],
        out_shape=[
            jax.ShapeDtypeStruct((t, d), F32),
            jax.ShapeDtypeStruct((t, d), F32),
            jax.ShapeDtypeStruct((t, LANES), jnp.int32),
            jax.ShapeDtypeStruct((t, LANES), F32),
            jax.ShapeDtypeStruct((SUBLANES, LANES), jnp.int32),
        ],
        scratch_shapes=[pltpu.VMEM((1, LANES), F32)],
        compiler_params=_cparams(("arbitrary",)),
        name="outproj",
    )(merged, x, mod_l, g2, w_out, w_router, b_router)


def _moe_body(layer, n_slots, te_ref, nu_ref, h_hbm, idx_hbm, wg_ref, wu_ref, wd_ref, y_hbm,
              idx_smem, xbuf, obuf, wgb, wub, wdb, idx_sem, g_sem, s_sem):
    del layer
    i = pl.program_id(0)
    n_used = nu_ref[0]
    tm = xbuf.shape[1]

    def idx_copy(tile, slot):
        return pltpu.make_async_copy(idx_hbm.at[tile], idx_smem.at[slot], idx_sem.at[slot])

    def issue_gather(idx_slot, buf):
        def body(j, carry):
            s = idx_smem[idx_slot, 0, j]
            tok = lax.shift_right_logical(jnp.maximum(s, 0), 1)
            pltpu.make_async_copy(h_hbm.at[pl.ds(tok, 1), :], xbuf.at[buf, pl.ds(j, 1), :], g_sem.at[buf]).start()
            return carry
        lax.fori_loop(0, tm, body, 0, unroll=8)

    def wait_gather(buf):
        pltpu.make_async_copy(h_hbm.at[pl.ds(0, tm), :], xbuf.at[buf], g_sem.at[buf]).wait()

    def issue_scatter(idx_slot, buf):
        def body(j, carry):
            s = idx_smem[idx_slot, 0, j]
            dst = jnp.where(s >= 0, s, n_slots + j)
            pltpu.make_async_copy(obuf.at[buf, pl.ds(j, 1), :], y_hbm.at[pl.ds(dst, 1), :], s_sem.at[buf]).start()
            return carry
        lax.fori_loop(0, tm, body, 0, unroll=8)

    def wait_scatter(buf):
        pltpu.make_async_copy(obuf.at[buf], y_hbm.at[pl.ds(0, tm), :], s_sem.at[buf]).wait()

    @pl.when(i < n_used)
    def _():
        cur = i % 2
        nxt = i + 1

        @pl.when(i == 0)
        def _():
            idx_copy(0, 0).start()

            @pl.when(n_used > 1)
            def _():
                idx_copy(1, 1).start()

            idx_copy(0, 0).wait()
            issue_gather(0, 0)

        @pl.when(nxt < n_used)
        def _():
            idx_copy(nxt, nxt % 3).wait()
            issue_gather(nxt % 3, nxt % 2)

            @pl.when(nxt + 1 < n_used)
            def _():
                idx_copy(nxt + 1, (nxt + 1) % 3).start()

        new_expert = jnp.logical_or(i == 0, te_ref[i] != te_ref[jnp.maximum(i - 1, 0)])

        @pl.when(new_expert)
        def _():
            wgb[...] = wg_ref[0, 0].astype(BF16)
            wub[...] = wu_ref[0, 0].astype(BF16)
            wdb[...] = wd_ref[0, 0].astype(BF16)

        wait_gather(cur)
        xb = xbuf[cur].astype(BF16)
        g = jnp.dot(xb, wgb[...], preferred_element_type=F32)
        u = jnp.dot(xb, wub[...], preferred_element_type=F32)
        a = ((g * jax.nn.sigmoid(g)) * u).astype(BF16)
        obuf[cur] = jnp.dot(a, wdb[...], preferred_element_type=F32)

        @pl.when(i > 0)
        def _():
            wait_scatter(1 - cur)

        issue_scatter(i % 3, cur)

        @pl.when(i == n_used - 1)
        def _():
            wait_scatter(cur)
            obuf[cur] = jnp.zeros((tm, obuf.shape[2]), F32)
            fill = pltpu.make_async_copy(obuf.at[cur], y_hbm.at[pl.ds(n_slots, tm), :], s_sem.at[cur])
            fill.start()
            fill.wait()


def _moe(layer, h2, slot_of_row, tile_expert, n_used, w_gate, w_up, w_down):
    t, d = h2.shape
    n_slots = 2 * t
    nt, _, tm = slot_of_row.shape
    de = w_gate.shape[-1]
    grid_spec = pltpu.PrefetchScalarGridSpec(
        num_scalar_prefetch=2,
        grid=(nt,),
        in_specs=[
            pl.BlockSpec(memory_space=pl.ANY),
            pl.BlockSpec(memory_space=pl.ANY),
            pl.BlockSpec((1, 1, d, de), lambda i, te, nu: (layer, te[i], 0, 0)),
            pl.BlockSpec((1, 1, d, de), lambda i, te, nu: (layer, te[i], 0, 0)),
            pl.BlockSpec((1, 1, de, d), lambda i, te, nu: (layer, te[i], 0, 0)),
        ],
        out_specs=pl.BlockSpec(memory_space=pl.ANY),
        scratch_shapes=[
            pltpu.SMEM((3, 1, tm), jnp.int32),
            pltpu.VMEM((2, tm, d), F32),
            pltpu.VMEM((2, tm, d), F32),
            pltpu.VMEM((d, de), BF16),
            pltpu.VMEM((d, de), BF16),
            pltpu.VMEM((de, d), BF16),
            pltpu.SemaphoreType.DMA((3,)),
            pltpu.SemaphoreType.DMA((2,)),
            pltpu.SemaphoreType.DMA((2,)),
        ],
    )
    return pl.pallas_call(
        functools.partial(_moe_body, layer, n_slots),
        grid_spec=grid_spec,
        out_shape=jax.ShapeDtypeStruct((n_slots + tm, d), F32),
        compiler_params=_cparams(("arbitrary",)),
        name="moe",
    )(tile_expert, n_used, h2, slot_of_row, w_gate, w_up, w_down)


def _route_plan(ids, counts, t):
    tm = MOE_TM
    nt = (2 * t) // tm + N_EXPERTS
    e = ids[:, 0:2]
    rank = ids[:, 2:4]
    cnt = counts[0, :N_EXPERTS]
    tiles = (cnt + tm - 1) // tm
    tile_end = jnp.cumsum(tiles)
    off = (tile_end - tiles) * tm
    pos = off[e] + rank
    slot_of_row = jnp.full((nt * tm,), -1, jnp.int32).at[pos.reshape(-1)].set(
        jnp.arange(2 * t, dtype=jnp.int32), unique_indices=True)
    n_used = tile_end[-1]
    tile_ids = jnp.arange(nt, dtype=jnp.int32)
    te = jnp.searchsorted(tile_end, jnp.minimum(tile_ids, n_used - 1), side="right").astype(jnp.int32)
    return slot_of_row.reshape(nt, 1, tm), te, n_used.reshape(1).astype(jnp.int32)


def _final_body(x_ref, y_ref, wt_ref, pmod_ref, g_ref, o_ref):
    x = _combine(x_ref[...], y_ref, wt_ref, pmod_ref[0, 5:6, :])
    ms = jnp.mean(x * x, axis=-1, keepdims=True)
    o_ref[...] = x * lax.rsqrt(ms + NORM_EPS) * g_ref[...]


def _final(x, y2, wts, pmod, g_final, seq):
    t, d = x.shape
    tm = FINAL_TM
    tiles_per_seq = seq // tm
    row = lambda i: (i, 0)
    return pl.pallas_call(
        _final_body,
        grid=(t // tm,),
        in_specs=[
            pl.BlockSpec((tm, d), row),
            pl.BlockSpec((tm, 2 * d), row),
            pl.BlockSpec((tm, LANES), row),
            pl.BlockSpec((1, N_MOD, d), lambda i: (i // tiles_per_seq, 0, 0)),
            pl.BlockSpec((1, d), lambda i: (0, 0)),
        ],
        out_specs=pl.BlockSpec((tm, d), row),
        out_shape=jax.ShapeDtypeStruct((t, d), F32),
        compiler_params=_cparams(("arbitrary",)),
        name="final",
    )(x, y2, wts, pmod, g_final)


def _rotary_tables(positions):
    half = ROT_DIM // 2
    inv_freq = ROPE_THETA ** (-(jnp.arange(0, ROT_DIM, 2, dtype=F32) / ROT_DIM))
    ang = positions.reshape(-1).astype(F32)[:, None] * inv_freq
    cos, sin = jnp.cos(ang), jnp.sin(ang)
    t = ang.shape[0]
    pad = HEAD_DIM - ROT_DIM
    cos_h = jnp.concatenate([cos, cos, jnp.ones((t, pad), F32)], axis=1)
    sa_h = jnp.concatenate([-sin, jnp.zeros((t, HEAD_DIM - half), F32)], axis=1)
    sb_h = jnp.concatenate([jnp.zeros((t, half), F32), sin, jnp.zeros((t, pad), F32)], axis=1)
    rep = LANES // HEAD_DIM
    return jnp.tile(cos_h, (1, rep)), jnp.tile(sa_h, (1, rep)), jnp.tile(sb_h, (1, rep))


def kernel(x, c, positions, w_ada, b_ada, g_norm1, w_in, b_in, sinks, ln_v_g, ln_v_b, w_spatial, b_spatial,
           g_attn_out, g_gmlp_out, w_out, g_norm2, w_router_group, b_router_group, w_router_expert,
           b_router_expert, w_e_gate, w_e_up, w_e_down, g_final):
    batch, seq, d = x.shape
    depth = w_ada.shape[0]
    t = batch * seq
    assert d == D_MODEL and seq % max(PROJ_TM, OUT_TM, FINAL_TM, BLOCK) == 0 and (2 * t) % MOE_TM == 0

    c_pad = jnp.zeros((SUBLANES, d), F32).at[:batch].set(c)
    mod = _adaln(c_pad, w_ada, b_ada).reshape(depth, SUBLANES, N_MOD, d)
    cos_t, sa_t, sb_t = _rotary_tables(positions)

    n_route = N_GROUPS + N_EXPERTS
    w_router = jnp.concatenate(
        [w_router_group, w_router_expert, jnp.zeros((depth, d, LANES - n_route), F32)], axis=-1)
    b_router = jnp.concatenate(
        [b_router_group, b_router_expert, jnp.zeros((depth, LANES - n_route), F32)], axis=-1)

    xt = x.reshape(t, d)
    prev = None
    for l in range(depth):
        xt, qkv, z = _proj(xt, prev, mod[l], g_norm1[l][None], w_in[l].astype(BF16), b_in[l][None],
                           cos_t, sa_t, sb_t, seq)
        merged = _mixer(qkv, z, sinks[l], w_spatial[l], b_spatial[l].T, ln_v_g[l][None], ln_v_b[l][None],
                        g_attn_out[l][None], g_gmlp_out[l][None], batch, seq)
        xt, h2, ids, wts, counts = _outproj(merged, xt, mod[l], g_norm2[l][None], w_out[l].astype(BF16),
                                            w_router[l], b_router[l][None], seq)
        slot_of_row, tile_expert, n_used = _route_plan(ids, counts, t)
        y2 = _moe(l, h2, slot_of_row, tile_expert, n_used, w_e_gate, w_e_up, w_e_down)
        prev = (y2.reshape(y2.shape[0] // 2, 2 * d), wts, mod[l])
    out = _final(xt, prev[0], prev[1], prev[2], g_final[None], seq)
    return out.reshape(batch, seq, d)
```

```python
import functools

import jax
import jax.numpy as jnp
from jax import lax
from jax.experimental import pallas as pl
from jax.experimental.pallas import tpu as pltpu

F32 = jnp.float32
BF16 = jnp.bfloat16

D_MODEL = 2048
ATTN_WIDTH = D_MODEL // 2
HEAD_DIM = 64
N_Q_HEADS = ATTN_WIDTH // HEAD_DIM
N_KV_HEADS = 2
GQA_GROUP = N_Q_HEADS // N_KV_HEADS
WINDOW = 128
BLOCK = 128
ROT_DIM = HEAD_DIM // 4
ROPE_THETA = 500000.0
GMLP_WIDTH = D_MODEL - ATTN_WIDTH
GMLP_GROUPS = 8
GMLP_GROUP_DIM = GMLP_WIDTH // GMLP_GROUPS
Q_COLS = N_Q_HEADS * HEAD_DIM
KV_COLS = N_KV_HEADS * HEAD_DIM
QKV_COLS = Q_COLS + 2 * KV_COLS
IN_COLS = QKV_COLS + 2 * GMLP_WIDTH
N_GROUPS = 4
EXPERTS_PER_GROUP = 8
N_EXPERTS = N_GROUPS * EXPERTS_PER_GROUP
D_EXPERT = D_MODEL // 4
NORM_EPS = 1e-6
N_MOD = 6

LANES = 128
SUBLANES = 8
VMEM_LIMIT_BYTES = 56 * 1024 * 1024

ADA_TN = 1024
PROJ_TM = 256
OUT_TM = 256
MOE_TM = 256
FINAL_TM = 512


def _cparams(sem):
    return pltpu.CompilerParams(dimension_semantics=sem, vmem_limit_bytes=VMEM_LIMIT_BYTES)


def _adaln_body(c_ref, w_ref, b_ref, o_ref):
    c = c_ref[...]
    ca = (c * jax.nn.sigmoid(c)).astype(BF16)
    w = w_ref[0].astype(BF16)
    o_ref[0] = jnp.dot(ca, w, preferred_element_type=F32) + b_ref[0]


def _adaln(c_pad, w_ada, b_ada):
    depth, d, n = w_ada.shape
    rows = c_pad.shape[0]
    return pl.pallas_call(
        _adaln_body,
        grid=(depth, n // ADA_TN),
        in_specs=[
            pl.BlockSpec((rows, d), lambda l, j: (0, 0)),
            pl.BlockSpec((1, d, ADA_TN), lambda l, j: (l, 0, j)),
            pl.BlockSpec((1, 1, ADA_TN), lambda l, j: (l, 0, j)),
        ],
        out_specs=pl.BlockSpec((1, rows, ADA_TN), lambda l, j: (l, 0, j)),
        out_shape=jax.ShapeDtypeStruct((depth, rows, n), F32),
        compiler_params=_cparams(("arbitrary", "arbitrary")),
        name="adaln",
    )(c_pad, w_ada, b_ada.reshape(depth, 1, n))


def _rms_mod(x, g, sc, sh):
    ms = jnp.mean(x * x, axis=-1, keepdims=True)
    return (x * lax.rsqrt(ms + NORM_EPS) * g) * (1.0 + sc) + sh


def _combine(x, y0_ref, y1_ref, wt_ref, gate):
    wt = wt_ref[...]
    y = wt[:, 0:1] * y0_ref[...] + wt[:, 1:2] * y1_ref[...]
    return x + (1.0 + gate) * y


def _gelu_exact(z):
    return 0.5 * z * (1.0 + lax.erf(z * (2.0 ** -0.5)))


def _proj_body(has_prev, *refs):
    if has_prev:
        (x_ref, y0_ref, y1_ref, wt_ref, pmod_ref, mod_ref, g_ref, w_ref, b_ref, cos_ref, sa_ref, sb_ref,
         xo_ref, qkv_ref, z_ref) = refs
        x = _combine(x_ref[...], y0_ref, y1_ref, wt_ref, pmod_ref[0, 5:6, :])
        xo_ref[...] = x
    else:
        x_ref, mod_ref, g_ref, w_ref, b_ref, cos_ref, sa_ref, sb_ref, qkv_ref, z_ref = refs
        x = x_ref[...]
    mod = mod_ref[0]
    h = _rms_mod(x, g_ref[...], mod[1:2, :], mod[0:1, :]).astype(BF16)

    qkv = jnp.dot(h, w_ref[:, :QKV_COLS], preferred_element_type=F32) + b_ref[:, :QKV_COLS]
    cos, sa, sb = cos_ref[...], sa_ref[...], sb_ref[...]
    n_rot = (Q_COLS + KV_COLS) // LANES
    for j in range(QKV_COLS // LANES):
        ch = qkv[:, j * LANES:(j + 1) * LANES]
        if j < n_rot:
            ch = ch * cos + pltpu.roll(ch, LANES - ROT_DIM // 2, 1) * sa + pltpu.roll(ch, ROT_DIM // 2, 1) * sb
        if j < Q_COLS // LANES:
            ch = ch * (HEAD_DIM ** -0.5)
        qkv_ref[:, j * LANES:(j + 1) * LANES] = ch.astype(BF16)

    z = jnp.dot(h, w_ref[:, QKV_COLS:], preferred_element_type=F32) + b_ref[:, QKV_COLS:]
    z_ref[...] = _gelu_exact(z)


def _proj(x, prev, mod_l, g1, w_in, b_in, cos_t, sa_t, sb_t, seq):
    t, d = x.shape
    tm = PROJ_TM
    tiles_per_seq = seq // tm
    row = lambda i: (i, 0)
    const = lambda i: (0, 0)
    modmap = lambda i: (i // tiles_per_seq, 0, 0)
    in_specs = [pl.BlockSpec((tm, d), row)]
    args = [x]
    if prev is not None:
        y2, wts, pmod = prev
        in_specs += [pl.BlockSpec((tm, d), row), pl.BlockSpec((tm, d), lambda i: (i + t // tm, 0)),
                     pl.BlockSpec((tm, LANES), row), pl.BlockSpec((1, N_MOD, d), modmap)]
        args += [y2, y2, wts, pmod]
    in_specs += [
        pl.BlockSpec((1, N_MOD, d), modmap),
        pl.BlockSpec((1, d), const),
        pl.BlockSpec((d, IN_COLS), const, pipeline_mode=pl.Buffered(1)),
        pl.BlockSpec((1, IN_COLS), const),
        pl.BlockSpec((tm, LANES), row),
        pl.BlockSpec((tm, LANES), row),
        pl.BlockSpec((tm, LANES), row),
    ]
    args += [mod_l, g1, w_in, b_in, cos_t, sa_t, sb_t]
    out_specs = [pl.BlockSpec((tm, QKV_COLS), row), pl.BlockSpec((tm, 2 * GMLP_WIDTH), row)]
    out_shape = [jax.ShapeDtypeStruct((t, QKV_COLS), BF16), jax.ShapeDtypeStruct((t, 2 * GMLP_WIDTH), F32)]
    if prev is not None:
        out_specs = [pl.BlockSpec((tm, d), row)] + out_specs
        out_shape = [jax.ShapeDtypeStruct((t, d), F32)] + out_shape
    outs = pl.pallas_call(
        functools.partial(_proj_body, prev is not None),
        grid=(t // tm,),
        in_specs=in_specs,
        out_specs=out_specs,
        out_shape=out_shape,
        compiler_params=_cparams(("arbitrary",)),
        name="proj",
    )(*args)
    if prev is None:
        return (x,) + tuple(outs)
    return tuple(outs)


def _mixer_body(sink_ref, q_ref, kvc_ref, kvp_ref, z_ref, ws_ref, bst_ref, lng_ref, lnb_ref, ga_ref, gg_ref,
                o_ref):
    n = pl.program_id(1)
    q = q_ref[...]
    kvc = kvc_ref[...]
    kvp = kvp_ref[...]
    kcat = jnp.concatenate([kvp[:, :KV_COLS], kvc[:, :KV_COLS]], axis=0)
    vcat = jnp.concatenate([kvp[:, KV_COLS:], kvc[:, KV_COLS:]], axis=0)

    rows = GQA_GROUP * BLOCK
    lane = lax.broadcasted_iota(jnp.int32, (BLOCK, LANES), 1)
    lo = lane < HEAD_DIM
    qi = lax.broadcasted_iota(jnp.int32, (rows, 2 * BLOCK), 0) & (BLOCK - 1)
    kj = lax.broadcasted_iota(jnp.int32, (rows, 2 * BLOCK), 1)
    rel = qi + BLOCK - kj
    valid = (rel >= 0) & (rel < WINDOW) & ((n > 0) | (kj >= BLOCK))
    lane2 = lax.broadcasted_iota(jnp.int32, (2 * BLOCK, LANES), 1)
    lo2 = lane2 < HEAD_DIM
    zero_q = jnp.zeros((BLOCK, LANES), BF16)
    zero_v = jnp.zeros((2 * BLOCK, LANES), BF16)

    attn = []
    chunks_per_kv = GQA_GROUP * HEAD_DIM // LANES
    for hk in range(N_KV_HEADS):
        ksl = kcat[:, hk * HEAD_DIM:(hk + 1) * HEAD_DIM]
        vsl = vcat[:, hk * HEAD_DIM:(hk + 1) * HEAD_DIM]
        kk = jnp.concatenate([ksl, ksl], axis=1)
        vd = jnp.concatenate([vsl, vsl], axis=1)
        vv = jnp.concatenate([jnp.where(lo2, vd, zero_v), jnp.where(lo2, zero_v, vd)], axis=0)
        parts, sinks = [], []
        for j in range(chunks_per_kv):
            c = hk * chunks_per_kv + j
            qc = q[:, c * LANES:(c + 1) * LANES]
            parts += [jnp.where(lo, qc, zero_q), jnp.where(lo, zero_q, qc)]
            sinks += [jnp.full((BLOCK, 1), sink_ref[2 * c], F32), jnp.full((BLOCK, 1), sink_ref[2 * c + 1], F32)]
        lhs = jnp.concatenate(parts, axis=0)
        sink = jnp.concatenate(sinks, axis=0)
        s = lax.dot_general(lhs, kk, (((1,), (1,)), ((), ())), preferred_element_type=F32)
        s = jnp.where(valid, s, -jnp.inf)
        m = jnp.maximum(jnp.max(s, axis=-1, keepdims=True), sink)
        p = jnp.exp(s - m)
        denom = jnp.sum(p, axis=-1, keepdims=True) + jnp.exp(sink - m)
        pn = (p * (1.0 / denom)).astype(BF16)
        for j in range(chunks_per_kv):
            pc = jnp.concatenate([pn[(2 * j) * BLOCK:(2 * j + 1) * BLOCK], pn[(2 * j + 1) * BLOCK:(2 * j + 2) * BLOCK]],
                                 axis=1)
            attn.append(jnp.dot(pc, vv, preferred_element_type=F32))

    ss = attn[0] * attn[0]
    for a in attn[1:]:
        ss = ss + a * a
    scale = lax.rsqrt(jnp.sum(ss, axis=-1, keepdims=True) * (1.0 / ATTN_WIDTH) + NORM_EPS)
    for c, a in enumerate(attn):
        o_ref[:, c * LANES:(c + 1) * LANES] = (a * scale * ga_ref[:, c * LANES:(c + 1) * LANES]).astype(BF16)

    u = z_ref[:, :GMLP_WIDTH]
    v = z_ref[:, GMLP_WIDTH:]
    mu = jnp.mean(v, axis=-1, keepdims=True)
    vc = v - mu
    var = jnp.mean(vc * vc, axis=-1, keepdims=True)
    vn = ((vc * lax.rsqrt(var + NORM_EPS)) * lng_ref[...] + lnb_ref[...]).astype(BF16)
    ti = lax.broadcasted_iota(jnp.int32, (BLOCK, BLOCK), 0)
    si = lax.broadcasted_iota(jnp.int32, (BLOCK, BLOCK), 1)
    tril = si <= ti
    bst = bst_ref[...]
    gm = []
    for h in range(GMLP_GROUPS):
        w = jnp.where(tril, ws_ref[h], 0.0).astype(BF16)
        sp = jnp.dot(w, vn[:, h * LANES:(h + 1) * LANES], preferred_element_type=F32) + bst[:, h:h + 1]
        gm.append(u[:, h * LANES:(h + 1) * LANES] * sp)
    ss = gm[0] * gm[0]
    for a in gm[1:]:
        ss = ss + a * a
    scale = lax.rsqrt(jnp.sum(ss, axis=-1, keepdims=True) * (1.0 / GMLP_WIDTH) + NORM_EPS)
    for c, a in enumerate(gm):
        o_ref[:, ATTN_WIDTH + c * LANES:ATTN_WIDTH + (c + 1) * LANES] = (
            a * scale * gg_ref[:, c * LANES:(c + 1) * LANES]).astype(BF16)


def _mixer(qkv, z, sinks, w_sp, bst, ln_g, ln_b, g_attn, g_gmlp, batch, seq):
    t = qkv.shape[0]
    nb = seq // BLOCK
    kv_blk = Q_COLS // (2 * KV_COLS)
    cur = lambda b, n, s: (b * nb + n, 0)
    const2 = lambda b, n, s: (0, 0)
    grid_spec = pltpu.PrefetchScalarGridSpec(
        num_scalar_prefetch=1,
        grid=(batch, nb),
        in_specs=[
            pl.BlockSpec((BLOCK, Q_COLS), cur),
            pl.BlockSpec((BLOCK, 2 * KV_COLS), lambda b, n, s: (b * nb + n, kv_blk)),
            pl.BlockSpec((BLOCK, 2 * KV_COLS), lambda b, n, s: (b * nb + jnp.maximum(n - 1, 0), kv_blk)),
            pl.BlockSpec((BLOCK, 2 * GMLP_WIDTH), cur),
            pl.BlockSpec((GMLP_GROUPS, BLOCK, BLOCK), lambda b, n, s: (0, 0, 0)),
            pl.BlockSpec((BLOCK, GMLP_GROUPS), const2),
            pl.BlockSpec((1, GMLP_WIDTH), const2),
            pl.BlockSpec((1, GMLP_WIDTH), const2),
            pl.BlockSpec((1, ATTN_WIDTH), const2),
            pl.BlockSpec((1, GMLP_WIDTH), const2),
        ],
        out_specs=pl.BlockSpec((BLOCK, D_MODEL), cur),
    )
    return pl.pallas_call(
        _mixer_body,
        grid_spec=grid_spec,
        out_shape=jax.ShapeDtypeStruct((t, D_MODEL), BF16),
        compiler_params=_cparams(("arbitrary", "arbitrary")),
        name="mixer",
    )(sinks, qkv, qkv, qkv, z, w_sp, bst, ln_g, ln_b, g_attn, g_gmlp)


def _first_index_of_max(vals, lane, width):
    mx = jnp.max(vals, axis=-1, keepdims=True)
    idx = jnp.min(jnp.where(vals == mx, lane, width), axis=-1, keepdims=True)
    return mx, idx


def _outproj_body(m_ref, x_ref, mod_ref, g_ref, w_ref, wr_ref, br_ref,
                  xo_ref, h_ref, ids_ref, wt_ref, cnt_ref, carry_ref):
    i = pl.program_id(0)
    tm = x_ref.shape[0]

    @pl.when(i == 0)
    def _():
        carry_ref[...] = jnp.zeros_like(carry_ref)

    mod = mod_ref[0]
    y = jnp.dot(m_ref[...], w_ref[...], preferred_element_type=F32)
    x = x_ref[...] + (1.0 + mod[2:3, :]) * y
    xo_ref[...] = x
    h = _rms_mod(x, g_ref[...], mod[4:5, :], mod[3:4, :])
    h_ref[...] = h

    logits = jnp.dot(h.astype(BF16), wr_ref[...], preferred_element_type=F32) + br_ref[...]
    lane = lax.broadcasted_iota(jnp.int32, (tm, LANES), 1)
    neg = -jnp.inf
    gl = jnp.where(lane < N_GROUPS, logits, neg)
    ge = jnp.exp(gl - jnp.max(gl, axis=-1, keepdims=True))
    gprob = ge / jnp.sum(ge, axis=-1, keepdims=True)
    g_p, g_idx = _first_index_of_max(gprob, lane, LANES)

    base = N_GROUPS + EXPERTS_PER_GROUP * g_idx
    in_grp = (lane >= base) & (lane < base + EXPERTS_PER_GROUP)
    el = jnp.where(in_grp, logits, neg)
    ee = jnp.exp(el - jnp.max(el, axis=-1, keepdims=True))
    eprob = jnp.where(in_grp, ee / jnp.sum(ee, axis=-1, keepdims=True), -1.0)
    p1, i1 = _first_index_of_max(eprob, lane, LANES)
    eprob2 = jnp.where(lane == i1, -1.0, eprob)
    p2, i2 = _first_index_of_max(eprob2, lane, LANES)
    psum = p1 + p2
    w1 = g_p * (p1 / psum)
    w2 = g_p * (p2 / psum)
    e1 = i1 - N_GROUPS
    e2 = i2 - N_GROUPS

    oh1 = lane == e1
    oh2 = lane == e2
    onehot = jnp.where(oh1 | oh2, 1.0, 0.0)
    ri = lax.broadcasted_iota(jnp.int32, (tm, tm), 0)
    ci = lax.broadcasted_iota(jnp.int32, (tm, tm), 1)
    lower = jnp.where(ci < ri, 1.0, 0.0).astype(BF16)
    before = jnp.dot(lower, onehot.astype(BF16), preferred_element_type=F32) + carry_ref[...]
    r1 = jnp.sum(jnp.where(oh1, before, 0.0), axis=-1, keepdims=True)
    r2 = jnp.sum(jnp.where(oh2, before, 0.0), axis=-1, keepdims=True)
    carry = carry_ref[...] + jnp.sum(onehot, axis=0, keepdims=True)
    carry_ref[...] = carry
    cnt_ref[...] = jnp.broadcast_to(carry, cnt_ref.shape).astype(jnp.int32)

    ids = jnp.where(lane == 0, e1, jnp.where(lane == 1, e2, 0))
    ranks = jnp.where(lane == 2, r1, jnp.where(lane == 3, r2, 0.0)).astype(jnp.int32)
    ids_ref[...] = ids + ranks
    wt_ref[...] = jnp.where(lane == 0, w1, jnp.where(lane == 1, w2, 0.0))


def _outproj(merged, x, mod_l, g2, w_out, w_router, b_router, seq):
    t, d = x.shape
    tm = OUT_TM
    tiles_per_seq = seq // tm
    row = lambda i: (i, 0)
    const = lambda i: (0, 0)
    return pl.pallas_call(
        _outproj_body,
        grid=(t // tm,),
        in_specs=[
            pl.BlockSpec((tm, d), row),
            pl.BlockSpec((tm, d), row),
            pl.BlockSpec((1, N_MOD, d), lambda i: (i // tiles_per_seq, 0, 0)),
            pl.BlockSpec((1, d), const),
            pl.BlockSpec((d, d), const, pipeline_mode=pl.Buffered(1)),
            pl.BlockSpec((d, LANES), const),
            pl.BlockSpec((1, LANES), const),
        ],
        out_specs=[
            pl.BlockSpec((tm, d), row),
            pl.BlockSpec((tm, d), row),
            pl.BlockSpec((tm, LANES), row),
            pl.BlockSpec((tm, LANES), row),
            pl.BlockSpec((SUBLANES, LANES), const),
        ],
        out_shape=[
            jax.ShapeDtypeStruct((t, d), F32),
            jax.ShapeDtypeStruct((t, d), F32),
            jax.ShapeDtypeStruct((t, LANES), jnp.int32),
            jax.ShapeDtypeStruct((t, LANES), F32),
            jax.ShapeDtypeStruct((SUBLANES, LANES), jnp.int32),
        ],
        scratch_shapes=[pltpu.VMEM((1, LANES), F32)],
        compiler_params=_cparams(("arbitrary",)),
        name="outproj",
    )(merged, x, mod_l, g2, w_out, w_router, b_router)


MOE_N_CHUNK = 256


def _moe_body(n_slots, te_ref, nu_ref, h_hbm, first_hbm, plan_hbm, wg_ref, wu_ref, wd_ref, y_hbm,
              first_smem, plan_smem, xbuf, obuf, wgb, wub, wdb, first_sem, plan_sem, g_sem, s_sem):
    i = pl.program_id(0)
    n_used = nu_ref[0]
    tm, d = xbuf.shape[1], xbuf.shape[2]
    de = wgb.shape[1]
    cur = i % 2
    oth = 1 - cur

    def plan_copy(k, slot):
        return pltpu.make_async_copy(plan_hbm.at[k], plan_smem.at[slot], plan_sem.at[slot])

    def gather_row(tok, buf, j):
        return pltpu.make_async_copy(h_hbm.at[pl.ds(tok, 1), :], xbuf.at[buf, pl.ds(j, 1), :], g_sem.at[buf])

    def scatter_row(dst, buf, j):
        return pltpu.make_async_copy(obuf.at[buf, pl.ds(j, 1), :], y_hbm.at[pl.ds(dst, 1), :], s_sem.at[buf])

    def wait_gather(buf):
        pltpu.make_async_copy(h_hbm.at[pl.ds(0, tm), :], xbuf.at[buf], g_sem.at[buf]).wait()

    def wait_scatter(buf):
        pltpu.make_async_copy(obuf.at[buf], y_hbm.at[pl.ds(0, tm), :], s_sem.at[buf]).wait()

    @pl.when(i == 0)
    def _():
        first = pltpu.make_async_copy(first_hbm, first_smem, first_sem.at[0])
        first.start()
        plan_copy(0, 0).start()
        first.wait()

        def body(j, carry):
            gather_row(first_smem[0, j], 0, j).start()
            return carry
        lax.fori_loop(0, tm, body, 0, unroll=8)
        obuf[1] = jnp.zeros((tm, d), F32)

    @pl.when(i < n_used)
    def _():
        plan_copy(i, cur).wait()
        plan_copy(i + 1, oth).start()
        wait_gather(cur)

        @pl.when(i >= 1)
        def _():
            wait_scatter(cur)

        new_expert = jnp.logical_or(i == 0, te_ref[i] != te_ref[jnp.maximum(i - 1, 0)])

        @pl.when(new_expert)
        def _():
            wgb[...] = wg_ref[0, 0].astype(BF16)
            wub[...] = wu_ref[0, 0].astype(BF16)
            wdb[...] = wd_ref[0, 0].astype(BF16)

        copies = []
        for j in range(tm):
            copies += [(0, j), (1, j)]
        n_up = de // MOE_N_CHUNK
        n_down = d // MOE_N_CHUNK
        n_pieces = 2 * n_up + n_down
        per_piece = -(-len(copies) // n_pieces)

        def issue(piece):
            for kind, j in copies[piece * per_piece:(piece + 1) * per_piece]:
                if kind == 0:
                    gather_row(plan_smem[cur, 0, j], oth, j).start()
                else:
                    scatter_row(plan_smem[cur, 1, j], oth, j).start()

        xb = xbuf[cur].astype(BF16)
        piece = 0
        acts = []
        for n in range(n_up):
            cols = slice(n * MOE_N_CHUNK, (n + 1) * MOE_N_CHUNK)
            g = jnp.dot(xb, wgb[:, cols], preferred_element_type=F32)
            issue(piece)
            u = jnp.dot(xb, wub[:, cols], preferred_element_type=F32)
            issue(piece + 1)
            piece += 2
            acts.append(((g * jax.nn.sigmoid(g)) * u).astype(BF16))
        a = jnp.concatenate(acts, axis=1)
        for n in range(n_down):
            cols = slice(n * MOE_N_CHUNK, (n + 1) * MOE_N_CHUNK)
            obuf[cur, :, cols] = jnp.dot(a, wdb[:, cols], preferred_element_type=F32)
            issue(piece)
            piece += 1

    @pl.when(i == n_used)
    def _():
        plan_copy(i, cur).wait()
        wait_gather(cur)
        wait_scatter(cur)

        def body(j, carry):
            scatter_row(plan_smem[cur, 1, j], oth, j).start()
            return carry
        lax.fori_loop(0, tm, body, 0, unroll=8)
        wait_scatter(oth)
        obuf[cur] = jnp.zeros((tm, d), F32)
        fill = pltpu.make_async_copy(obuf.at[cur], y_hbm.at[pl.ds(n_slots, tm), :], s_sem.at[cur])
        fill.start()
        fill.wait()


def _moe(layer, h2, first_src, plan, tile_expert, n_used, w_gate, w_up, w_down):
    t, d = h2.shape
    n_slots = 2 * t
    steps, _, tm = plan.shape
    de = w_gate.shape[-1]
    grid_spec = pltpu.PrefetchScalarGridSpec(
        num_scalar_prefetch=2,
        grid=(steps,),
        in_specs=[
            pl.BlockSpec(memory_space=pl.ANY),
            pl.BlockSpec(memory_space=pl.ANY),
            pl.BlockSpec(memory_space=pl.ANY),
            pl.BlockSpec((1, 1, d, de), lambda i, te, nu: (layer, te[i], 0, 0)),
            pl.BlockSpec((1, 1, d, de), lambda i, te, nu: (layer, te[i], 0, 0)),
            pl.BlockSpec((1, 1, de, d), lambda i, te, nu: (layer, te[i], 0, 0)),
        ],
        out_specs=pl.BlockSpec(memory_space=pl.ANY),
        scratch_shapes=[
            pltpu.SMEM((1, tm), jnp.int32),
            pltpu.SMEM((2, 2, tm), jnp.int32),
            pltpu.VMEM((2, tm, d), F32),
            pltpu.VMEM((2, tm, d), F32),
            pltpu.VMEM((d, de), BF16),
            pltpu.VMEM((d, de), BF16),
            pltpu.VMEM((de, d), BF16),
            pltpu.SemaphoreType.DMA((1,)),
            pltpu.SemaphoreType.DMA((2,)),
            pltpu.SemaphoreType.DMA((2,)),
            pltpu.SemaphoreType.DMA((2,)),
        ],
    )
    return pl.pallas_call(
        functools.partial(_moe_body, n_slots),
        grid_spec=grid_spec,
        out_shape=jax.ShapeDtypeStruct((n_slots + tm, d), F32),
        compiler_params=_cparams(("arbitrary",)),
        name="moe",
    )(tile_expert, n_used, h2, first_src, plan, w_gate, w_up, w_down)


def _route_plan(ids, counts, t):
    tm = MOE_TM
    nt = (2 * t) // tm + N_EXPERTS
    e = ids[:, 0:2]
    rank = ids[:, 2:4]
    cnt = counts[0, :N_EXPERTS]
    tiles = (cnt + tm - 1) // tm
    tile_end = jnp.cumsum(tiles)
    off = (tile_end - tiles) * tm
    pos = off[e] + rank
    slot_ids = jnp.arange(t, dtype=jnp.int32)[:, None] + t * jnp.arange(2, dtype=jnp.int32)[None, :]
    slot_of_row = jnp.full((nt * tm,), -1, jnp.int32).at[pos.reshape(-1)].set(
        slot_ids.reshape(-1), unique_indices=True)
    n_used = tile_end[-1]
    steps = jnp.minimum(jnp.arange(nt + 1, dtype=jnp.int32), n_used - 1)
    tile_expert = jnp.sum(tile_end[None, :] <= steps[:, None], axis=1).astype(jnp.int32)
    valid = slot_of_row >= 0
    src_tok = jnp.where(valid, slot_of_row % t, 0).reshape(nt, tm)
    dump = 2 * t + jnp.arange(tm, dtype=jnp.int32)
    dst_row = jnp.where(valid.reshape(nt, tm), slot_of_row.reshape(nt, tm), dump[None, :])
    plan_src = jnp.concatenate([src_tok[1:], jnp.zeros((2, tm), jnp.int32)], axis=0)
    plan_dst = jnp.concatenate([dump[None, :], dst_row], axis=0)
    plan = jnp.stack([plan_src, plan_dst], axis=1)
    return src_tok[0:1], plan, tile_expert, n_used.reshape(1).astype(jnp.int32)


def _final_body(x_ref, y0_ref, y1_ref, wt_ref, pmod_ref, g_ref, o_ref):
    x = _combine(x_ref[...], y0_ref, y1_ref, wt_ref, pmod_ref[0, 5:6, :])
    ms = jnp.mean(x * x, axis=-1, keepdims=True)
    o_ref[...] = x * lax.rsqrt(ms + NORM_EPS) * g_ref[...]


def _final(x, y2, wts, pmod, g_final, seq):
    t, d = x.shape
    tm = FINAL_TM
    tiles_per_seq = seq // tm
    row = lambda i: (i, 0)
    return pl.pallas_call(
        _final_body,
        grid=(t // tm,),
        in_specs=[
            pl.BlockSpec((tm, d), row),
            pl.BlockSpec((tm, d), row),
            pl.BlockSpec((tm, d), lambda i: (i + t // tm, 0)),
            pl.BlockSpec((tm, LANES), row),
            pl.BlockSpec((1, N_MOD, d), lambda i: (i // tiles_per_seq, 0, 0)),
            pl.BlockSpec((1, d), lambda i: (0, 0)),
        ],
        out_specs=pl.BlockSpec((tm, d), row),
        out_shape=jax.ShapeDtypeStruct((t, d), F32),
        compiler_params=_cparams(("arbitrary",)),
        name="final",
    )(x, y2, y2, wts, pmod, g_final)


def _rotary_tables(positions):
    half = ROT_DIM // 2
    inv_freq = ROPE_THETA ** (-(jnp.arange(0, ROT_DIM, 2, dtype=F32) / ROT_DIM))
    ang = positions.reshape(-1).astype(F32)[:, None] * inv_freq
    cos, sin = jnp.cos(ang), jnp.sin(ang)
    t = ang.shape[0]
    pad = HEAD_DIM - ROT_DIM
    cos_h = jnp.concatenate([cos, cos, jnp.ones((t, pad), F32)], axis=1)
    sa_h = jnp.concatenate([-sin, jnp.zeros((t, HEAD_DIM - half), F32)], axis=1)
    sb_h = jnp.concatenate([jnp.zeros((t, half), F32), sin, jnp.zeros((t, pad), F32)], axis=1)
    rep = LANES // HEAD_DIM
    return jnp.tile(cos_h, (1, rep)), jnp.tile(sa_h, (1, rep)), jnp.tile(sb_h, (1, rep))


def kernel(x, c, positions, w_ada, b_ada, g_norm1, w_in, b_in, sinks, ln_v_g, ln_v_b, w_spatial, b_spatial,
           g_attn_out, g_gmlp_out, w_out, g_norm2, w_router_group, b_router_group, w_router_expert,
           b_router_expert, w_e_gate, w_e_up, w_e_down, g_final):
    batch, seq, d = x.shape
    depth = w_ada.shape[0]
    t = batch * seq
    assert d == D_MODEL and seq % max(PROJ_TM, OUT_TM, FINAL_TM, BLOCK) == 0 and (2 * t) % MOE_TM == 0

    c_pad = jnp.zeros((SUBLANES, d), F32).at[:batch].set(c)
    mod = _adaln(c_pad, w_ada, b_ada).reshape(depth, SUBLANES, N_MOD, d)
    cos_t, sa_t, sb_t = _rotary_tables(positions)

    n_route = N_GROUPS + N_EXPERTS
    w_router = jnp.concatenate(
        [w_router_group, w_router_expert, jnp.zeros((depth, d, LANES - n_route), F32)], axis=-1)
    b_router = jnp.concatenate(
        [b_router_group, b_router_expert, jnp.zeros((depth, LANES - n_route), F32)], axis=-1)

    xt = x.reshape(t, d)
    prev = None
    for l in range(depth):
        xt, qkv, z = _proj(xt, prev, mod[l], g_norm1[l][None], w_in[l].astype(BF16), b_in[l][None],
                           cos_t, sa_t, sb_t, seq)
        merged = _mixer(qkv, z, sinks[l], w_spatial[l], b_spatial[l].T, ln_v_g[l][None], ln_v_b[l][None],
                        g_attn_out[l][None], g_gmlp_out[l][None], batch, seq)
        xt, h2, ids, wts, counts = _outproj(merged, xt, mod[l], g_norm2[l][None], w_out[l].astype(BF16),
                                            w_router[l].astype(BF16), b_router[l][None], seq)
        first_src, plan, tile_expert, n_used = _route_plan(ids, counts, t)
        y2 = _moe(l, h2, first_src, plan, tile_expert, n_used, w_e_gate, w_e_up, w_e_down)
        prev = (y2, wts, mod[l])
    out = _final(xt, prev[0], prev[1], prev[2], g_final[None], seq)
    return out.reshape(batch, seq, d)
```

```python
import functools

import jax
import jax.numpy as jnp
from jax import lax
from jax.experimental import pallas as pl
from jax.experimental.pallas import tpu as pltpu

F32 = jnp.float32
BF16 = jnp.bfloat16

D_MODEL = 2048
ATTN_WIDTH = D_MODEL // 2
HEAD_DIM = 64
N_Q_HEADS = ATTN_WIDTH // HEAD_DIM
N_KV_HEADS = 2
GQA_GROUP = N_Q_HEADS // N_KV_HEADS
WINDOW = 128
BLOCK = 128
ROT_DIM = HEAD_DIM // 4
ROPE_THETA = 500000.0
GMLP_WIDTH = D_MODEL - ATTN_WIDTH
GMLP_GROUPS = 8
GMLP_GROUP_DIM = GMLP_WIDTH // GMLP_GROUPS
Q_COLS = N_Q_HEADS * HEAD_DIM
KV_COLS = N_KV_HEADS * HEAD_DIM
QKV_COLS = Q_COLS + 2 * KV_COLS
IN_COLS = QKV_COLS + 2 * GMLP_WIDTH
N_GROUPS = 4
EXPERTS_PER_GROUP = 8
N_EXPERTS = N_GROUPS * EXPERTS_PER_GROUP
D_EXPERT = D_MODEL // 4
NORM_EPS = 1e-6
N_MOD = 6

LANES = 128
SUBLANES = 8
VMEM_LIMIT_BYTES = 56 * 1024 * 1024

SLAB = D_MODEL // LANES


def _to_slabs(ref, base, x):
    n = x.shape[0]
    for c in range(SLAB):
        ref[pl.ds(base + c, n, stride=SLAB), :] = x[:, c * LANES:(c + 1) * LANES]


def _from_slabs(ref, base, n, dtype=F32):
    return jnp.concatenate([ref[pl.ds(base + c, n, stride=SLAB), :].astype(dtype) for c in range(SLAB)], axis=1)

ADA_TN = 1024
PROJ_TM = 256
OUT_TM = 256
MOE_TM = 256
FINAL_TM = 512


def _cparams(sem):
    return pltpu.CompilerParams(dimension_semantics=sem, vmem_limit_bytes=VMEM_LIMIT_BYTES)


def _adaln_body(c_ref, w_ref, b_ref, o_ref):
    c = c_ref[...]
    ca = (c * jax.nn.sigmoid(c)).astype(BF16)
    w = w_ref[0].astype(BF16)
    o_ref[0] = jnp.dot(ca, w, preferred_element_type=F32) + b_ref[0]


def _adaln(c_pad, w_ada, b_ada):
    depth, d, n = w_ada.shape
    rows = c_pad.shape[0]
    return pl.pallas_call(
        _adaln_body,
        grid=(depth, n // ADA_TN),
        in_specs=[
            pl.BlockSpec((rows, d), lambda l, j: (0, 0)),
            pl.BlockSpec((1, d, ADA_TN), lambda l, j: (l, 0, j)),
            pl.BlockSpec((1, 1, ADA_TN), lambda l, j: (l, 0, j)),
        ],
        out_specs=pl.BlockSpec((1, rows, ADA_TN), lambda l, j: (l, 0, j)),
        out_shape=jax.ShapeDtypeStruct((depth, rows, n), F32),
        compiler_params=_cparams(("arbitrary", "arbitrary")),
        name="adaln",
    )(c_pad, w_ada, b_ada.reshape(depth, 1, n))


def _rms_mod(x, g, sc, sh):
    ms = jnp.mean(x * x, axis=-1, keepdims=True)
    return (x * lax.rsqrt(ms + NORM_EPS) * g) * (1.0 + sc) + sh


def _combine(x, y0_ref, y1_ref, wt_ref, gate):
    wt = wt_ref[...]
    n = x.shape[0]
    y = wt[:, 0:1] * _from_slabs(y0_ref, 0, n) + wt[:, 1:2] * _from_slabs(y1_ref, 0, n)
    return x + (1.0 + gate) * y


def _gelu_exact(z):
    return 0.5 * z * (1.0 + lax.erf(z * (2.0 ** -0.5)))


def _proj_body(has_prev, *refs):
    if has_prev:
        (x_ref, y0_ref, y1_ref, wt_ref, pmod_ref, mod_ref, g_ref, w_ref, b_ref, cos_ref, sa_ref, sb_ref,
         xo_ref, qkv_ref, z_ref) = refs
        x = _combine(x_ref[...], y0_ref, y1_ref, wt_ref, pmod_ref[0, 5:6, :])
        xo_ref[...] = x
    else:
        x_ref, mod_ref, g_ref, w_ref, b_ref, cos_ref, sa_ref, sb_ref, qkv_ref, z_ref = refs
        x = x_ref[...]
    mod = mod_ref[0]
    h = _rms_mod(x, g_ref[...], mod[1:2, :], mod[0:1, :]).astype(BF16)

    qkv = jnp.dot(h, w_ref[:, :QKV_COLS], preferred_element_type=F32) + b_ref[:, :QKV_COLS]
    cos, sa, sb = cos_ref[...], sa_ref[...], sb_ref[...]
    n_rot = (Q_COLS + KV_COLS) // LANES
    for j in range(QKV_COLS // LANES):
        ch = qkv[:, j * LANES:(j + 1) * LANES]
        if j < n_rot:
            ch = ch * cos + pltpu.roll(ch, LANES - ROT_DIM // 2, 1) * sa + pltpu.roll(ch, ROT_DIM // 2, 1) * sb
        if j < Q_COLS // LANES:
            ch = ch * (HEAD_DIM ** -0.5)
        qkv_ref[:, j * LANES:(j + 1) * LANES] = ch.astype(BF16)

    z = jnp.dot(h, w_ref[:, QKV_COLS:], preferred_element_type=F32) + b_ref[:, QKV_COLS:]
    z_ref[...] = _gelu_exact(z)


def _proj(x, prev, mod_l, g1, w_in, b_in, cos_t, sa_t, sb_t, seq):
    t, d = x.shape
    tm = PROJ_TM
    tiles_per_seq = seq // tm
    row = lambda i: (i, 0)
    const = lambda i: (0, 0)
    modmap = lambda i: (i // tiles_per_seq, 0, 0)
    in_specs = [pl.BlockSpec((tm, d), row)]
    args = [x]
    if prev is not None:
        y2, wts, pmod = prev
        in_specs += [pl.BlockSpec((tm * SLAB, LANES), row),
                     pl.BlockSpec((tm * SLAB, LANES), lambda i: (i + t // tm, 0)),
                     pl.BlockSpec((tm, LANES), row), pl.BlockSpec((1, N_MOD, d), modmap)]
        args += [y2, y2, wts, pmod]
    in_specs += [
        pl.BlockSpec((1, N_MOD, d), modmap),
        pl.BlockSpec((1, d), const),
        pl.BlockSpec((d, IN_COLS), const, pipeline_mode=pl.Buffered(1)),
        pl.BlockSpec((1, IN_COLS), const),
        pl.BlockSpec((tm, LANES), row),
        pl.BlockSpec((tm, LANES), row),
        pl.BlockSpec((tm, LANES), row),
    ]
    args += [mod_l, g1, w_in, b_in, cos_t, sa_t, sb_t]
    out_specs = [pl.BlockSpec((tm, QKV_COLS), row), pl.BlockSpec((tm, 2 * GMLP_WIDTH), row)]
    out_shape = [jax.ShapeDtypeStruct((t, QKV_COLS), BF16), jax.ShapeDtypeStruct((t, 2 * GMLP_WIDTH), F32)]
    if prev is not None:
        out_specs = [pl.BlockSpec((tm, d), row)] + out_specs
        out_shape = [jax.ShapeDtypeStruct((t, d), F32)] + out_shape
    outs = pl.pallas_call(
        functools.partial(_proj_body, prev is not None),
        grid=(t // tm,),
        in_specs=in_specs,
        out_specs=out_specs,
        out_shape=out_shape,
        compiler_params=_cparams(("arbitrary",)),
        name="proj",
    )(*args)
    if prev is None:
        return (x,) + tuple(outs)
    return tuple(outs)


def _mixer_body(sink_ref, q_ref, kvc_ref, kvp_ref, z_ref, ws_ref, bst_ref, lng_ref, lnb_ref, ga_ref, gg_ref,
                o_ref):
    n = pl.program_id(1)
    q = q_ref[...]
    kvc = kvc_ref[...]
    kvp = kvp_ref[...]
    kcat = jnp.concatenate([kvp[:, :KV_COLS], kvc[:, :KV_COLS]], axis=0)
    vcat = jnp.concatenate([kvp[:, KV_COLS:], kvc[:, KV_COLS:]], axis=0)

    rows = GQA_GROUP * BLOCK
    lane = lax.broadcasted_iota(jnp.int32, (BLOCK, LANES), 1)
    lo = lane < HEAD_DIM
    qi = lax.broadcasted_iota(jnp.int32, (rows, 2 * BLOCK), 0) & (BLOCK - 1)
    kj = lax.broadcasted_iota(jnp.int32, (rows, 2 * BLOCK), 1)
    rel = qi + BLOCK - kj
    valid = (rel >= 0) & (rel < WINDOW) & ((n > 0) | (kj >= BLOCK))
    lane2 = lax.broadcasted_iota(jnp.int32, (2 * BLOCK, LANES), 1)
    lo2 = lane2 < HEAD_DIM
    zero_q = jnp.zeros((BLOCK, LANES), BF16)
    zero_v = jnp.zeros((2 * BLOCK, LANES), BF16)

    attn = []
    chunks_per_kv = GQA_GROUP * HEAD_DIM // LANES
    for hk in range(N_KV_HEADS):
        ksl = kcat[:, hk * HEAD_DIM:(hk + 1) * HEAD_DIM]
        vsl = vcat[:, hk * HEAD_DIM:(hk + 1) * HEAD_DIM]
        kk = jnp.concatenate([ksl, ksl], axis=1)
        vd = jnp.concatenate([vsl, vsl], axis=1)
        vv = jnp.concatenate([jnp.where(lo2, vd, zero_v), jnp.where(lo2, zero_v, vd)], axis=0)
        parts, sinks = [], []
        for j in range(chunks_per_kv):
            c = hk * chunks_per_kv + j
            qc = q[:, c * LANES:(c + 1) * LANES]
            parts += [jnp.where(lo, qc, zero_q), jnp.where(lo, zero_q, qc)]
            sinks += [jnp.full((BLOCK, 1), sink_ref[2 * c], F32), jnp.full((BLOCK, 1), sink_ref[2 * c + 1], F32)]
        lhs = jnp.concatenate(parts, axis=0)
        sink = jnp.concatenate(sinks, axis=0)
        s = lax.dot_general(lhs, kk, (((1,), (1,)), ((), ())), preferred_element_type=F32)
        s = jnp.where(valid, s, -jnp.inf)
        m = jnp.maximum(jnp.max(s, axis=-1, keepdims=True), sink)
        p = jnp.exp(s - m)
        denom = jnp.sum(p, axis=-1, keepdims=True) + jnp.exp(sink - m)
        pn = (p * (1.0 / denom)).astype(BF16)
        for j in range(chunks_per_kv):
            pc = jnp.concatenate([pn[(2 * j) * BLOCK:(2 * j + 1) * BLOCK], pn[(2 * j + 1) * BLOCK:(2 * j + 2) * BLOCK]],
                                 axis=1)
            attn.append(jnp.dot(pc, vv, preferred_element_type=F32))

    ss = attn[0] * attn[0]
    for a in attn[1:]:
        ss = ss + a * a
    scale = lax.rsqrt(jnp.sum(ss, axis=-1, keepdims=True) * (1.0 / ATTN_WIDTH) + NORM_EPS)
    for c, a in enumerate(attn):
        o_ref[:, c * LANES:(c + 1) * LANES] = (a * scale * ga_ref[:, c * LANES:(c + 1) * LANES]).astype(BF16)

    u = z_ref[:, :GMLP_WIDTH]
    v = z_ref[:, GMLP_WIDTH:]
    mu = jnp.mean(v, axis=-1, keepdims=True)
    vc = v - mu
    var = jnp.mean(vc * vc, axis=-1, keepdims=True)
    vn = ((vc * lax.rsqrt(var + NORM_EPS)) * lng_ref[...] + lnb_ref[...]).astype(BF16)
    ti = lax.broadcasted_iota(jnp.int32, (BLOCK, BLOCK), 0)
    si = lax.broadcasted_iota(jnp.int32, (BLOCK, BLOCK), 1)
    tril = si <= ti
    bst = bst_ref[...]
    gm = []
    for h in range(GMLP_GROUPS):
        w = jnp.where(tril, ws_ref[h], 0.0).astype(BF16)
        sp = jnp.dot(w, vn[:, h * LANES:(h + 1) * LANES], preferred_element_type=F32) + bst[:, h:h + 1]
        gm.append(u[:, h * LANES:(h + 1) * LANES] * sp)
    ss = gm[0] * gm[0]
    for a in gm[1:]:
        ss = ss + a * a
    scale = lax.rsqrt(jnp.sum(ss, axis=-1, keepdims=True) * (1.0 / GMLP_WIDTH) + NORM_EPS)
    for c, a in enumerate(gm):
        o_ref[:, ATTN_WIDTH + c * LANES:ATTN_WIDTH + (c + 1) * LANES] = (
            a * scale * gg_ref[:, c * LANES:(c + 1) * LANES]).astype(BF16)


def _mixer(qkv, z, sinks, w_sp, bst, ln_g, ln_b, g_attn, g_gmlp, batch, seq):
    t = qkv.shape[0]
    nb = seq // BLOCK
    kv_blk = Q_COLS // (2 * KV_COLS)
    cur = lambda b, n, s: (b * nb + n, 0)
    const2 = lambda b, n, s: (0, 0)
    grid_spec = pltpu.PrefetchScalarGridSpec(
        num_scalar_prefetch=1,
        grid=(batch, nb),
        in_specs=[
            pl.BlockSpec((BLOCK, Q_COLS), cur),
            pl.BlockSpec((BLOCK, 2 * KV_COLS), lambda b, n, s: (b * nb + n, kv_blk)),
            pl.BlockSpec((BLOCK, 2 * KV_COLS), lambda b, n, s: (b * nb + jnp.maximum(n - 1, 0), kv_blk)),
            pl.BlockSpec((BLOCK, 2 * GMLP_WIDTH), cur),
            pl.BlockSpec((GMLP_GROUPS, BLOCK, BLOCK), lambda b, n, s: (0, 0, 0)),
            pl.BlockSpec((BLOCK, GMLP_GROUPS), const2),
            pl.BlockSpec((1, GMLP_WIDTH), const2),
            pl.BlockSpec((1, GMLP_WIDTH), const2),
            pl.BlockSpec((1, ATTN_WIDTH), const2),
            pl.BlockSpec((1, GMLP_WIDTH), const2),
        ],
        out_specs=pl.BlockSpec((BLOCK, D_MODEL), cur),
    )
    return pl.pallas_call(
        _mixer_body,
        grid_spec=grid_spec,
        out_shape=jax.ShapeDtypeStruct((t, D_MODEL), BF16),
        compiler_params=_cparams(("arbitrary", "arbitrary")),
        name="mixer",
    )(sinks, qkv, qkv, qkv, z, w_sp, bst, ln_g, ln_b, g_attn, g_gmlp)


def _first_index_of_max(vals, lane, width):
    mx = jnp.max(vals, axis=-1, keepdims=True)
    idx = jnp.min(jnp.where(vals == mx, lane, width), axis=-1, keepdims=True)
    return mx, idx


def _outproj_body(m_ref, x_ref, mod_ref, g_ref, w_ref, wr_ref, br_ref,
                  xo_ref, h_ref, ids_ref, wt_ref, cnt_ref, carry_ref):
    i = pl.program_id(0)
    tm = x_ref.shape[0]

    @pl.when(i == 0)
    def _():
        carry_ref[...] = jnp.zeros_like(carry_ref)

    mod = mod_ref[0]
    y = jnp.dot(m_ref[...], w_ref[...], preferred_element_type=F32)
    x = x_ref[...] + (1.0 + mod[2:3, :]) * y
    xo_ref[...] = x
    h = _rms_mod(x, g_ref[...], mod[4:5, :], mod[3:4, :])
    _to_slabs(h_ref, 0, h)

    logits = jnp.dot(h.astype(BF16), wr_ref[...], preferred_element_type=F32) + br_ref[...]
    lane = lax.broadcasted_iota(jnp.int32, (tm, LANES), 1)
    neg = -jnp.inf
    gl = jnp.where(lane < N_GROUPS, logits, neg)
    ge = jnp.exp(gl - jnp.max(gl, axis=-1, keepdims=True))
    gprob = ge / jnp.sum(ge, axis=-1, keepdims=True)
    g_p, g_idx = _first_index_of_max(gprob, lane, LANES)

    base = N_GROUPS + EXPERTS_PER_GROUP * g_idx
    in_grp = (lane >= base) & (lane < base + EXPERTS_PER_GROUP)
    el = jnp.where(in_grp, logits, neg)
    ee = jnp.exp(el - jnp.max(el, axis=-1, keepdims=True))
    eprob = jnp.where(in_grp, ee / jnp.sum(ee, axis=-1, keepdims=True), -1.0)
    p1, i1 = _first_index_of_max(eprob, lane, LANES)
    eprob2 = jnp.where(lane == i1, -1.0, eprob)
    p2, i2 = _first_index_of_max(eprob2, lane, LANES)
    psum = p1 + p2
    w1 = g_p * (p1 / psum)
    w2 = g_p * (p2 / psum)
    e1 = i1 - N_GROUPS
    e2 = i2 - N_GROUPS

    oh1 = lane == e1
    oh2 = lane == e2
    onehot = jnp.where(oh1 | oh2, 1.0, 0.0)
    ri = lax.broadcasted_iota(jnp.int32, (tm, tm), 0)
    ci = lax.broadcasted_iota(jnp.int32, (tm, tm), 1)
    lower = jnp.where(ci < ri, 1.0, 0.0).astype(BF16)
    before = jnp.dot(lower, onehot.astype(BF16), preferred_element_type=F32) + carry_ref[...]
    r1 = jnp.sum(jnp.where(oh1, before, 0.0), axis=-1, keepdims=True)
    r2 = jnp.sum(jnp.where(oh2, before, 0.0), axis=-1, keepdims=True)
    carry = carry_ref[...] + jnp.sum(onehot, axis=0, keepdims=True)
    carry_ref[...] = carry
    cnt_ref[...] = jnp.broadcast_to(carry, cnt_ref.shape).astype(jnp.int32)

    ids = jnp.where(lane == 0, e1, jnp.where(lane == 1, e2, 0))
    ranks = jnp.where(lane == 2, r1, jnp.where(lane == 3, r2, 0.0)).astype(jnp.int32)
    ids_ref[...] = ids + ranks
    wt_ref[...] = jnp.where(lane == 0, w1, jnp.where(lane == 1, w2, 0.0))


def _outproj(merged, x, mod_l, g2, w_out, w_router, b_router, seq):
    t, d = x.shape
    tm = OUT_TM
    tiles_per_seq = seq // tm
    row = lambda i: (i, 0)
    const = lambda i: (0, 0)
    return pl.pallas_call(
        _outproj_body,
        grid=(t // tm,),
        in_specs=[
            pl.BlockSpec((tm, d), row),
            pl.BlockSpec((tm, d), row),
            pl.BlockSpec((1, N_MOD, d), lambda i: (i // tiles_per_seq, 0, 0)),
            pl.BlockSpec((1, d), const),
            pl.BlockSpec((d, d), const, pipeline_mode=pl.Buffered(1)),
            pl.BlockSpec((d, LANES), const),
            pl.BlockSpec((1, LANES), const),
        ],
        out_specs=[
            pl.BlockSpec((tm, d), row),
            pl.BlockSpec((tm * SLAB, LANES), row),
            pl.BlockSpec((tm, LANES), row),
            pl.BlockSpec((tm, LANES), row),
            pl.BlockSpec((SUBLANES, LANES), const),
        ],
        out_shape=[
            jax.ShapeDtypeStruct((t, d), F32),
            jax.ShapeDtypeStruct((t * SLAB, LANES), F32),
            jax.ShapeDtypeStruct((t, LANES), jnp.int32),
            jax.ShapeDtypeStruct((t, LANES), F32),
            jax.ShapeDtypeStruct((SUBLANES, LANES), jnp.int32),
        ],
        scratch_shapes=[pltpu.VMEM((1, LANES), F32)],
        compiler_params=_cparams(("arbitrary",)),
        name="outproj",
    )(merged, x, mod_l, g2, w_out, w_router, b_router)


MOE_N_CHUNK = 256


def _moe_body(n_slots, te_ref, nu_ref, h_hbm, first_hbm, plan_hbm, wg_ref, wu_ref, wd_ref, y_hbm,
              first_smem, plan_smem, xbuf, obuf, wgb, wub, wdb, first_sem, plan_sem, g_sem, s_sem):
    i = pl.program_id(0)
    n_used = nu_ref[0]
    d, de = wgb.shape
    rows = xbuf.shape[0] // 2
    tm = rows // SLAB
    cur = i % 2
    oth = 1 - cur

    def plan_copy(k, slot):
        return pltpu.make_async_copy(plan_hbm.at[k], plan_smem.at[slot], plan_sem.at[slot])

    def tile_rows(buf):
        return pl.ds(pl.multiple_of(buf * rows, rows), rows)

    def slab(buf, j):
        return pl.ds(pl.multiple_of(buf * rows + j * SLAB, SLAB), SLAB)

    def gather_row(src, buf, j):
        return pltpu.make_async_copy(h_hbm.at[pl.ds(pl.multiple_of(src, SLAB), SLAB), :], xbuf.at[slab(buf, j), :],
                                     g_sem.at[buf])

    def scatter_row(dst, buf, j):
        return pltpu.make_async_copy(obuf.at[slab(buf, j), :], y_hbm.at[pl.ds(pl.multiple_of(dst, SLAB), SLAB), :],
                                     s_sem.at[buf])

    def wait_gather(buf):
        pltpu.make_async_copy(h_hbm.at[pl.ds(0, rows), :], xbuf.at[tile_rows(buf), :], g_sem.at[buf]).wait()

    def wait_scatter(buf):
        pltpu.make_async_copy(obuf.at[tile_rows(buf), :], y_hbm.at[pl.ds(0, rows), :], s_sem.at[buf]).wait()

    @pl.when(i == 0)
    def _():
        first = pltpu.make_async_copy(first_hbm, first_smem, first_sem.at[0])
        first.start()
        plan_copy(0, 0).start()
        first.wait()

        def body(j, carry):
            gather_row(first_smem[0, j], 0, j).start()
            return carry
        lax.fori_loop(0, tm, body, 0, unroll=8)
        obuf[pl.ds(rows, rows), :] = jnp.zeros((rows, LANES), F32)

    @pl.when(i < n_used)
    def _():
        plan_copy(i, cur).wait()
        plan_copy(i + 1, oth).start()
        wait_gather(cur)

        @pl.when(i >= 1)
        def _():
            wait_scatter(cur)

        new_expert = jnp.logical_or(i == 0, te_ref[i] != te_ref[jnp.maximum(i - 1, 0)])

        @pl.when(new_expert)
        def _():
            wgb[...] = wg_ref[0, 0].astype(BF16)
            wub[...] = wu_ref[0, 0].astype(BF16)
            wdb[...] = wd_ref[0, 0].astype(BF16)

        copies = []
        for j in range(tm):
            copies += [(0, j), (1, j)]
        n_up = de // MOE_N_CHUNK
        n_down = d // MOE_N_CHUNK
        n_pieces = 2 * n_up + n_down
        per_piece = -(-len(copies) // n_pieces)

        def issue(piece):
            for kind, j in copies[piece * per_piece:(piece + 1) * per_piece]:
                if kind == 0:
                    gather_row(plan_smem[cur, 0, j], oth, j).start(priority=j % 2)
                else:
                    scatter_row(plan_smem[cur, 1, j], oth, j).start(priority=(j + 1) % 2)

        base = pl.multiple_of(cur * rows, rows)
        xb = _from_slabs(xbuf, base, tm, BF16)
        piece = 0
        acts = []
        for n in range(n_up):
            cols = slice(n * MOE_N_CHUNK, (n + 1) * MOE_N_CHUNK)
            g = jnp.dot(xb, wgb[:, cols], preferred_element_type=F32)
            issue(piece)
            u = jnp.dot(xb, wub[:, cols], preferred_element_type=F32)
            issue(piece + 1)
            piece += 2
            acts.append(((g * jax.nn.sigmoid(g)) * u).astype(BF16))
        a = jnp.concatenate(acts, axis=1)
        for n in range(n_down):
            cols = slice(n * MOE_N_CHUNK, (n + 1) * MOE_N_CHUNK)
            y = jnp.dot(a, wdb[:, cols], preferred_element_type=F32)
            for c in range(MOE_N_CHUNK // LANES):
                obuf[pl.ds(base + n * (MOE_N_CHUNK // LANES) + c, tm, stride=SLAB), :] = (
                    y[:, c * LANES:(c + 1) * LANES])
            issue(piece)
            piece += 1

    @pl.when(i == n_used)
    def _():
        plan_copy(i, cur).wait()
        wait_gather(cur)
        wait_scatter(cur)

        def body(j, carry):
            scatter_row(plan_smem[cur, 1, j], oth, j).start()
            return carry
        lax.fori_loop(0, tm, body, 0, unroll=8)
        wait_scatter(oth)
        obuf[tile_rows(cur), :] = jnp.zeros((rows, LANES), F32)
        fill = pltpu.make_async_copy(obuf.at[tile_rows(cur), :], y_hbm.at[pl.ds(n_slots * SLAB, rows), :],
                                     s_sem.at[cur])
        fill.start()
        fill.wait()


def _moe(layer, h2, first_src, plan, tile_expert, n_used, w_gate, w_up, w_down):
    t = h2.shape[0] // SLAB
    n_slots = 2 * t
    steps, _, tm = plan.shape
    d, de = w_gate.shape[-2:]
    grid_spec = pltpu.PrefetchScalarGridSpec(
        num_scalar_prefetch=2,
        grid=(steps,),
        in_specs=[
            pl.BlockSpec(memory_space=pl.ANY),
            pl.BlockSpec(memory_space=pl.ANY),
            pl.BlockSpec(memory_space=pl.ANY),
            pl.BlockSpec((1, 1, d, de), lambda i, te, nu: (layer, te[i], 0, 0)),
            pl.BlockSpec((1, 1, d, de), lambda i, te, nu: (layer, te[i], 0, 0)),
            pl.BlockSpec((1, 1, de, d), lambda i, te, nu: (layer, te[i], 0, 0)),
        ],
        out_specs=pl.BlockSpec(memory_space=pl.ANY),
        scratch_shapes=[
            pltpu.SMEM((1, tm), jnp.int32),
            pltpu.SMEM((2, 2, tm), jnp.int32),
            pltpu.VMEM((2 * tm * SLAB, LANES), F32),
            pltpu.VMEM((2 * tm * SLAB, LANES), F32),
            pltpu.VMEM((d, de), BF16),
            pltpu.VMEM((d, de), BF16),
            pltpu.VMEM((de, d), BF16),
            pltpu.SemaphoreType.DMA((1,)),
            pltpu.SemaphoreType.DMA((2,)),
            pltpu.SemaphoreType.DMA((2,)),
            pltpu.SemaphoreType.DMA((2,)),
        ],
    )
    return pl.pallas_call(
        functools.partial(_moe_body, n_slots),
        grid_spec=grid_spec,
        out_shape=jax.ShapeDtypeStruct(((n_slots + tm) * SLAB, LANES), F32),
        compiler_params=_cparams(("arbitrary",)),
        name="moe",
    )(tile_expert, n_used, h2, first_src, plan, w_gate, w_up, w_down)


def _route_plan(ids, counts, t):
    tm = MOE_TM
    nt = (2 * t) // tm + N_EXPERTS
    e = ids[:, 0:2]
    rank = ids[:, 2:4]
    cnt = counts[0, :N_EXPERTS]
    tiles = (cnt + tm - 1) // tm
    tile_end = jnp.cumsum(tiles)
    off = (tile_end - tiles) * tm
    pos = off[e] + rank
    slot_ids = jnp.arange(t, dtype=jnp.int32)[:, None] + t * jnp.arange(2, dtype=jnp.int32)[None, :]
    slot_of_row = jnp.full((nt * tm,), -1, jnp.int32).at[pos.reshape(-1)].set(
        slot_ids.reshape(-1), unique_indices=True)
    n_used = tile_end[-1]
    steps = jnp.minimum(jnp.arange(nt + 1, dtype=jnp.int32), n_used - 1)
    tile_expert = jnp.sum(tile_end[None, :] <= steps[:, None], axis=1).astype(jnp.int32)
    valid = slot_of_row >= 0
    src_tok = jnp.where(valid, slot_of_row % t, 0).reshape(nt, tm)
    dump = 2 * t + jnp.arange(tm, dtype=jnp.int32)
    dst_row = jnp.where(valid.reshape(nt, tm), slot_of_row.reshape(nt, tm), dump[None, :])
    plan_src = jnp.concatenate([src_tok[1:], jnp.zeros((2, tm), jnp.int32)], axis=0)
    plan_dst = jnp.concatenate([dump[None, :], dst_row], axis=0)
    plan = jnp.stack([plan_src, plan_dst], axis=1) * SLAB
    return src_tok[0:1] * SLAB, plan, tile_expert, n_used.reshape(1).astype(jnp.int32)


def _final_body(x_ref, y0_ref, y1_ref, wt_ref, pmod_ref, g_ref, o_ref):
    x = _combine(x_ref[...], y0_ref, y1_ref, wt_ref, pmod_ref[0, 5:6, :])
    ms = jnp.mean(x * x, axis=-1, keepdims=True)
    o_ref[...] = x * lax.rsqrt(ms + NORM_EPS) * g_ref[...]


def _final(x, y2, wts, pmod, g_final, seq):
    t, d = x.shape
    tm = FINAL_TM
    tiles_per_seq = seq // tm
    row = lambda i: (i, 0)
    return pl.pallas_call(
        _final_body,
        grid=(t // tm,),
        in_specs=[
            pl.BlockSpec((tm, d), row),
            pl.BlockSpec((tm * SLAB, LANES), row),
            pl.BlockSpec((tm * SLAB, LANES), lambda i: (i + t // tm, 0)),
            pl.BlockSpec((tm, LANES), row),
            pl.BlockSpec((1, N_MOD, d), lambda i: (i // tiles_per_seq, 0, 0)),
            pl.BlockSpec((1, d), lambda i: (0, 0)),
        ],
        out_specs=pl.BlockSpec((tm, d), row),
        out_shape=jax.ShapeDtypeStruct((t, d), F32),
        compiler_params=_cparams(("arbitrary",)),
        name="final",
    )(x, y2, y2, wts, pmod, g_final)


def _rotary_tables(positions):
    half = ROT_DIM // 2
    inv_freq = ROPE_THETA ** (-(jnp.arange(0, ROT_DIM, 2, dtype=F32) / ROT_DIM))
    ang = positions.reshape(-1).astype(F32)[:, None] * inv_freq
    cos, sin = jnp.cos(ang), jnp.sin(ang)
    t = ang.shape[0]
    pad = HEAD_DIM - ROT_DIM
    cos_h = jnp.concatenate([cos, cos, jnp.ones((t, pad), F32)], axis=1)
    sa_h = jnp.concatenate([-sin, jnp.zeros((t, HEAD_DIM - half), F32)], axis=1)
    sb_h = jnp.concatenate([jnp.zeros((t, half), F32), sin, jnp.zeros((t, pad), F32)], axis=1)
    rep = LANES // HEAD_DIM
    return jnp.tile(cos_h, (1, rep)), jnp.tile(sa_h, (1, rep)), jnp.tile(sb_h, (1, rep))


def kernel(x, c, positions, w_ada, b_ada, g_norm1, w_in, b_in, sinks, ln_v_g, ln_v_b, w_spatial, b_spatial,
           g_attn_out, g_gmlp_out, w_out, g_norm2, w_router_group, b_router_group, w_router_expert,
           b_router_expert, w_e_gate, w_e_up, w_e_down, g_final):
    batch, seq, d = x.shape
    depth = w_ada.shape[0]
    t = batch * seq
    assert d == D_MODEL and seq % max(PROJ_TM, OUT_TM, FINAL_TM, BLOCK) == 0 and (2 * t) % MOE_TM == 0

    c_pad = jnp.zeros((SUBLANES, d), F32).at[:batch].set(c)
    mod = _adaln(c_pad, w_ada, b_ada).reshape(depth, SUBLANES, N_MOD, d)
    cos_t, sa_t, sb_t = _rotary_tables(positions)

    n_route = N_GROUPS + N_EXPERTS
    w_router = jnp.concatenate(
        [w_router_group, w_router_expert, jnp.zeros((depth, d, LANES - n_route), F32)], axis=-1)
    b_router = jnp.concatenate(
        [b_router_group, b_router_expert, jnp.zeros((depth, LANES - n_route), F32)], axis=-1)

    xt = x.reshape(t, d)
    prev = None
    for l in range(depth):
        xt, qkv, z = _proj(xt, prev, mod[l], g_norm1[l][None], w_in[l].astype(BF16), b_in[l][None],
                           cos_t, sa_t, sb_t, seq)
        merged = _mixer(qkv, z, sinks[l], w_spatial[l], b_spatial[l].T, ln_v_g[l][None], ln_v_b[l][None],
                        g_attn_out[l][None], g_gmlp_out[l][None], batch, seq)
        xt, h2, ids, wts, counts = _outproj(merged, xt, mod[l], g_norm2[l][None], w_out[l].astype(BF16),
                                            w_router[l].astype(BF16), b_router[l][None], seq)
        first_src, plan, tile_expert, n_used = _route_plan(ids, counts, t)
        y2 = _moe(l, h2, first_src, plan, tile_expert, n_used, w_e_gate, w_e_up, w_e_down)
        prev = (y2, wts, mod[l])
    out = _final(xt, prev[0], prev[1], prev[2], g_final[None], seq)
    return out.reshape(batch, seq, d)
```

```python
import functools

import jax
import jax.numpy as jnp
from jax import lax
from jax.experimental import pallas as pl
from jax.experimental.pallas import tpu as pltpu

F32 = jnp.float32
BF16 = jnp.bfloat16

D_MODEL = 2048
ATTN_WIDTH = D_MODEL // 2
HEAD_DIM = 64
N_Q_HEADS = ATTN_WIDTH // HEAD_DIM
N_KV_HEADS = 2
GQA_GROUP = N_Q_HEADS // N_KV_HEADS
WINDOW = 128
BLOCK = 128
ROT_DIM = HEAD_DIM // 4
ROPE_THETA = 500000.0
GMLP_WIDTH = D_MODEL - ATTN_WIDTH
GMLP_GROUPS = 8
GMLP_GROUP_DIM = GMLP_WIDTH // GMLP_GROUPS
Q_COLS = N_Q_HEADS * HEAD_DIM
KV_COLS = N_KV_HEADS * HEAD_DIM
QKV_COLS = Q_COLS + 2 * KV_COLS
IN_COLS = QKV_COLS + 2 * GMLP_WIDTH
N_GROUPS = 4
EXPERTS_PER_GROUP = 8
N_EXPERTS = N_GROUPS * EXPERTS_PER_GROUP
D_EXPERT = D_MODEL // 4
NORM_EPS = 1e-6
N_MOD = 6

LANES = 128
SUBLANES = 8
VMEM_LIMIT_BYTES = 56 * 1024 * 1024

ADA_TN = 1024
PROJ_TM = 256
OUT_TM = 256
MOE_TM = 256
MOE_N_CHUNK = 256
MOE_BUFS = 3
FINAL_TM = 512


def _cparams(sem):
    return pltpu.CompilerParams(dimension_semantics=sem, vmem_limit_bytes=VMEM_LIMIT_BYTES)


def _adaln_body(c_ref, w_ref, b_ref, o_ref):
    c = c_ref[...]
    ca = (c * jax.nn.sigmoid(c)).astype(BF16)
    w = w_ref[0].astype(BF16)
    o_ref[0] = jnp.dot(ca, w, preferred_element_type=F32) + b_ref[0]


def _adaln(c_pad, w_ada, b_ada):
    depth, d, n = w_ada.shape
    rows = c_pad.shape[0]
    return pl.pallas_call(
        _adaln_body,
        grid=(depth, n // ADA_TN),
        in_specs=[
            pl.BlockSpec((rows, d), lambda l, j: (0, 0)),
            pl.BlockSpec((1, d, ADA_TN), lambda l, j: (l, 0, j)),
            pl.BlockSpec((1, 1, ADA_TN), lambda l, j: (l, 0, j)),
        ],
        out_specs=pl.BlockSpec((1, rows, ADA_TN), lambda l, j: (l, 0, j)),
        out_shape=jax.ShapeDtypeStruct((depth, rows, n), F32),
        compiler_params=_cparams(("arbitrary", "arbitrary")),
        name="adaln",
    )(c_pad, w_ada, b_ada.reshape(depth, 1, n))


def _rms_mod(x, g, sc, sh):
    ms = jnp.mean(x * x, axis=-1, keepdims=True)
    return (x * lax.rsqrt(ms + NORM_EPS) * g) * (1.0 + sc) + sh


def _combine(x, y0_ref, y1_ref, wt_ref, gate):
    wt = wt_ref[...]
    y = wt[:, 0:1] * y0_ref[...] + wt[:, 1:2] * y1_ref[...]
    return x + (1.0 + gate) * y


def _gelu_exact(z):
    return 0.5 * z * (1.0 + lax.erf(z * (2.0 ** -0.5)))


def _proj_body(has_prev, *refs):
    if has_prev:
        (x_ref, y0_ref, y1_ref, wt_ref, pmod_ref, mod_ref, g_ref, w_ref, b_ref, cos_ref, sa_ref, sb_ref,
         xo_ref, qkv_ref, z_ref) = refs
        x = _combine(x_ref[...], y0_ref, y1_ref, wt_ref, pmod_ref[0, 5:6, :])
        xo_ref[...] = x
    else:
        x_ref, mod_ref, g_ref, w_ref, b_ref, cos_ref, sa_ref, sb_ref, qkv_ref, z_ref = refs
        x = x_ref[...]
    mod = mod_ref[0]
    h = _rms_mod(x, g_ref[...], mod[1:2, :], mod[0:1, :]).astype(BF16)

    qkv = jnp.dot(h, w_ref[:, :QKV_COLS], preferred_element_type=F32) + b_ref[:, :QKV_COLS]
    cos, sa, sb = cos_ref[...], sa_ref[...], sb_ref[...]
    n_rot = (Q_COLS + KV_COLS) // LANES
    for j in range(QKV_COLS // LANES):
        ch = qkv[:, j * LANES:(j + 1) * LANES]
        if j < n_rot:
            ch = ch * cos + pltpu.roll(ch, LANES - ROT_DIM // 2, 1) * sa + pltpu.roll(ch, ROT_DIM // 2, 1) * sb
        if j < Q_COLS // LANES:
            ch = ch * (HEAD_DIM ** -0.5)
        qkv_ref[:, j * LANES:(j + 1) * LANES] = ch.astype(BF16)

    z = jnp.dot(h, w_ref[:, QKV_COLS:], preferred_element_type=F32) + b_ref[:, QKV_COLS:]
    z_ref[...] = _gelu_exact(z)


def _proj(x, prev, mod_l, g1, w_in, b_in, cos_t, sa_t, sb_t, seq):
    t, d = x.shape
    tm = PROJ_TM
    tiles_per_seq = seq // tm
    row = lambda i: (i, 0)
    const = lambda i: (0, 0)
    modmap = lambda i: (i // tiles_per_seq, 0, 0)
    in_specs = [pl.BlockSpec((tm, d), row)]
    args = [x]
    if prev is not None:
        y2, wts, pmod = prev
        in_specs += [pl.BlockSpec((tm, d), row), pl.BlockSpec((tm, d), lambda i: (i + t // tm, 0)),
                     pl.BlockSpec((tm, LANES), row), pl.BlockSpec((1, N_MOD, d), modmap)]
        args += [y2, y2, wts, pmod]
    in_specs += [
        pl.BlockSpec((1, N_MOD, d), modmap),
        pl.BlockSpec((1, d), const),
        pl.BlockSpec((d, IN_COLS), const, pipeline_mode=pl.Buffered(1)),
        pl.BlockSpec((1, IN_COLS), const),
        pl.BlockSpec((tm, LANES), row),
        pl.BlockSpec((tm, LANES), row),
        pl.BlockSpec((tm, LANES), row),
    ]
    args += [mod_l, g1, w_in, b_in, cos_t, sa_t, sb_t]
    out_specs = [pl.BlockSpec((tm, QKV_COLS), row), pl.BlockSpec((tm, 2 * GMLP_WIDTH), row)]
    out_shape = [jax.ShapeDtypeStruct((t, QKV_COLS), BF16), jax.ShapeDtypeStruct((t, 2 * GMLP_WIDTH), F32)]
    if prev is not None:
        out_specs = [pl.BlockSpec((tm, d), row)] + out_specs
        out_shape = [jax.ShapeDtypeStruct((t, d), F32)] + out_shape
    outs = pl.pallas_call(
        functools.partial(_proj_body, prev is not None),
        grid=(t // tm,),
        in_specs=in_specs,
        out_specs=out_specs,
        out_shape=out_shape,
        compiler_params=_cparams(("arbitrary",)),
        name="proj",
    )(*args)
    if prev is None:
        return (x,) + tuple(outs)
    return tuple(outs)


def _mixer_body(sink_ref, q_ref, kvc_ref, kvp_ref, z_ref, ws_ref, bst_ref, lng_ref, lnb_ref, ga_ref, gg_ref,
                o_ref):
    n = pl.program_id(1)
    q = q_ref[...]
    kvc = kvc_ref[...]
    kvp = kvp_ref[...]
    kcat = jnp.concatenate([kvp[:, :KV_COLS], kvc[:, :KV_COLS]], axis=0)
    vcat = jnp.concatenate([kvp[:, KV_COLS:], kvc[:, KV_COLS:]], axis=0)

    lane = lax.broadcasted_iota(jnp.int32, (BLOCK, LANES), 1)
    lo = lane < HEAD_DIM
    kj = lax.broadcasted_iota(jnp.int32, (2 * BLOCK, BLOCK), 0)
    qi = lax.broadcasted_iota(jnp.int32, (2 * BLOCK, BLOCK), 1)
    rel = qi + BLOCK - kj
    valid = (rel >= 0) & (rel < WINDOW) & ((n > 0) | (kj >= BLOCK))
    row_lo = lax.broadcasted_iota(jnp.int32, (LANES, BLOCK), 0) < HEAD_DIM
    zero_q = jnp.zeros((BLOCK, LANES), BF16)
    nt_dims = (((1,), (1,)), ((), ()))

    attn = []
    chunks_per_kv = GQA_GROUP * HEAD_DIM // LANES
    for hk in range(N_KV_HEADS):
        ksl = kcat[:, hk * HEAD_DIM:(hk + 1) * HEAD_DIM]
        vsl = vcat[:, hk * HEAD_DIM:(hk + 1) * HEAD_DIM]
        kk = jnp.concatenate([ksl, ksl], axis=1)
        vt = jnp.concatenate([vsl, vsl], axis=1).astype(F32).T.astype(BF16)
        for j in range(chunks_per_kv):
            c = hk * chunks_per_kv + j
            qc = q[:, c * LANES:(c + 1) * LANES]
            probs = []
            for half in range(2):
                qm = jnp.where(lo, qc, zero_q) if half == 0 else jnp.where(lo, zero_q, qc)
                s = lax.dot_general(kk, qm, nt_dims, preferred_element_type=F32)
                s = jnp.where(valid, s, -jnp.inf)
                sink = sink_ref[2 * c + half]
                m = jnp.maximum(jnp.max(s, axis=0, keepdims=True), sink)
                p = jnp.exp(s - m)
                denom = jnp.sum(p, axis=0, keepdims=True) + jnp.exp(sink - m)
                probs.append((p * (1.0 / denom)).astype(BF16))
            pn = jnp.concatenate(probs, axis=1)
            ot = jnp.dot(vt, pn, preferred_element_type=F32)
            sel = jnp.where(row_lo, ot[:, :BLOCK], ot[:, BLOCK:])
            attn.append(sel.T)

    ss = attn[0] * attn[0]
    for a in attn[1:]:
        ss = ss + a * a
    scale = lax.rsqrt(jnp.sum(ss, axis=-1, keepdims=True) * (1.0 / ATTN_WIDTH) + NORM_EPS)
    for c, a in enumerate(attn):
        o_ref[:, c * LANES:(c + 1) * LANES] = (a * scale * ga_ref[:, c * LANES:(c + 1) * LANES]).astype(BF16)

    u = z_ref[:, :GMLP_WIDTH]
    v = z_ref[:, GMLP_WIDTH:]
    mu = jnp.mean(v, axis=-1, keepdims=True)
    vc = v - mu
    var = jnp.mean(vc * vc, axis=-1, keepdims=True)
    vn = ((vc * lax.rsqrt(var + NORM_EPS)) * lng_ref[...] + lnb_ref[...]).astype(BF16)
    ti = lax.broadcasted_iota(jnp.int32, (BLOCK, BLOCK), 0)
    si = lax.broadcasted_iota(jnp.int32, (BLOCK, BLOCK), 1)
    tril = si <= ti
    bst = bst_ref[...]
    gm = []
    for h in range(GMLP_GROUPS):
        w = jnp.where(tril, ws_ref[h], 0.0).astype(BF16)
        sp = jnp.dot(w, vn[:, h * LANES:(h + 1) * LANES], preferred_element_type=F32) + bst[:, h:h + 1]
        gm.append(u[:, h * LANES:(h + 1) * LANES] * sp)
    ss = gm[0] * gm[0]
    for a in gm[1:]:
        ss = ss + a * a
    scale = lax.rsqrt(jnp.sum(ss, axis=-1, keepdims=True) * (1.0 / GMLP_WIDTH) + NORM_EPS)
    for c, a in enumerate(gm):
        o_ref[:, ATTN_WIDTH + c * LANES:ATTN_WIDTH + (c + 1) * LANES] = (
            a * scale * gg_ref[:, c * LANES:(c + 1) * LANES]).astype(BF16)


def _mixer(qkv, z, sinks, w_sp, bst, ln_g, ln_b, g_attn, g_gmlp, batch, seq):
    t = qkv.shape[0]
    nb = seq // BLOCK
    kv_blk = Q_COLS // (2 * KV_COLS)
    cur = lambda b, n, s: (b * nb + n, 0)
    const2 = lambda b, n, s: (0, 0)
    grid_spec = pltpu.PrefetchScalarGridSpec(
        num_scalar_prefetch=1,
        grid=(batch, nb),
        in_specs=[
            pl.BlockSpec((BLOCK, Q_COLS), cur),
            pl.BlockSpec((BLOCK, 2 * KV_COLS), lambda b, n, s: (b * nb + n, kv_blk)),
            pl.BlockSpec((BLOCK, 2 * KV_COLS), lambda b, n, s: (b * nb + jnp.maximum(n - 1, 0), kv_blk)),
            pl.BlockSpec((BLOCK, 2 * GMLP_WIDTH), cur),
            pl.BlockSpec((GMLP_GROUPS, BLOCK, BLOCK), lambda b, n, s: (0, 0, 0)),
            pl.BlockSpec((BLOCK, GMLP_GROUPS), const2),
            pl.BlockSpec((1, GMLP_WIDTH), const2),
            pl.BlockSpec((1, GMLP_WIDTH), const2),
            pl.BlockSpec((1, ATTN_WIDTH), const2),
            pl.BlockSpec((1, GMLP_WIDTH), const2),
        ],
        out_specs=pl.BlockSpec((BLOCK, D_MODEL), cur),
    )
    return pl.pallas_call(
        _mixer_body,
        grid_spec=grid_spec,
        out_shape=jax.ShapeDtypeStruct((t, D_MODEL), BF16),
        compiler_params=_cparams(("arbitrary", "arbitrary")),
        name="mixer",
    )(sinks, qkv, qkv, qkv, z, w_sp, bst, ln_g, ln_b, g_attn, g_gmlp)


def _first_index_of_max(vals, lane, width):
    mx = jnp.max(vals, axis=-1, keepdims=True)
    idx = jnp.min(jnp.where(vals == mx, lane, width), axis=-1, keepdims=True)
    return mx, idx


def _outproj_body(m_ref, x_ref, mod_ref, g_ref, w_ref, wr_ref, br_ref,
                  xo_ref, h_ref, ids_ref, wt_ref, cnt_ref, carry_ref):
    i = pl.program_id(0)
    tm = x_ref.shape[0]

    @pl.when(i == 0)
    def _():
        carry_ref[...] = jnp.zeros_like(carry_ref)

    mod = mod_ref[0]
    y = jnp.dot(m_ref[...], w_ref[...], preferred_element_type=F32)
    x = x_ref[...] + (1.0 + mod[2:3, :]) * y
    xo_ref[...] = x
    h = _rms_mod(x, g_ref[...], mod[4:5, :], mod[3:4, :])
    h_ref[...] = h

    logits = jnp.dot(h.astype(BF16), wr_ref[...], preferred_element_type=F32) + br_ref[...]
    lane = lax.broadcasted_iota(jnp.int32, (tm, LANES), 1)
    neg = -jnp.inf
    gl = jnp.where(lane < N_GROUPS, logits, neg)
    ge = jnp.exp(gl - jnp.max(gl, axis=-1, keepdims=True))
    gprob = ge / jnp.sum(ge, axis=-1, keepdims=True)
    g_p, g_idx = _first_index_of_max(gprob, lane, LANES)

    base = N_GROUPS + EXPERTS_PER_GROUP * g_idx
    in_grp = (lane >= base) & (lane < base + EXPERTS_PER_GROUP)
    el = jnp.where(in_grp, logits, neg)
    ee = jnp.exp(el - jnp.max(el, axis=-1, keepdims=True))
    eprob = jnp.where(in_grp, ee / jnp.sum(ee, axis=-1, keepdims=True), -1.0)
    p1, i1 = _first_index_of_max(eprob, lane, LANES)
    eprob2 = jnp.where(lane == i1, -1.0, eprob)
    p2, i2 = _first_index_of_max(eprob2, lane, LANES)
    psum = p1 + p2
    w1 = g_p * (p1 / psum)
    w2 = g_p * (p2 / psum)
    e1 = i1 - N_GROUPS
    e2 = i2 - N_GROUPS

    oh1 = lane == e1
    oh2 = lane == e2
    onehot = jnp.where(oh1 | oh2, 1.0, 0.0)
    ri = lax.broadcasted_iota(jnp.int32, (tm, tm), 0)
    ci = lax.broadcasted_iota(jnp.int32, (tm, tm), 1)
    lower = jnp.where(ci < ri, 1.0, 0.0).astype(BF16)
    before = jnp.dot(lower, onehot.astype(BF16), preferred_element_type=F32) + carry_ref[...]
    r1 = jnp.sum(jnp.where(oh1, before, 0.0), axis=-1, keepdims=True)
    r2 = jnp.sum(jnp.where(oh2, before, 0.0), axis=-1, keepdims=True)
    carry = carry_ref[...] + jnp.sum(onehot, axis=0, keepdims=True)
    carry_ref[...] = carry
    cnt_ref[...] = jnp.broadcast_to(carry, cnt_ref.shape).astype(jnp.int32)

    ids = jnp.where(lane == 0, e1, jnp.where(lane == 1, e2, 0))
    ranks = jnp.where(lane == 2, r1, jnp.where(lane == 3, r2, 0.0)).astype(jnp.int32)
    ids_ref[...] = ids + ranks
    wt_ref[...] = jnp.where(lane == 0, w1, jnp.where(lane == 1, w2, 0.0))


def _outproj(merged, x, mod_l, g2, w_out, w_router, b_router, seq):
    t, d = x.shape
    tm = OUT_TM
    tiles_per_seq = seq // tm
    row = lambda i: (i, 0)
    const = lambda i: (0, 0)
    return pl.pallas_call(
        _outproj_body,
        grid=(t // tm,),
        in_specs=[
            pl.BlockSpec((tm, d), row),
            pl.BlockSpec((tm, d), row),
            pl.BlockSpec((1, N_MOD, d), lambda i: (i // tiles_per_seq, 0, 0)),
            pl.BlockSpec((1, d), const),
            pl.BlockSpec((d, d), const, pipeline_mode=pl.Buffered(1)),
            pl.BlockSpec((d, LANES), const),
            pl.BlockSpec((1, LANES), const),
        ],
        out_specs=[
            pl.BlockSpec((tm, d), row),
            pl.BlockSpec((tm, d), row),
            pl.BlockSpec((tm, LANES), row),
            pl.BlockSpec((tm, LANES), row),
            pl.BlockSpec((SUBLANES, LANES), const),
        ],
        out_shape=[
            jax.ShapeDtypeStruct((t, d), F32),
            jax.ShapeDtypeStruct((t, d), F32),
            jax.ShapeDtypeStruct((t, LANES), jnp.int32),
            jax.ShapeDtypeStruct((t, LANES), F32),
            jax.ShapeDtypeStruct((SUBLANES, LANES), jnp.int32),
        ],
        scratch_shapes=[pltpu.VMEM((1, LANES), F32)],
        compiler_params=_cparams(("arbitrary",)),
        name="outproj",
    )(merged, x, mod_l, g2, w_out, w_router, b_router)


def _moe_body(n_slots, te_ref, nu_ref, h_hbm, first_hbm, plan_hbm, wg_ref, wu_ref, wd_ref, y_hbm,
              first_smem, plan_smem, xbuf, obuf, wgb, wub, wdb, first_sem, plan_sem, g_sem, s_sem):
    i = pl.program_id(0)
    n_used = nu_ref[0]
    nb, tm, d = xbuf.shape
    de = wgb.shape[1]
    b0 = i % nb
    b1 = (i + 1) % nb
    b2 = (i + 2) % nb
    p0 = i % 2
    p1 = 1 - p0

    def plan_copy(k, slot):
        return pltpu.make_async_copy(plan_hbm.at[k], plan_smem.at[slot], plan_sem.at[slot])

    def gather_row(tok, buf, j):
        return pltpu.make_async_copy(h_hbm.at[pl.ds(tok, 1), :], xbuf.at[buf, pl.ds(j, 1), :], g_sem.at[buf])

    def scatter_row(dst, buf, j):
        return pltpu.make_async_copy(obuf.at[buf, pl.ds(j, 1), :], y_hbm.at[pl.ds(dst, 1), :], s_sem.at[buf])

    def wait_gather(buf):
        pltpu.make_async_copy(h_hbm.at[pl.ds(0, tm), :], xbuf.at[buf], g_sem.at[buf]).wait()

    def wait_scatter(buf):
        pltpu.make_async_copy(obuf.at[buf], y_hbm.at[pl.ds(0, tm), :], s_sem.at[buf]).wait()

    @pl.when(i == 0)
    def _():
        first = pltpu.make_async_copy(first_hbm, first_smem, first_sem.at[0])
        first.start()
        plan_copy(0, 0).start()
        first.wait()
        for tile in range(2):
            def body(j, carry, tile=tile):
                gather_row(first_smem[tile, j], tile, j).start()
                return carry
            lax.fori_loop(0, tm, body, 0, unroll=8)
        obuf[nb - 1] = jnp.zeros((tm, d), F32)

    @pl.when(i < n_used)
    def _():
        plan_copy(i, p0).wait()
        plan_copy(i + 1, p1).start()
        wait_gather(b0)

        @pl.when(i >= 2)
        def _():
            wait_scatter(b0)

        new_expert = jnp.logical_or(i == 0, te_ref[i] != te_ref[jnp.maximum(i - 1, 0)])

        @pl.when(new_expert)
        def _():
            wgb[...] = wg_ref[0, 0].astype(BF16)
            wub[...] = wu_ref[0, 0].astype(BF16)
            wdb[...] = wd_ref[0, 0].astype(BF16)

        copies = []
        for j in range(tm):
            copies += [(0, j), (1, j)]
        n_up = de // MOE_N_CHUNK
        n_down = d // MOE_N_CHUNK
        n_pieces = 2 * n_up + n_down
        per_piece = -(-len(copies) // n_pieces)

        def issue(piece):
            for kind, j in copies[piece * per_piece:(piece + 1) * per_piece]:
                if kind == 0:
                    gather_row(plan_smem[p0, 0, j], b2, j).start()
                else:
                    scatter_row(plan_smem[p0, 1, j], b2, j).start()

        xb = xbuf[b0].astype(BF16)
        piece = 0
        acts = []
        for n in range(n_up):
            cols = slice(n * MOE_N_CHUNK, (n + 1) * MOE_N_CHUNK)
            g = jnp.dot(xb, wgb[:, cols], preferred_element_type=F32)
            issue(piece)
            u = jnp.dot(xb, wub[:, cols], preferred_element_type=F32)
            issue(piece + 1)
            piece += 2
            acts.append(((g * jax.nn.sigmoid(g)) * u).astype(BF16))
        a = jnp.concatenate(acts, axis=1)
        for n in range(n_down):
            cols = slice(n * MOE_N_CHUNK, (n + 1) * MOE_N_CHUNK)
            obuf[b0, :, cols] = jnp.dot(a, wdb[:, cols], preferred_element_type=F32)
            issue(piece)
            piece += 1

    @pl.when(i == n_used)
    def _():
        plan_copy(i, p0).wait()
        wait_gather(b0)
        wait_gather(b1)
        wait_scatter(b1)

        @pl.when(i >= 2)
        def _():
            wait_scatter(b0)

        def body(j, carry):
            scatter_row(plan_smem[p0, 1, j], b2, j).start()
            return carry
        lax.fori_loop(0, tm, body, 0, unroll=8)
        wait_scatter(b2)
        obuf[b0] = jnp.zeros((tm, d), F32)
        for k in range(nb):
            fill = pltpu.make_async_copy(obuf.at[b0], y_hbm.at[pl.ds(n_slots + k * tm, tm), :], s_sem.at[b0])
            fill.start()
            fill.wait()


def _moe(layer, h2, first_src, plan, tile_expert, n_used, w_gate, w_up, w_down):
    t, d = h2.shape
    n_slots = 2 * t
    steps, _, tm = plan.shape
    de = w_gate.shape[-1]
    grid_spec = pltpu.PrefetchScalarGridSpec(
        num_scalar_prefetch=2,
        grid=(steps,),
        in_specs=[
            pl.BlockSpec(memory_space=pl.ANY),
            pl.BlockSpec(memory_space=pl.ANY),
            pl.BlockSpec(memory_space=pl.ANY),
            pl.BlockSpec((1, 1, d, de), lambda i, te, nu: (layer, te[i], 0, 0)),
            pl.BlockSpec((1, 1, d, de), lambda i, te, nu: (layer, te[i], 0, 0)),
            pl.BlockSpec((1, 1, de, d), lambda i, te, nu: (layer, te[i], 0, 0)),
        ],
        out_specs=pl.BlockSpec(memory_space=pl.ANY),
        scratch_shapes=[
            pltpu.SMEM((2, tm), jnp.int32),
            pltpu.SMEM((2, 2, tm), jnp.int32),
            pltpu.VMEM((MOE_BUFS, tm, d), F32),
            pltpu.VMEM((MOE_BUFS, tm, d), F32),
            pltpu.VMEM((d, de), BF16),
            pltpu.VMEM((d, de), BF16),
            pltpu.VMEM((de, d), BF16),
            pltpu.SemaphoreType.DMA((1,)),
            pltpu.SemaphoreType.DMA((2,)),
            pltpu.SemaphoreType.DMA((MOE_BUFS,)),
            pltpu.SemaphoreType.DMA((MOE_BUFS,)),
        ],
    )
    return pl.pallas_call(
        functools.partial(_moe_body, n_slots),
        grid_spec=grid_spec,
        out_shape=jax.ShapeDtypeStruct((n_slots + MOE_BUFS * tm, d), F32),
        compiler_params=_cparams(("arbitrary",)),
        name="moe",
    )(tile_expert, n_used, h2, first_src, plan, w_gate, w_up, w_down)


def _route_plan(ids, counts, t):
    tm = MOE_TM
    nt = (2 * t) // tm + N_EXPERTS
    e = ids[:, 0:2]
    rank = ids[:, 2:4]
    cnt = counts[0, :N_EXPERTS]
    tiles = (cnt + tm - 1) // tm
    tile_end = jnp.cumsum(tiles)
    off = (tile_end - tiles) * tm
    pos = off[e] + rank
    slot_ids = jnp.arange(t, dtype=jnp.int32)[:, None] + t * jnp.arange(2, dtype=jnp.int32)[None, :]
    slot_of_row = jnp.full((nt * tm,), -1, jnp.int32).at[pos.reshape(-1)].set(
        slot_ids.reshape(-1), unique_indices=True)
    n_used = tile_end[-1]
    steps = jnp.minimum(jnp.arange(nt + 1, dtype=jnp.int32), n_used - 1)
    tile_expert = jnp.sum(tile_end[None, :] <= steps[:, None], axis=1).astype(jnp.int32)
    valid = slot_of_row >= 0
    src_tok = jnp.where(valid, slot_of_row % t, 0).reshape(nt, tm)
    tile_buf = jnp.arange(-1, nt, dtype=jnp.int32) % MOE_BUFS
    dump = 2 * t + tile_buf[:, None] * tm + jnp.arange(tm, dtype=jnp.int32)[None, :]
    dst_row = jnp.where(valid.reshape(nt, tm), slot_of_row.reshape(nt, tm), dump[1:])
    plan_src = jnp.concatenate([src_tok[2:], jnp.zeros((3, tm), jnp.int32)], axis=0)
    plan_dst = jnp.concatenate([dump[0:1], dst_row], axis=0)
    plan = jnp.stack([plan_src, plan_dst], axis=1)
    return src_tok[0:2], plan, tile_expert, n_used.reshape(1).astype(jnp.int32)


def _final_body(x_ref, y0_ref, y1_ref, wt_ref, pmod_ref, g_ref, o_ref):
    x = _combine(x_ref[...], y0_ref, y1_ref, wt_ref, pmod_ref[0, 5:6, :])
    ms = jnp.mean(x * x, axis=-1, keepdims=True)
    o_ref[...] = x * lax.rsqrt(ms + NORM_EPS) * g_ref[...]


def _final(x, y2, wts, pmod, g_final, seq):
    t, d = x.shape
    tm = FINAL_TM
    tiles_per_seq = seq // tm
    row = lambda i: (i, 0)
    return pl.pallas_call(
        _final_body,
        grid=(t // tm,),
        in_specs=[
            pl.BlockSpec((tm, d), row),
            pl.BlockSpec((tm, d), row),
            pl.BlockSpec((tm, d), lambda i: (i + t // tm, 0)),
            pl.BlockSpec((tm, LANES), row),
            pl.BlockSpec((1, N_MOD, d), lambda i: (i // tiles_per_seq, 0, 0)),
            pl.BlockSpec((1, d), lambda i: (0, 0)),
        ],
        out_specs=pl.BlockSpec((tm, d), row),
        out_shape=jax.ShapeDtypeStruct((t, d), F32),
        compiler_params=_cparams(("arbitrary",)),
        name="final",
    )(x, y2, y2, wts, pmod, g_final)


def _rotary_tables(positions):
    half = ROT_DIM // 2
    inv_freq = ROPE_THETA ** (-(jnp.arange(0, ROT_DIM, 2, dtype=F32) / ROT_DIM))
    ang = positions.reshape(-1).astype(F32)[:, None] * inv_freq
    cos, sin = jnp.cos(ang), jnp.sin(ang)
    t = ang.shape[0]
    pad = HEAD_DIM - ROT_DIM
    cos_h = jnp.concatenate([cos, cos, jnp.ones((t, pad), F32)], axis=1)
    sa_h = jnp.concatenate([-sin, jnp.zeros((t, HEAD_DIM - half), F32)], axis=1)
    sb_h = jnp.concatenate([jnp.zeros((t, half), F32), sin, jnp.zeros((t, pad), F32)], axis=1)
    rep = LANES // HEAD_DIM
    return jnp.tile(cos_h, (1, rep)), jnp.tile(sa_h, (1, rep)), jnp.tile(sb_h, (1, rep))


def kernel(x, c, positions, w_ada, b_ada, g_norm1, w_in, b_in, sinks, ln_v_g, ln_v_b, w_spatial, b_spatial,
           g_attn_out, g_gmlp_out, w_out, g_norm2, w_router_group, b_router_group, w_router_expert,
           b_router_expert, w_e_gate, w_e_up, w_e_down, g_final):
    batch, seq, d = x.shape
    depth = w_ada.shape[0]
    t = batch * seq
    assert d == D_MODEL and seq % max(PROJ_TM, OUT_TM, FINAL_TM, BLOCK) == 0 and (2 * t) % MOE_TM == 0

    c_pad = jnp.zeros((SUBLANES, d), F32).at[:batch].set(c)
    mod = _adaln(c_pad, w_ada, b_ada).reshape(depth, SUBLANES, N_MOD, d)
    cos_t, sa_t, sb_t = _rotary_tables(positions)

    n_route = N_GROUPS + N_EXPERTS
    w_router = jnp.concatenate(
        [w_router_group, w_router_expert, jnp.zeros((depth, d, LANES - n_route), F32)], axis=-1)
    b_router = jnp.concatenate(
        [b_router_group, b_router_expert, jnp.zeros((depth, LANES - n_route), F32)], axis=-1)

    xt = x.reshape(t, d)
    prev = None
    for l in range(depth):
        xt, qkv, z = _proj(xt, prev, mod[l], g_norm1[l][None], w_in[l].astype(BF16), b_in[l][None],
                           cos_t, sa_t, sb_t, seq)
        merged = _mixer(qkv, z, sinks[l], w_spatial[l], b_spatial[l].T, ln_v_g[l][None], ln_v_b[l][None],
                        g_attn_out[l][None], g_gmlp_out[l][None], batch, seq)
        xt, h2, ids, wts, counts = _outproj(merged, xt, mod[l], g_norm2[l][None], w_out[l].astype(BF16),
                                            w_router[l].astype(BF16), b_router[l][None], seq)
        first_src, plan, tile_expert, n_used = _route_plan(ids, counts, t)
        y2 = _moe(l, h2, first_src, plan, tile_expert, n_used, w_e_gate, w_e_up, w_e_down)
        prev = (y2, wts, mod[l])
    out = _final(xt, prev[0], prev[1], prev[2], g_final[None], seq)
    return out.reshape(batch, seq, d)
```

```python
import functools

import jax
import jax.numpy as jnp
from jax import lax
from jax.experimental import pallas as pl
from jax.experimental.pallas import tpu as pltpu

F32 = jnp.float32
BF16 = jnp.bfloat16

D_MODEL = 2048
ATTN_WIDTH = D_MODEL // 2
HEAD_DIM = 64
N_Q_HEADS = ATTN_WIDTH // HEAD_DIM
N_KV_HEADS = 2
GQA_GROUP = N_Q_HEADS // N_KV_HEADS
WINDOW = 128
BLOCK = 128
ROT_DIM = HEAD_DIM // 4
ROPE_THETA = 500000.0
GMLP_WIDTH = D_MODEL - ATTN_WIDTH
GMLP_GROUPS = 8
GMLP_GROUP_DIM = GMLP_WIDTH // GMLP_GROUPS
Q_COLS = N_Q_HEADS * HEAD_DIM
KV_COLS = N_KV_HEADS * HEAD_DIM
QKV_COLS = Q_COLS + 2 * KV_COLS
IN_COLS = QKV_COLS + 2 * GMLP_WIDTH
N_GROUPS = 4
EXPERTS_PER_GROUP = 8
N_EXPERTS = N_GROUPS * EXPERTS_PER_GROUP
D_EXPERT = D_MODEL // 4
NORM_EPS = 1e-6
N_MOD = 6

LANES = 128
SUBLANES = 8
VMEM_LIMIT_BYTES = 56 * 1024 * 1024

ADA_TN = 1024
PROJ_TM = 256
OUT_TM = 512
MIX_BLOCKS = 2
MOE_TM = 256
MOE_N_CHUNK = 256
MOE_BUFS = 3
FINAL_TM = 512


def _cparams(sem):
    return pltpu.CompilerParams(dimension_semantics=sem, vmem_limit_bytes=VMEM_LIMIT_BYTES)


def _adaln_body(c_ref, w_ref, b_ref, o_ref):
    c = c_ref[...]
    ca = (c * jax.nn.sigmoid(c)).astype(BF16)
    w = w_ref[0].astype(BF16)
    o_ref[0] = jnp.dot(ca, w, preferred_element_type=F32) + b_ref[0]


def _adaln(c_pad, w_ada, b_ada):
    depth, d, n = w_ada.shape
    rows = c_pad.shape[0]
    return pl.pallas_call(
        _adaln_body,
        grid=(depth, n // ADA_TN),
        in_specs=[
            pl.BlockSpec((rows, d), lambda l, j: (0, 0)),
            pl.BlockSpec((1, d, ADA_TN), lambda l, j: (l, 0, j)),
            pl.BlockSpec((1, 1, ADA_TN), lambda l, j: (l, 0, j)),
        ],
        out_specs=pl.BlockSpec((1, rows, ADA_TN), lambda l, j: (l, 0, j)),
        out_shape=jax.ShapeDtypeStruct((depth, rows, n), F32),
        compiler_params=_cparams(("arbitrary", "arbitrary")),
        name="adaln",
    )(c_pad, w_ada, b_ada.reshape(depth, 1, n))


def _rms_mod(x, g, sc, sh):
    ms = jnp.mean(x * x, axis=-1, keepdims=True)
    return (x * lax.rsqrt(ms + NORM_EPS) * g) * (1.0 + sc) + sh


def _combine(x, y0_ref, y1_ref, wt_ref, gate):
    wt = wt_ref[...]
    y = wt[:, 0:1] * y0_ref[...] + wt[:, 1:2] * y1_ref[...]
    return x + (1.0 + gate) * y


def _gelu_exact(z):
    return 0.5 * z * (1.0 + lax.erf(z * (2.0 ** -0.5)))


def _proj_body(has_prev, *refs):
    if has_prev:
        (x_ref, y0_ref, y1_ref, wt_ref, pmod_ref, mod_ref, g_ref, w_ref, b_ref, cos_ref, sa_ref, sb_ref,
         xo_ref, qkv_ref, z_ref) = refs
        x = _combine(x_ref[...], y0_ref, y1_ref, wt_ref, pmod_ref[0, 5:6, :])
        xo_ref[...] = x
    else:
        x_ref, mod_ref, g_ref, w_ref, b_ref, cos_ref, sa_ref, sb_ref, qkv_ref, z_ref = refs
        x = x_ref[...]
    mod = mod_ref[0]
    h = _rms_mod(x, g_ref[...], mod[1:2, :], mod[0:1, :]).astype(BF16)

    qkv = jnp.dot(h, w_ref[:, :QKV_COLS], preferred_element_type=F32) + b_ref[:, :QKV_COLS]
    cos, sa, sb = cos_ref[...], sa_ref[...], sb_ref[...]
    n_rot = (Q_COLS + KV_COLS) // LANES
    for j in range(QKV_COLS // LANES):
        ch = qkv[:, j * LANES:(j + 1) * LANES]
        if j < n_rot:
            ch = ch * cos + pltpu.roll(ch, LANES - ROT_DIM // 2, 1) * sa + pltpu.roll(ch, ROT_DIM // 2, 1) * sb
        if j < Q_COLS // LANES:
            ch = ch * (HEAD_DIM ** -0.5)
        qkv_ref[:, j * LANES:(j + 1) * LANES] = ch.astype(BF16)

    z = jnp.dot(h, w_ref[:, QKV_COLS:], preferred_element_type=F32) + b_ref[:, QKV_COLS:]
    z_ref[...] = _gelu_exact(z)


def _proj(x, prev, mod_l, g1, w_in, b_in, cos_t, sa_t, sb_t, seq):
    t, d = x.shape
    tm = PROJ_TM
    tiles_per_seq = seq // tm
    row = lambda i: (i, 0)
    const = lambda i: (0, 0)
    modmap = lambda i: (i // tiles_per_seq, 0, 0)
    in_specs = [pl.BlockSpec((tm, d), row)]
    args = [x]
    if prev is not None:
        y2, wts, pmod = prev
        in_specs += [pl.BlockSpec((tm, d), row), pl.BlockSpec((tm, d), lambda i: (i + t // tm, 0)),
                     pl.BlockSpec((tm, LANES), row), pl.BlockSpec((1, N_MOD, d), modmap)]
        args += [y2, y2, wts, pmod]
    in_specs += [
        pl.BlockSpec((1, N_MOD, d), modmap),
        pl.BlockSpec((1, d), const),
        pl.BlockSpec((d, IN_COLS), const, pipeline_mode=pl.Buffered(1)),
        pl.BlockSpec((1, IN_COLS), const),
        pl.BlockSpec((tm, LANES), row),
        pl.BlockSpec((tm, LANES), row),
        pl.BlockSpec((tm, LANES), row),
    ]
    args += [mod_l, g1, w_in, b_in, cos_t, sa_t, sb_t]
    out_specs = [pl.BlockSpec((tm, QKV_COLS), row), pl.BlockSpec((tm, 2 * GMLP_WIDTH), row)]
    out_shape = [jax.ShapeDtypeStruct((t, QKV_COLS), BF16), jax.ShapeDtypeStruct((t, 2 * GMLP_WIDTH), F32)]
    if prev is not None:
        out_specs = [pl.BlockSpec((tm, d), row)] + out_specs
        out_shape = [jax.ShapeDtypeStruct((t, d), F32)] + out_shape
    outs = pl.pallas_call(
        functools.partial(_proj_body, prev is not None),
        grid=(t // tm,),
        in_specs=in_specs,
        out_specs=out_specs,
        out_shape=out_shape,
        compiler_params=_cparams(("arbitrary",)),
        name="proj",
    )(*args)
    if prev is None:
        return (x,) + tuple(outs)
    return tuple(outs)


def _mixer_body(sink_ref, q_ref, kvc_ref, kvp_ref, z_ref, ws_ref, bst_ref, lng_ref, lnb_ref, ga_ref, gg_ref,
                o_ref):
    n = pl.program_id(1)
    for sub in range(MIX_BLOCKS):
        r0 = sub * BLOCK
        kvp = kvp_ref[...] if sub == 0 else kvc_ref[r0 - BLOCK:r0, :]
        has_prev = (n > 0) if sub == 0 else None
        _mixer_block(sink_ref, q_ref[r0:r0 + BLOCK, :], kvc_ref[r0:r0 + BLOCK, :], kvp, has_prev, z_ref, r0,
                     ws_ref, bst_ref, lng_ref, lnb_ref, ga_ref, gg_ref, o_ref)


def _mixer_block(sink_ref, q, kvc, kvp, has_prev, z_ref, r0, ws_ref, bst_ref, lng_ref, lnb_ref, ga_ref, gg_ref,
                 o_ref):
    rows = slice(r0, r0 + BLOCK)
    kcat = jnp.concatenate([kvp[:, :KV_COLS], kvc[:, :KV_COLS]], axis=0)
    vcat = jnp.concatenate([kvp[:, KV_COLS:], kvc[:, KV_COLS:]], axis=0)

    lane = lax.broadcasted_iota(jnp.int32, (BLOCK, LANES), 1)
    lo = lane < HEAD_DIM
    kj = lax.broadcasted_iota(jnp.int32, (2 * BLOCK, BLOCK), 0)
    qi = lax.broadcasted_iota(jnp.int32, (2 * BLOCK, BLOCK), 1)
    rel = qi + BLOCK - kj
    valid = (rel >= 0) & (rel < WINDOW)
    if has_prev is not None:
        valid = valid & (has_prev | (kj >= BLOCK))
    row_lo = lax.broadcasted_iota(jnp.int32, (LANES, BLOCK), 0) < HEAD_DIM
    zero_q = jnp.zeros((BLOCK, LANES), BF16)
    nt_dims = (((1,), (1,)), ((), ()))

    attn = []
    chunks_per_kv = GQA_GROUP * HEAD_DIM // LANES
    for hk in range(N_KV_HEADS):
        ksl = kcat[:, hk * HEAD_DIM:(hk + 1) * HEAD_DIM]
        vsl = vcat[:, hk * HEAD_DIM:(hk + 1) * HEAD_DIM]
        kk = jnp.concatenate([ksl, ksl], axis=1)
        vt = jnp.concatenate([vsl, vsl], axis=1).astype(F32).T.astype(BF16)
        for j in range(chunks_per_kv):
            c = hk * chunks_per_kv + j
            qc = q[:, c * LANES:(c + 1) * LANES]
            probs = []
            for half in range(2):
                qm = jnp.where(lo, qc, zero_q) if half == 0 else jnp.where(lo, zero_q, qc)
                s = lax.dot_general(kk, qm, nt_dims, preferred_element_type=F32)
                s = jnp.where(valid, s, -jnp.inf)
                sink = sink_ref[2 * c + half]
                m = jnp.maximum(jnp.max(s, axis=0, keepdims=True), sink)
                p = jnp.exp(s - m)
                denom = jnp.sum(p, axis=0, keepdims=True) + jnp.exp(sink - m)
                probs.append((p * (1.0 / denom)).astype(BF16))
            pn = jnp.concatenate(probs, axis=1)
            ot = jnp.dot(vt, pn, preferred_element_type=F32)
            sel = jnp.where(row_lo, ot[:, :BLOCK], ot[:, BLOCK:])
            attn.append(sel.T)

    ss = attn[0] * attn[0]
    for a in attn[1:]:
        ss = ss + a * a
    scale = lax.rsqrt(jnp.sum(ss, axis=-1, keepdims=True) * (1.0 / ATTN_WIDTH) + NORM_EPS)
    for c, a in enumerate(attn):
        o_ref[rows, c * LANES:(c + 1) * LANES] = (a * scale * ga_ref[:, c * LANES:(c + 1) * LANES]).astype(BF16)

    u = z_ref[rows, :GMLP_WIDTH]
    v = z_ref[rows, GMLP_WIDTH:]
    mu = jnp.mean(v, axis=-1, keepdims=True)
    vc = v - mu
    var = jnp.mean(vc * vc, axis=-1, keepdims=True)
    vn = ((vc * lax.rsqrt(var + NORM_EPS)) * lng_ref[...] + lnb_ref[...]).astype(BF16)
    ti = lax.broadcasted_iota(jnp.int32, (BLOCK, BLOCK), 0)
    si = lax.broadcasted_iota(jnp.int32, (BLOCK, BLOCK), 1)
    tril = si <= ti
    bst = bst_ref[...]
    gm = []
    for h in range(GMLP_GROUPS):
        w = jnp.where(tril, ws_ref[h], 0.0).astype(BF16)
        sp = jnp.dot(w, vn[:, h * LANES:(h + 1) * LANES], preferred_element_type=F32) + bst[:, h:h + 1]
        gm.append(u[:, h * LANES:(h + 1) * LANES] * sp)
    ss = gm[0] * gm[0]
    for a in gm[1:]:
        ss = ss + a * a
    scale = lax.rsqrt(jnp.sum(ss, axis=-1, keepdims=True) * (1.0 / GMLP_WIDTH) + NORM_EPS)
    for c, a in enumerate(gm):
        o_ref[rows, ATTN_WIDTH + c * LANES:ATTN_WIDTH + (c + 1) * LANES] = (
            a * scale * gg_ref[:, c * LANES:(c + 1) * LANES]).astype(BF16)


def _mixer(qkv, z, sinks, w_sp, bst, ln_g, ln_b, g_attn, g_gmlp, batch, seq):
    t = qkv.shape[0]
    nb = seq // BLOCK
    steps = nb // MIX_BLOCKS
    rows = MIX_BLOCKS * BLOCK
    kv_blk = Q_COLS // (2 * KV_COLS)
    cur = lambda b, n, s: (b * steps + n, 0)
    const2 = lambda b, n, s: (0, 0)
    grid_spec = pltpu.PrefetchScalarGridSpec(
        num_scalar_prefetch=1,
        grid=(batch, steps),
        in_specs=[
            pl.BlockSpec((rows, Q_COLS), cur),
            pl.BlockSpec((rows, 2 * KV_COLS), lambda b, n, s: (b * steps + n, kv_blk)),
            pl.BlockSpec((BLOCK, 2 * KV_COLS),
                         lambda b, n, s: (b * nb + jnp.maximum(MIX_BLOCKS * n - 1, 0), kv_blk)),
            pl.BlockSpec((rows, 2 * GMLP_WIDTH), cur),
            pl.BlockSpec((GMLP_GROUPS, BLOCK, BLOCK), lambda b, n, s: (0, 0, 0)),
            pl.BlockSpec((BLOCK, GMLP_GROUPS), const2),
            pl.BlockSpec((1, GMLP_WIDTH), const2),
            pl.BlockSpec((1, GMLP_WIDTH), const2),
            pl.BlockSpec((1, ATTN_WIDTH), const2),
            pl.BlockSpec((1, GMLP_WIDTH), const2),
        ],
        out_specs=pl.BlockSpec((rows, D_MODEL), cur),
    )
    return pl.pallas_call(
        _mixer_body,
        grid_spec=grid_spec,
        out_shape=jax.ShapeDtypeStruct((t, D_MODEL), BF16),
        compiler_params=_cparams(("arbitrary", "arbitrary")),
        name="mixer",
    )(sinks, qkv, qkv, qkv, z, w_sp, bst, ln_g, ln_b, g_attn, g_gmlp)


def _first_index_of_max(vals, lane, width):
    mx = jnp.max(vals, axis=-1, keepdims=True)
    idx = jnp.min(jnp.where(vals == mx, lane, width), axis=-1, keepdims=True)
    return mx, idx


def _outproj_body(m_ref, x_ref, mod_ref, g_ref, w_ref, wr_ref, br_ref,
                  xo_ref, h_ref, ids_ref, wt_ref, cnt_ref, carry_ref):
    i = pl.program_id(0)
    tm = x_ref.shape[0]

    @pl.when(i == 0)
    def _():
        carry_ref[...] = jnp.zeros_like(carry_ref)

    mod = mod_ref[0]
    y = jnp.dot(m_ref[...], w_ref[...], preferred_element_type=F32)
    x = x_ref[...] + (1.0 + mod[2:3, :]) * y
    xo_ref[...] = x
    h = _rms_mod(x, g_ref[...], mod[4:5, :], mod[3:4, :])
    h_ref[...] = h

    logits = jnp.dot(h.astype(BF16), wr_ref[...], preferred_element_type=F32) + br_ref[...]
    lane = lax.broadcasted_iota(jnp.int32, (tm, LANES), 1)
    neg = -jnp.inf
    gl = jnp.where(lane < N_GROUPS, logits, neg)
    ge = jnp.exp(gl - jnp.max(gl, axis=-1, keepdims=True))
    gprob = ge / jnp.sum(ge, axis=-1, keepdims=True)
    g_p, g_idx = _first_index_of_max(gprob, lane, LANES)

    base = N_GROUPS + EXPERTS_PER_GROUP * g_idx
    in_grp = (lane >= base) & (lane < base + EXPERTS_PER_GROUP)
    el = jnp.where(in_grp, logits, neg)
    ee = jnp.exp(el - jnp.max(el, axis=-1, keepdims=True))
    eprob = jnp.where(in_grp, ee / jnp.sum(ee, axis=-1, keepdims=True), -1.0)
    p1, i1 = _first_index_of_max(eprob, lane, LANES)
    eprob2 = jnp.where(lane == i1, -1.0, eprob)
    p2, i2 = _first_index_of_max(eprob2, lane, LANES)
    psum = p1 + p2
    w1 = g_p * (p1 / psum)
    w2 = g_p * (p2 / psum)
    e1 = i1 - N_GROUPS
    e2 = i2 - N_GROUPS

    oh1 = lane == e1
    oh2 = lane == e2
    onehot = jnp.where(oh1 | oh2, 1.0, 0.0)
    ri = lax.broadcasted_iota(jnp.int32, (tm, tm), 0)
    ci = lax.broadcasted_iota(jnp.int32, (tm, tm), 1)
    lower = jnp.where(ci < ri, 1.0, 0.0).astype(BF16)
    before = jnp.dot(lower, onehot.astype(BF16), preferred_element_type=F32) + carry_ref[...]
    r1 = jnp.sum(jnp.where(oh1, before, 0.0), axis=-1, keepdims=True)
    r2 = jnp.sum(jnp.where(oh2, before, 0.0), axis=-1, keepdims=True)
    carry = carry_ref[...] + jnp.sum(onehot, axis=0, keepdims=True)
    carry_ref[...] = carry
    cnt_ref[...] = jnp.broadcast_to(carry, cnt_ref.shape).astype(jnp.int32)

    ids = jnp.where(lane == 0, e1, jnp.where(lane == 1, e2, 0))
    ranks = jnp.where(lane == 2, r1, jnp.where(lane == 3, r2, 0.0)).astype(jnp.int32)
    ids_ref[...] = ids + ranks
    wt_ref[...] = jnp.where(lane == 0, w1, jnp.where(lane == 1, w2, 0.0))


def _outproj(merged, x, mod_l, g2, w_out, w_router, b_router, seq):
    t, d = x.shape
    tm = OUT_TM
    tiles_per_seq = seq // tm
    row = lambda i: (i, 0)
    const = lambda i: (0, 0)
    return pl.pallas_call(
        _outproj_body,
        grid=(t // tm,),
        in_specs=[
            pl.BlockSpec((tm, d), row),
            pl.BlockSpec((tm, d), row),
            pl.BlockSpec((1, N_MOD, d), lambda i: (i // tiles_per_seq, 0, 0)),
            pl.BlockSpec((1, d), const),
            pl.BlockSpec((d, d), const, pipeline_mode=pl.Buffered(1)),
            pl.BlockSpec((d, LANES), const),
            pl.BlockSpec((1, LANES), const),
        ],
        out_specs=[
            pl.BlockSpec((tm, d), row),
            pl.BlockSpec((tm, d), row),
            pl.BlockSpec((tm, LANES), row),
            pl.BlockSpec((tm, LANES), row),
            pl.BlockSpec((SUBLANES, LANES), const),
        ],
        out_shape=[
            jax.ShapeDtypeStruct((t, d), F32),
            jax.ShapeDtypeStruct((t, d), F32),
            jax.ShapeDtypeStruct((t, LANES), jnp.int32),
            jax.ShapeDtypeStruct((t, LANES), F32),
            jax.ShapeDtypeStruct((SUBLANES, LANES), jnp.int32),
        ],
        scratch_shapes=[pltpu.VMEM((1, LANES), F32)],
        compiler_params=_cparams(("arbitrary",)),
        name="outproj",
    )(merged, x, mod_l, g2, w_out, w_router, b_router)


def _moe_body(n_slots, te_ref, nu_ref, h_hbm, first_hbm, plan_hbm, wg_ref, wu_ref, wd_ref, y_hbm,
              first_smem, plan_smem, xbuf, obuf, wgb, wub, wdb, first_sem, plan_sem, g_sem, s_sem):
    i = pl.program_id(0)
    n_used = nu_ref[0]
    nb, tm, d = xbuf.shape
    de = wgb.shape[1]
    b0 = i % nb
    b1 = (i + 1) % nb
    b2 = (i + 2) % nb
    p0 = i % 2
    p1 = 1 - p0

    def plan_copy(k, slot):
        return pltpu.make_async_copy(plan_hbm.at[k], plan_smem.at[slot], plan_sem.at[slot])

    def gather_row(tok, buf, j):
        return pltpu.make_async_copy(h_hbm.at[pl.ds(tok, 1), :], xbuf.at[buf, pl.ds(j, 1), :], g_sem.at[buf])

    def scatter_row(dst, buf, j):
        return pltpu.make_async_copy(obuf.at[buf, pl.ds(j, 1), :], y_hbm.at[pl.ds(dst, 1), :], s_sem.at[buf])

    def wait_gather(buf):
        pltpu.make_async_copy(h_hbm.at[pl.ds(0, tm), :], xbuf.at[buf], g_sem.at[buf]).wait()

    def wait_scatter(buf):
        pltpu.make_async_copy(obuf.at[buf], y_hbm.at[pl.ds(0, tm), :], s_sem.at[buf]).wait()

    @pl.when(i == 0)
    def _():
        first = pltpu.make_async_copy(first_hbm, first_smem, first_sem.at[0])
        first.start()
        plan_copy(0, 0).start()
        first.wait()
        for tile in range(2):
            def body(j, carry, tile=tile):
                gather_row(first_smem[tile, j], tile, j).start()
                return carry
            lax.fori_loop(0, tm, body, 0, unroll=8)
        obuf[nb - 1] = jnp.zeros((tm, d), F32)

    @pl.when(i < n_used)
    def _():
        plan_copy(i, p0).wait()
        plan_copy(i + 1, p1).start()
        wait_gather(b0)

        @pl.when(i >= 2)
        def _():
            wait_scatter(b0)

        new_expert = jnp.logical_or(i == 0, te_ref[i] != te_ref[jnp.maximum(i - 1, 0)])

        @pl.when(new_expert)
        def _():
            wgb[...] = wg_ref[0, 0].astype(BF16)
            wub[...] = wu_ref[0, 0].astype(BF16)
            wdb[...] = wd_ref[0, 0].astype(BF16)

        copies = []
        for j in range(tm):
            copies += [(0, j), (1, j)]
        n_up = de // MOE_N_CHUNK
        n_down = d // MOE_N_CHUNK
        n_pieces = 2 * n_up + n_down
        per_piece = -(-len(copies) // n_pieces)

        def issue(piece):
            for kind, j in copies[piece * per_piece:(piece + 1) * per_piece]:
                if kind == 0:
                    gather_row(plan_smem[p0, 0, j], b2, j).start(priority=j % 2)
                else:
                    scatter_row(plan_smem[p0, 1, j], b2, j).start(priority=(j + 1) % 2)

        xb = xbuf[b0].astype(BF16)
        piece = 0
        acts = []
        for n in range(n_up):
            cols = slice(n * MOE_N_CHUNK, (n + 1) * MOE_N_CHUNK)
            g = jnp.dot(xb, wgb[:, cols], preferred_element_type=F32)
            issue(piece)
            u = jnp.dot(xb, wub[:, cols], preferred_element_type=F32)
            issue(piece + 1)
            piece += 2
            acts.append(((g * jax.nn.sigmoid(g)) * u).astype(BF16))
        a = jnp.concatenate(acts, axis=1)
        for n in range(n_down):
            cols = slice(n * MOE_N_CHUNK, (n + 1) * MOE_N_CHUNK)
            obuf[b0, :, cols] = jnp.dot(a, wdb[:, cols], preferred_element_type=F32)
            issue(piece)
            piece += 1

    @pl.when(i == n_used)
    def _():
        plan_copy(i, p0).wait()
        wait_gather(b0)
        wait_gather(b1)
        wait_scatter(b1)

        @pl.when(i >= 2)
        def _():
            wait_scatter(b0)

        def body(j, carry):
            scatter_row(plan_smem[p0, 1, j], b2, j).start()
            return carry
        lax.fori_loop(0, tm, body, 0, unroll=8)
        wait_scatter(b2)
        obuf[b0] = jnp.zeros((tm, d), F32)
        for k in range(nb):
            fill = pltpu.make_async_copy(obuf.at[b0], y_hbm.at[pl.ds(n_slots + k * tm, tm), :], s_sem.at[b0])
            fill.start()
            fill.wait()


def _moe(layer, h2, first_src, plan, tile_expert, n_used, w_gate, w_up, w_down):
    t, d = h2.shape
    n_slots = 2 * t
    steps, _, tm = plan.shape
    de = w_gate.shape[-1]
    grid_spec = pltpu.PrefetchScalarGridSpec(
        num_scalar_prefetch=2,
        grid=(steps,),
        in_specs=[
            pl.BlockSpec(memory_space=pl.ANY),
            pl.BlockSpec(memory_space=pl.ANY),
            pl.BlockSpec(memory_space=pl.ANY),
            pl.BlockSpec((1, 1, d, de), lambda i, te, nu: (layer, te[i], 0, 0)),
            pl.BlockSpec((1, 1, d, de), lambda i, te, nu: (layer, te[i], 0, 0)),
            pl.BlockSpec((1, 1, de, d), lambda i, te, nu: (layer, te[i], 0, 0)),
        ],
        out_specs=pl.BlockSpec(memory_space=pl.ANY),
        scratch_shapes=[
            pltpu.SMEM((2, tm), jnp.int32),
            pltpu.SMEM((2, 2, tm), jnp.int32),
            pltpu.VMEM((MOE_BUFS, tm, d), F32),
            pltpu.VMEM((MOE_BUFS, tm, d), F32),
            pltpu.VMEM((d, de), BF16),
            pltpu.VMEM((d, de), BF16),
            pltpu.VMEM((de, d), BF16),
            pltpu.SemaphoreType.DMA((1,)),
            pltpu.SemaphoreType.DMA((2,)),
            pltpu.SemaphoreType.DMA((MOE_BUFS,)),
            pltpu.SemaphoreType.DMA((MOE_BUFS,)),
        ],
    )
    return pl.pallas_call(
        functools.partial(_moe_body, n_slots),
        grid_spec=grid_spec,
        out_shape=jax.ShapeDtypeStruct((n_slots + MOE_BUFS * tm, d), F32),
        compiler_params=_cparams(("arbitrary",)),
        name="moe",
    )(tile_expert, n_used, h2, first_src, plan, w_gate, w_up, w_down)


def _route_plan(ids, counts, t):
    tm = MOE_TM
    nt = (2 * t) // tm + N_EXPERTS
    e = ids[:, 0:2]
    rank = ids[:, 2:4]
    cnt = counts[0, :N_EXPERTS]
    tiles = (cnt + tm - 1) // tm
    tile_end = jnp.cumsum(tiles)
    off = (tile_end - tiles) * tm
    pos = off[e] + rank
    slot_ids = jnp.arange(t, dtype=jnp.int32)[:, None] + t * jnp.arange(2, dtype=jnp.int32)[None, :]
    slot_of_row = jnp.full((nt * tm,), -1, jnp.int32).at[pos.reshape(-1)].set(
        slot_ids.reshape(-1), unique_indices=True)
    n_used = tile_end[-1]
    steps = jnp.minimum(jnp.arange(nt + 1, dtype=jnp.int32), n_used - 1)
    tile_expert = jnp.sum(tile_end[None, :] <= steps[:, None], axis=1).astype(jnp.int32)
    valid = slot_of_row >= 0
    src_tok = jnp.where(valid, slot_of_row % t, 0).reshape(nt, tm)
    tile_buf = jnp.arange(-1, nt, dtype=jnp.int32) % MOE_BUFS
    dump = 2 * t + tile_buf[:, None] * tm + jnp.arange(tm, dtype=jnp.int32)[None, :]
    dst_row = jnp.where(valid.reshape(nt, tm), slot_of_row.reshape(nt, tm), dump[1:])
    plan_src = jnp.concatenate([src_tok[2:], jnp.zeros((3, tm), jnp.int32)], axis=0)
    plan_dst = jnp.concatenate([dump[0:1], dst_row], axis=0)
    plan = jnp.stack([plan_src, plan_dst], axis=1)
    return src_tok[0:2], plan, tile_expert, n_used.reshape(1).astype(jnp.int32)


def _final_body(x_ref, y0_ref, y1_ref, wt_ref, pmod_ref, g_ref, o_ref):
    x = _combine(x_ref[...], y0_ref, y1_ref, wt_ref, pmod_ref[0, 5:6, :])
    ms = jnp.mean(x * x, axis=-1, keepdims=True)
    o_ref[...] = x * lax.rsqrt(ms + NORM_EPS) * g_ref[...]


def _final(x, y2, wts, pmod, g_final, seq):
    t, d = x.shape
    tm = FINAL_TM
    tiles_per_seq = seq // tm
    row = lambda i: (i, 0)
    return pl.pallas_call(
        _final_body,
        grid=(t // tm,),
        in_specs=[
            pl.BlockSpec((tm, d), row),
            pl.BlockSpec((tm, d), row),
            pl.BlockSpec((tm, d), lambda i: (i + t // tm, 0)),
            pl.BlockSpec((tm, LANES), row),
            pl.BlockSpec((1, N_MOD, d), lambda i: (i // tiles_per_seq, 0, 0)),
            pl.BlockSpec((1, d), lambda i: (0, 0)),
        ],
        out_specs=pl.BlockSpec((tm, d), row),
        out_shape=jax.ShapeDtypeStruct((t, d), F32),
        compiler_params=_cparams(("arbitrary",)),
        name="final",
    )(x, y2, y2, wts, pmod, g_final)


def _rotary_tables(positions):
    half = ROT_DIM // 2
    inv_freq = ROPE_THETA ** (-(jnp.arange(0, ROT_DIM, 2, dtype=F32) / ROT_DIM))
    ang = positions.reshape(-1).astype(F32)[:, None] * inv_freq
    cos, sin = jnp.cos(ang), jnp.sin(ang)
    t = ang.shape[0]
    pad = HEAD_DIM - ROT_DIM
    cos_h = jnp.concatenate([cos, cos, jnp.ones((t, pad), F32)], axis=1)
    sa_h = jnp.concatenate([-sin, jnp.zeros((t, HEAD_DIM - half), F32)], axis=1)
    sb_h = jnp.concatenate([jnp.zeros((t, half), F32), sin, jnp.zeros((t, pad), F32)], axis=1)
    rep = LANES // HEAD_DIM
    return jnp.tile(cos_h, (1, rep)), jnp.tile(sa_h, (1, rep)), jnp.tile(sb_h, (1, rep))


def kernel(x, c, positions, w_ada, b_ada, g_norm1, w_in, b_in, sinks, ln_v_g, ln_v_b, w_spatial, b_spatial,
           g_attn_out, g_gmlp_out, w_out, g_norm2, w_router_group, b_router_group, w_router_expert,
           b_router_expert, w_e_gate, w_e_up, w_e_down, g_final):
    batch, seq, d = x.shape
    depth = w_ada.shape[0]
    t = batch * seq
    assert d == D_MODEL and seq % max(PROJ_TM, OUT_TM, FINAL_TM, MIX_BLOCKS * BLOCK) == 0 and (2 * t) % MOE_TM == 0

    c_pad = jnp.zeros((SUBLANES, d), F32).at[:batch].set(c)
    mod = _adaln(c_pad, w_ada, b_ada).reshape(depth, SUBLANES, N_MOD, d)
    cos_t, sa_t, sb_t = _rotary_tables(positions)

    n_route = N_GROUPS + N_EXPERTS
    w_router = jnp.concatenate(
        [w_router_group, w_router_expert, jnp.zeros((depth, d, LANES - n_route), F32)], axis=-1)
    b_router = jnp.concatenate(
        [b_router_group, b_router_expert, jnp.zeros((depth, LANES - n_route), F32)], axis=-1)

    xt = x.reshape(t, d)
    prev = None
    for l in range(depth):
        xt, qkv, z = _proj(xt, prev, mod[l], g_norm1[l][None], w_in[l].astype(BF16), b_in[l][None],
                           cos_t, sa_t, sb_t, seq)
        merged = _mixer(qkv, z, sinks[l], w_spatial[l], b_spatial[l].T, ln_v_g[l][None], ln_v_b[l][None],
                        g_attn_out[l][None], g_gmlp_out[l][None], batch, seq)
        xt, h2, ids, wts, counts = _outproj(merged, xt, mod[l], g_norm2[l][None], w_out[l].astype(BF16),
                                            w_router[l].astype(BF16), b_router[l][None], seq)
        first_src, plan, tile_expert, n_used = _route_plan(ids, counts, t)
        y2 = _moe(l, h2, first_src, plan, tile_expert, n_used, w_e_gate, w_e_up, w_e_down)
        prev = (y2, wts, mod[l])
    out = _final(xt, prev[0], prev[1], prev[2], g_final[None], seq)
    return out.reshape(batch, seq, d)
```

```python
import functools

import jax
import jax.numpy as jnp
from jax import lax
from jax.experimental import pallas as pl
from jax.experimental.pallas import tpu as pltpu

F32 = jnp.float32
BF16 = jnp.bfloat16

D_MODEL = 2048
ATTN_WIDTH = D_MODEL // 2
HEAD_DIM = 64
N_Q_HEADS = ATTN_WIDTH // HEAD_DIM
N_KV_HEADS = 2
GQA_GROUP = N_Q_HEADS // N_KV_HEADS
WINDOW = 128
BLOCK = 128
ROT_DIM = HEAD_DIM // 4
ROPE_THETA = 500000.0
GMLP_WIDTH = D_MODEL - ATTN_WIDTH
GMLP_GROUPS = 8
GMLP_GROUP_DIM = GMLP_WIDTH // GMLP_GROUPS
Q_COLS = N_Q_HEADS * HEAD_DIM
KV_COLS = N_KV_HEADS * HEAD_DIM
QKV_COLS = Q_COLS + 2 * KV_COLS
IN_COLS = QKV_COLS + 2 * GMLP_WIDTH
N_GROUPS = 4
EXPERTS_PER_GROUP = 8
N_EXPERTS = N_GROUPS * EXPERTS_PER_GROUP
D_EXPERT = D_MODEL // 4
NORM_EPS = 1e-6
N_MOD = 6

LANES = 128
SUBLANES = 8
VMEM_LIMIT_BYTES = 56 * 1024 * 1024

ADA_TN = 1024
PROJ_TM = 256
OUT_TM = 512
MIX_BLOCKS = 2
MOE_TM = 256
MOE_N_CHUNK = 256
MOE_BUFS = 3
FINAL_TM = 512


def _cparams(sem):
    return pltpu.CompilerParams(dimension_semantics=sem, vmem_limit_bytes=VMEM_LIMIT_BYTES)


def _adaln_body(c_ref, w_ref, b_ref, o_ref):
    c = c_ref[...]
    ca = (c * jax.nn.sigmoid(c)).astype(BF16)
    w = w_ref[0].astype(BF16)
    o_ref[0] = jnp.dot(ca, w, preferred_element_type=F32) + b_ref[0]


def _adaln(c_pad, w_ada, b_ada):
    depth, d, n = w_ada.shape
    rows = c_pad.shape[0]
    return pl.pallas_call(
        _adaln_body,
        grid=(depth, n // ADA_TN),
        in_specs=[
            pl.BlockSpec((rows, d), lambda l, j: (0, 0)),
            pl.BlockSpec((1, d, ADA_TN), lambda l, j: (l, 0, j)),
            pl.BlockSpec((1, 1, ADA_TN), lambda l, j: (l, 0, j)),
        ],
        out_specs=pl.BlockSpec((1, rows, ADA_TN), lambda l, j: (l, 0, j)),
        out_shape=jax.ShapeDtypeStruct((depth, rows, n), F32),
        compiler_params=_cparams(("arbitrary", "arbitrary")),
        name="adaln",
    )(c_pad, w_ada, b_ada.reshape(depth, 1, n))


def _rms_mod(x, g, sc, sh):
    ms = jnp.mean(x * x, axis=-1, keepdims=True)
    return (x * lax.rsqrt(ms + NORM_EPS) * g) * (1.0 + sc) + sh


def _pack_bf16_halves(h):
    half = h.shape[1] // 2
    lo = lax.bitcast_convert_type(h[:, :half].astype(BF16).astype(F32), jnp.uint32)
    hi = lax.bitcast_convert_type(h[:, half:].astype(BF16).astype(F32), jnp.uint32)
    return lax.shift_right_logical(lo, jnp.uint32(16)) | (hi & jnp.uint32(0xFFFF0000))


def _unpack_bf16_halves(p):
    lo = lax.bitcast_convert_type(lax.shift_left(p, jnp.uint32(16)), F32).astype(BF16)
    hi = lax.bitcast_convert_type(p & jnp.uint32(0xFFFF0000), F32).astype(BF16)
    return jnp.concatenate([lo, hi], axis=1)


def _combine(x, y0_ref, y1_ref, wt_ref, gate):
    wt = wt_ref[...]
    y = wt[:, 0:1] * y0_ref[...] + wt[:, 1:2] * y1_ref[...]
    return x + (1.0 + gate) * y


def _gelu_exact(z):
    return 0.5 * z * (1.0 + lax.erf(z * (2.0 ** -0.5)))


def _proj_body(has_prev, *refs):
    if has_prev:
        (x_ref, y0_ref, y1_ref, wt_ref, pmod_ref, mod_ref, g_ref, w_ref, b_ref, cos_ref, sa_ref, sb_ref,
         xo_ref, qkv_ref, z_ref) = refs
        x = _combine(x_ref[...], y0_ref, y1_ref, wt_ref, pmod_ref[0, 5:6, :])
        xo_ref[...] = x
    else:
        x_ref, mod_ref, g_ref, w_ref, b_ref, cos_ref, sa_ref, sb_ref, qkv_ref, z_ref = refs
        x = x_ref[...]
    mod = mod_ref[0]
    h = _rms_mod(x, g_ref[...], mod[1:2, :], mod[0:1, :]).astype(BF16)

    qkv = jnp.dot(h, w_ref[:, :QKV_COLS], preferred_element_type=F32) + b_ref[:, :QKV_COLS]
    cos, sa, sb = cos_ref[...], sa_ref[...], sb_ref[...]
    n_rot = (Q_COLS + KV_COLS) // LANES
    for j in range(QKV_COLS // LANES):
        ch = qkv[:, j * LANES:(j + 1) * LANES]
        if j < n_rot:
            ch = ch * cos + pltpu.roll(ch, LANES - ROT_DIM // 2, 1) * sa + pltpu.roll(ch, ROT_DIM // 2, 1) * sb
        if j < Q_COLS // LANES:
            ch = ch * (HEAD_DIM ** -0.5)
        qkv_ref[:, j * LANES:(j + 1) * LANES] = ch.astype(BF16)

    z = jnp.dot(h, w_ref[:, QKV_COLS:], preferred_element_type=F32) + b_ref[:, QKV_COLS:]
    z_ref[...] = _gelu_exact(z)


def _proj(x, prev, mod_l, g1, w_in, b_in, cos_t, sa_t, sb_t, seq):
    t, d = x.shape
    tm = PROJ_TM
    tiles_per_seq = seq // tm
    row = lambda i: (i, 0)
    const = lambda i: (0, 0)
    modmap = lambda i: (i // tiles_per_seq, 0, 0)
    in_specs = [pl.BlockSpec((tm, d), row)]
    args = [x]
    if prev is not None:
        y2, wts, pmod = prev
        in_specs += [pl.BlockSpec((tm, d), row), pl.BlockSpec((tm, d), lambda i: (i + t // tm, 0)),
                     pl.BlockSpec((tm, LANES), row), pl.BlockSpec((1, N_MOD, d), modmap)]
        args += [y2, y2, wts, pmod]
    in_specs += [
        pl.BlockSpec((1, N_MOD, d), modmap),
        pl.BlockSpec((1, d), const),
        pl.BlockSpec((d, IN_COLS), const, pipeline_mode=pl.Buffered(1)),
        pl.BlockSpec((1, IN_COLS), const),
        pl.BlockSpec((tm, LANES), row),
        pl.BlockSpec((tm, LANES), row),
        pl.BlockSpec((tm, LANES), row),
    ]
    args += [mod_l, g1, w_in, b_in, cos_t, sa_t, sb_t]
    out_specs = [pl.BlockSpec((tm, QKV_COLS), row), pl.BlockSpec((tm, 2 * GMLP_WIDTH), row)]
    out_shape = [jax.ShapeDtypeStruct((t, QKV_COLS), BF16), jax.ShapeDtypeStruct((t, 2 * GMLP_WIDTH), F32)]
    if prev is not None:
        out_specs = [pl.BlockSpec((tm, d), row)] + out_specs
        out_shape = [jax.ShapeDtypeStruct((t, d), F32)] + out_shape
    outs = pl.pallas_call(
        functools.partial(_proj_body, prev is not None),
        grid=(t // tm,),
        in_specs=in_specs,
        out_specs=out_specs,
        out_shape=out_shape,
        compiler_params=_cparams(("arbitrary",)),
        name="proj",
    )(*args)
    if prev is None:
        return (x,) + tuple(outs)
    return tuple(outs)


def _mixer_body(sink_ref, q_ref, kvc_ref, kvp_ref, z_ref, ws_ref, bst_ref, lng_ref, lnb_ref, ga_ref, gg_ref,
                o_ref):
    n = pl.program_id(1)
    for sub in range(MIX_BLOCKS):
        r0 = sub * BLOCK
        kvp = kvp_ref[...] if sub == 0 else kvc_ref[r0 - BLOCK:r0, :]
        has_prev = (n > 0) if sub == 0 else None
        _mixer_block(sink_ref, q_ref[r0:r0 + BLOCK, :], kvc_ref[r0:r0 + BLOCK, :], kvp, has_prev, z_ref, r0,
                     ws_ref, bst_ref, lng_ref, lnb_ref, ga_ref, gg_ref, o_ref)


def _mixer_block(sink_ref, q, kvc, kvp, has_prev, z_ref, r0, ws_ref, bst_ref, lng_ref, lnb_ref, ga_ref, gg_ref,
                 o_ref):
    rows = slice(r0, r0 + BLOCK)
    kcat = jnp.concatenate([kvp[:, :KV_COLS], kvc[:, :KV_COLS]], axis=0)
    vcat = jnp.concatenate([kvp[:, KV_COLS:], kvc[:, KV_COLS:]], axis=0)

    lane = lax.broadcasted_iota(jnp.int32, (BLOCK, LANES), 1)
    lo = lane < HEAD_DIM
    kj = lax.broadcasted_iota(jnp.int32, (2 * BLOCK, BLOCK), 0)
    qi = lax.broadcasted_iota(jnp.int32, (2 * BLOCK, BLOCK), 1)
    rel = qi + BLOCK - kj
    valid = (rel >= 0) & (rel < WINDOW)
    if has_prev is not None:
        valid = valid & (has_prev | (kj >= BLOCK))
    row_lo = lax.broadcasted_iota(jnp.int32, (LANES, BLOCK), 0) < HEAD_DIM
    zero_q = jnp.zeros((BLOCK, LANES), BF16)
    nt_dims = (((1,), (1,)), ((), ()))

    attn = []
    chunks_per_kv = GQA_GROUP * HEAD_DIM // LANES
    for hk in range(N_KV_HEADS):
        ksl = kcat[:, hk * HEAD_DIM:(hk + 1) * HEAD_DIM]
        vsl = vcat[:, hk * HEAD_DIM:(hk + 1) * HEAD_DIM]
        kk = jnp.concatenate([ksl, ksl], axis=1)
        vt = jnp.concatenate([vsl, vsl], axis=1).astype(F32).T.astype(BF16)
        for j in range(chunks_per_kv):
            c = hk * chunks_per_kv + j
            qc = q[:, c * LANES:(c + 1) * LANES]
            probs = []
            for half in range(2):
                qm = jnp.where(lo, qc, zero_q) if half == 0 else jnp.where(lo, zero_q, qc)
                s = lax.dot_general(kk, qm, nt_dims, preferred_element_type=F32)
                s = jnp.where(valid, s, -jnp.inf)
                sink = sink_ref[2 * c + half]
                m = jnp.maximum(jnp.max(s, axis=0, keepdims=True), sink)
                p = jnp.exp(s - m)
                denom = jnp.sum(p, axis=0, keepdims=True) + jnp.exp(sink - m)
                probs.append((p * (1.0 / denom)).astype(BF16))
            pn = jnp.concatenate(probs, axis=1)
            ot = jnp.dot(vt, pn, preferred_element_type=F32)
            sel = jnp.where(row_lo, ot[:, :BLOCK], ot[:, BLOCK:])
            attn.append(sel.T)

    ss = attn[0] * attn[0]
    for a in attn[1:]:
        ss = ss + a * a
    scale = lax.rsqrt(jnp.sum(ss, axis=-1, keepdims=True) * (1.0 / ATTN_WIDTH) + NORM_EPS)
    for c, a in enumerate(attn):
        o_ref[rows, c * LANES:(c + 1) * LANES] = (a * scale * ga_ref[:, c * LANES:(c + 1) * LANES]).astype(BF16)

    u = z_ref[rows, :GMLP_WIDTH]
    v = z_ref[rows, GMLP_WIDTH:]
    mu = jnp.mean(v, axis=-1, keepdims=True)
    vc = v - mu
    var = jnp.mean(vc * vc, axis=-1, keepdims=True)
    vn = ((vc * lax.rsqrt(var + NORM_EPS)) * lng_ref[...] + lnb_ref[...]).astype(BF16)
    ti = lax.broadcasted_iota(jnp.int32, (BLOCK, BLOCK), 0)
    si = lax.broadcasted_iota(jnp.int32, (BLOCK, BLOCK), 1)
    tril = si <= ti
    bst = bst_ref[...]
    gm = []
    for h in range(GMLP_GROUPS):
        w = jnp.where(tril, ws_ref[h], 0.0).astype(BF16)
        sp = jnp.dot(w, vn[:, h * LANES:(h + 1) * LANES], preferred_element_type=F32) + bst[:, h:h + 1]
        gm.append(u[:, h * LANES:(h + 1) * LANES] * sp)
    ss = gm[0] * gm[0]
    for a in gm[1:]:
        ss = ss + a * a
    scale = lax.rsqrt(jnp.sum(ss, axis=-1, keepdims=True) * (1.0 / GMLP_WIDTH) + NORM_EPS)
    for c, a in enumerate(gm):
        o_ref[rows, ATTN_WIDTH + c * LANES:ATTN_WIDTH + (c + 1) * LANES] = (
            a * scale * gg_ref[:, c * LANES:(c + 1) * LANES]).astype(BF16)


def _mixer(qkv, z, sinks, w_sp, bst, ln_g, ln_b, g_attn, g_gmlp, batch, seq):
    t = qkv.shape[0]
    nb = seq // BLOCK
    steps = nb // MIX_BLOCKS
    rows = MIX_BLOCKS * BLOCK
    kv_blk = Q_COLS // (2 * KV_COLS)
    cur = lambda b, n, s: (b * steps + n, 0)
    const2 = lambda b, n, s: (0, 0)
    grid_spec = pltpu.PrefetchScalarGridSpec(
        num_scalar_prefetch=1,
        grid=(batch, steps),
        in_specs=[
            pl.BlockSpec((rows, Q_COLS), cur),
            pl.BlockSpec((rows, 2 * KV_COLS), lambda b, n, s: (b * steps + n, kv_blk)),
            pl.BlockSpec((BLOCK, 2 * KV_COLS),
                         lambda b, n, s: (b * nb + jnp.maximum(MIX_BLOCKS * n - 1, 0), kv_blk)),
            pl.BlockSpec((rows, 2 * GMLP_WIDTH), cur),
            pl.BlockSpec((GMLP_GROUPS, BLOCK, BLOCK), lambda b, n, s: (0, 0, 0)),
            pl.BlockSpec((BLOCK, GMLP_GROUPS), const2),
            pl.BlockSpec((1, GMLP_WIDTH), const2),
            pl.BlockSpec((1, GMLP_WIDTH), const2),
            pl.BlockSpec((1, ATTN_WIDTH), const2),
            pl.BlockSpec((1, GMLP_WIDTH), const2),
        ],
        out_specs=pl.BlockSpec((rows, D_MODEL), cur),
    )
    return pl.pallas_call(
        _mixer_body,
        grid_spec=grid_spec,
        out_shape=jax.ShapeDtypeStruct((t, D_MODEL), BF16),
        compiler_params=_cparams(("arbitrary", "arbitrary")),
        name="mixer",
    )(sinks, qkv, qkv, qkv, z, w_sp, bst, ln_g, ln_b, g_attn, g_gmlp)


def _first_index_of_max(vals, lane, width):
    mx = jnp.max(vals, axis=-1, keepdims=True)
    idx = jnp.min(jnp.where(vals == mx, lane, width), axis=-1, keepdims=True)
    return mx, idx


def _outproj_body(m_ref, x_ref, mod_ref, g_ref, w_ref, wr_ref, br_ref,
                  xo_ref, h_ref, ids_ref, wt_ref, cnt_ref, carry_ref):
    i = pl.program_id(0)
    tm = x_ref.shape[0]

    @pl.when(i == 0)
    def _():
        carry_ref[...] = jnp.zeros_like(carry_ref)

    mod = mod_ref[0]
    y = jnp.dot(m_ref[...], w_ref[...], preferred_element_type=F32)
    x = x_ref[...] + (1.0 + mod[2:3, :]) * y
    xo_ref[...] = x
    h = _rms_mod(x, g_ref[...], mod[4:5, :], mod[3:4, :])
    h_ref[...] = _pack_bf16_halves(h)

    logits = jnp.dot(h.astype(BF16), wr_ref[...], preferred_element_type=F32) + br_ref[...]
    lane = lax.broadcasted_iota(jnp.int32, (tm, LANES), 1)
    neg = -jnp.inf
    gl = jnp.where(lane < N_GROUPS, logits, neg)
    ge = jnp.exp(gl - jnp.max(gl, axis=-1, keepdims=True))
    gprob = ge / jnp.sum(ge, axis=-1, keepdims=True)
    g_p, g_idx = _first_index_of_max(gprob, lane, LANES)

    base = N_GROUPS + EXPERTS_PER_GROUP * g_idx
    in_grp = (lane >= base) & (lane < base + EXPERTS_PER_GROUP)
    el = jnp.where(in_grp, logits, neg)
    ee = jnp.exp(el - jnp.max(el, axis=-1, keepdims=True))
    eprob = jnp.where(in_grp, ee / jnp.sum(ee, axis=-1, keepdims=True), -1.0)
    p1, i1 = _first_index_of_max(eprob, lane, LANES)
    eprob2 = jnp.where(lane == i1, -1.0, eprob)
    p2, i2 = _first_index_of_max(eprob2, lane, LANES)
    psum = p1 + p2
    w1 = g_p * (p1 / psum)
    w2 = g_p * (p2 / psum)
    e1 = i1 - N_GROUPS
    e2 = i2 - N_GROUPS

    oh1 = lane == e1
    oh2 = lane == e2
    onehot = jnp.where(oh1 | oh2, 1.0, 0.0)
    ri = lax.broadcasted_iota(jnp.int32, (tm, tm), 0)
    ci = lax.broadcasted_iota(jnp.int32, (tm, tm), 1)
    lower = jnp.where(ci < ri, 1.0, 0.0).astype(BF16)
    before = jnp.dot(lower, onehot.astype(BF16), preferred_element_type=F32) + carry_ref[...]
    r1 = jnp.sum(jnp.where(oh1, before, 0.0), axis=-1, keepdims=True)
    r2 = jnp.sum(jnp.where(oh2, before, 0.0), axis=-1, keepdims=True)
    carry = carry_ref[...] + jnp.sum(onehot, axis=0, keepdims=True)
    carry_ref[...] = carry
    cnt_ref[...] = jnp.broadcast_to(carry, cnt_ref.shape).astype(jnp.int32)

    ids = jnp.where(lane == 0, e1, jnp.where(lane == 1, e2, 0))
    ranks = jnp.where(lane == 2, r1, jnp.where(lane == 3, r2, 0.0)).astype(jnp.int32)
    ids_ref[...] = ids + ranks
    wt_ref[...] = jnp.where(lane == 0, w1, jnp.where(lane == 1, w2, 0.0))


def _outproj(merged, x, mod_l, g2, w_out, w_router, b_router, seq):
    t, d = x.shape
    tm = OUT_TM
    tiles_per_seq = seq // tm
    row = lambda i: (i, 0)
    const = lambda i: (0, 0)
    return pl.pallas_call(
        _outproj_body,
        grid=(t // tm,),
        in_specs=[
            pl.BlockSpec((tm, d), row),
            pl.BlockSpec((tm, d), row),
            pl.BlockSpec((1, N_MOD, d), lambda i: (i // tiles_per_seq, 0, 0)),
            pl.BlockSpec((1, d), const),
            pl.BlockSpec((d, d), const, pipeline_mode=pl.Buffered(1)),
            pl.BlockSpec((d, LANES), const),
            pl.BlockSpec((1, LANES), const),
        ],
        out_specs=[
            pl.BlockSpec((tm, d), row),
            pl.BlockSpec((tm, d // 2), row),
            pl.BlockSpec((tm, LANES), row),
            pl.BlockSpec((tm, LANES), row),
            pl.BlockSpec((SUBLANES, LANES), const),
        ],
        out_shape=[
            jax.ShapeDtypeStruct((t, d), F32),
            jax.ShapeDtypeStruct((t, d // 2), jnp.uint32),
            jax.ShapeDtypeStruct((t, LANES), jnp.int32),
            jax.ShapeDtypeStruct((t, LANES), F32),
            jax.ShapeDtypeStruct((SUBLANES, LANES), jnp.int32),
        ],
        scratch_shapes=[pltpu.VMEM((1, LANES), F32)],
        compiler_params=_cparams(("arbitrary",)),
        name="outproj",
    )(merged, x, mod_l, g2, w_out, w_router, b_router)


def _moe_body(layer, n_slots, te_ref, nx_ref, nu_ref, h_hbm, first_hbm, plan_hbm, wg_hbm, wu_hbm, wd_hbm, y_hbm,
              first_smem, plan_smem, xbuf, obuf, wgs, wus, wds, wgb, wub, wdb,
              first_sem, plan_sem, g_sem, s_sem, w_sem):
    i = pl.program_id(0)
    n_used = nu_ref[0]
    nb, tm, d = obuf.shape
    de = wgb.shape[1]

    def weight_copies(e):
        return (pltpu.make_async_copy(wg_hbm.at[layer, e], wgs, w_sem.at[0]),
                pltpu.make_async_copy(wu_hbm.at[layer, e], wus, w_sem.at[1]),
                pltpu.make_async_copy(wd_hbm.at[layer, e], wds, w_sem.at[2]))
    b0 = i % nb
    b1 = (i + 1) % nb
    b2 = (i + 2) % nb
    p0 = i % 2
    p1 = 1 - p0

    def plan_copy(k, slot):
        return pltpu.make_async_copy(plan_hbm.at[k], plan_smem.at[slot], plan_sem.at[slot])

    def gather_row(tok, buf, j):
        return pltpu.make_async_copy(h_hbm.at[pl.ds(tok, 1), :], xbuf.at[buf, pl.ds(j, 1), :], g_sem.at[buf])

    def scatter_row(dst, buf, j):
        return pltpu.make_async_copy(obuf.at[buf, pl.ds(j, 1), :], y_hbm.at[pl.ds(dst, 1), :], s_sem.at[buf])

    def wait_gather(buf):
        pltpu.make_async_copy(h_hbm.at[pl.ds(0, tm), :], xbuf.at[buf], g_sem.at[buf]).wait()

    def wait_scatter(buf):
        pltpu.make_async_copy(obuf.at[buf], y_hbm.at[pl.ds(0, tm), :], s_sem.at[buf]).wait()

    @pl.when(i == 0)
    def _():
        first = pltpu.make_async_copy(first_hbm, first_smem, first_sem.at[0])
        first.start()
        plan_copy(0, 0).start()
        for cp in weight_copies(te_ref[0]):
            cp.start()
        first.wait()
        for tile in range(2):
            def body(j, carry, tile=tile):
                gather_row(first_smem[tile, j], tile, j).start()
                return carry
            lax.fori_loop(0, tm, body, 0, unroll=8)
        obuf[nb - 1] = jnp.zeros((tm, d), F32)

    @pl.when(i < n_used)
    def _():
        plan_copy(i, p0).wait()
        plan_copy(i + 1, p1).start()
        wait_gather(b0)

        @pl.when(i >= 2)
        def _():
            wait_scatter(b0)

        new_expert = jnp.logical_or(i == 0, te_ref[i] != te_ref[jnp.maximum(i - 1, 0)])

        @pl.when(new_expert)
        def _():
            for cp in weight_copies(te_ref[i]):
                cp.wait()
            wgb[...] = wgs[...].astype(BF16)
            wub[...] = wus[...].astype(BF16)
            wdb[...] = wds[...].astype(BF16)

            @pl.when(nx_ref[i] != te_ref[i])
            def _():
                for cp in weight_copies(nx_ref[i]):
                    cp.start()

        copies = []
        for j in range(tm):
            copies += [(0, j), (1, j)]
        n_up = de // MOE_N_CHUNK
        n_down = d // MOE_N_CHUNK
        n_pieces = 2 * n_up + n_down
        per_piece = -(-len(copies) // n_pieces)

        def issue(piece):
            for kind, j in copies[piece * per_piece:(piece + 1) * per_piece]:
                if kind == 0:
                    gather_row(plan_smem[p0, 0, j], b2, j).start()
                else:
                    scatter_row(plan_smem[p0, 1, j], b2, j).start()

        xb = _unpack_bf16_halves(xbuf[b0])
        piece = 0
        acts = []
        for n in range(n_up):
            cols = slice(n * MOE_N_CHUNK, (n + 1) * MOE_N_CHUNK)
            g = jnp.dot(xb, wgb[:, cols], preferred_element_type=F32)
            issue(piece)
            u = jnp.dot(xb, wub[:, cols], preferred_element_type=F32)
            issue(piece + 1)
            piece += 2
            acts.append(((g * jax.nn.sigmoid(g)) * u).astype(BF16))
        a = jnp.concatenate(acts, axis=1)
        for n in range(n_down):
            cols = slice(n * MOE_N_CHUNK, (n + 1) * MOE_N_CHUNK)
            obuf[b0, :, cols] = jnp.dot(a, wdb[:, cols], preferred_element_type=F32)
            issue(piece)
            piece += 1

    @pl.when(i == n_used)
    def _():
        plan_copy(i, p0).wait()
        wait_gather(b0)
        wait_gather(b1)
        wait_scatter(b1)

        @pl.when(i >= 2)
        def _():
            wait_scatter(b0)

        def body(j, carry):
            scatter_row(plan_smem[p0, 1, j], b2, j).start()
            return carry
        lax.fori_loop(0, tm, body, 0, unroll=8)
        wait_scatter(b2)
        obuf[b0] = jnp.zeros((tm, d), F32)
        for k in range(nb):
            fill = pltpu.make_async_copy(obuf.at[b0], y_hbm.at[pl.ds(n_slots + k * tm, tm), :], s_sem.at[b0])
            fill.start()
            fill.wait()


def _moe(layer, h2, first_src, plan, tile_expert, next_expert, n_used, w_gate, w_up, w_down):
    t = h2.shape[0]
    n_slots = 2 * t
    steps, _, tm = plan.shape
    d, de = w_gate.shape[-2:]
    hbm = pl.BlockSpec(memory_space=pl.ANY)
    grid_spec = pltpu.PrefetchScalarGridSpec(
        num_scalar_prefetch=3,
        grid=(steps,),
        in_specs=[hbm] * 6,
        out_specs=hbm,
        scratch_shapes=[
            pltpu.SMEM((2, tm), jnp.int32),
            pltpu.SMEM((2, 2, tm), jnp.int32),
            pltpu.VMEM((MOE_BUFS, tm, d // 2), jnp.uint32),
            pltpu.VMEM((MOE_BUFS, tm, d), F32),
            pltpu.VMEM((d, de), F32),
            pltpu.VMEM((d, de), F32),
            pltpu.VMEM((de, d), F32),
            pltpu.VMEM((d, de), BF16),
            pltpu.VMEM((d, de), BF16),
            pltpu.VMEM((de, d), BF16),
            pltpu.SemaphoreType.DMA((1,)),
            pltpu.SemaphoreType.DMA((2,)),
            pltpu.SemaphoreType.DMA((MOE_BUFS,)),
            pltpu.SemaphoreType.DMA((MOE_BUFS,)),
            pltpu.SemaphoreType.DMA((3,)),
        ],
    )
    return pl.pallas_call(
        functools.partial(_moe_body, layer, n_slots),
        grid_spec=grid_spec,
        out_shape=jax.ShapeDtypeStruct((n_slots + MOE_BUFS * tm, d), F32),
        compiler_params=_cparams(("arbitrary",)),
        name="moe",
    )(tile_expert, next_expert, n_used, h2, first_src, plan, w_gate, w_up, w_down)


def _route_plan(ids, counts, t):
    tm = MOE_TM
    nt = (2 * t) // tm + N_EXPERTS
    e = ids[:, 0:2]
    rank = ids[:, 2:4]
    cnt = counts[0, :N_EXPERTS]
    tiles = (cnt + tm - 1) // tm
    tile_end = jnp.cumsum(tiles)
    off = (tile_end - tiles) * tm
    pos = off[e] + rank
    slot_ids = jnp.arange(t, dtype=jnp.int32)[:, None] + t * jnp.arange(2, dtype=jnp.int32)[None, :]
    slot_of_row = jnp.full((nt * tm,), -1, jnp.int32).at[pos.reshape(-1)].set(
        slot_ids.reshape(-1), unique_indices=True)
    n_used = tile_end[-1]
    steps = jnp.minimum(jnp.arange(nt + 1, dtype=jnp.int32), n_used - 1)
    tile_expert = jnp.sum(tile_end[None, :] <= steps[:, None], axis=1).astype(jnp.int32)
    eid = jnp.arange(N_EXPERTS, dtype=jnp.int32)
    later_used = (tiles[None, :] > 0) & (eid[None, :] > eid[:, None])
    nxt = jnp.min(jnp.where(later_used, eid[None, :], N_EXPERTS), axis=1)
    next_expert = jnp.where(nxt == N_EXPERTS, eid, nxt)[tile_expert].astype(jnp.int32)
    valid = slot_of_row >= 0
    src_tok = jnp.where(valid, slot_of_row % t, 0).reshape(nt, tm)
    tile_buf = jnp.arange(-1, nt, dtype=jnp.int32) % MOE_BUFS
    dump = 2 * t + tile_buf[:, None] * tm + jnp.arange(tm, dtype=jnp.int32)[None, :]
    dst_row = jnp.where(valid.reshape(nt, tm), slot_of_row.reshape(nt, tm), dump[1:])
    plan_src = jnp.concatenate([src_tok[2:], jnp.zeros((3, tm), jnp.int32)], axis=0)
    plan_dst = jnp.concatenate([dump[0:1], dst_row], axis=0)
    plan = jnp.stack([plan_src, plan_dst], axis=1)
    return src_tok[0:2], plan, tile_expert, next_expert, n_used.reshape(1).astype(jnp.int32)


def _final_body(x_ref, y0_ref, y1_ref, wt_ref, pmod_ref, g_ref, o_ref):
    x = _combine(x_ref[...], y0_ref, y1_ref, wt_ref, pmod_ref[0, 5:6, :])
    ms = jnp.mean(x * x, axis=-1, keepdims=True)
    o_ref[...] = x * lax.rsqrt(ms + NORM_EPS) * g_ref[...]


def _final(x, y2, wts, pmod, g_final, seq):
    t, d = x.shape
    tm = FINAL_TM
    tiles_per_seq = seq // tm
    row = lambda i: (i, 0)
    return pl.pallas_call(
        _final_body,
        grid=(t // tm,),
        in_specs=[
            pl.BlockSpec((tm, d), row),
            pl.BlockSpec((tm, d), row),
            pl.BlockSpec((tm, d), lambda i: (i + t // tm, 0)),
            pl.BlockSpec((tm, LANES), row),
            pl.BlockSpec((1, N_MOD, d), lambda i: (i // tiles_per_seq, 0, 0)),
            pl.BlockSpec((1, d), lambda i: (0, 0)),
        ],
        out_specs=pl.BlockSpec((tm, d), row),
        out_shape=jax.ShapeDtypeStruct((t, d), F32),
        compiler_params=_cparams(("arbitrary",)),
        name="final",
    )(x, y2, y2, wts, pmod, g_final)


def _rotary_tables(positions):
    half = ROT_DIM // 2
    inv_freq = ROPE_THETA ** (-(jnp.arange(0, ROT_DIM, 2, dtype=F32) / ROT_DIM))
    ang = positions.reshape(-1).astype(F32)[:, None] * inv_freq
    cos, sin = jnp.cos(ang), jnp.sin(ang)
    t = ang.shape[0]
    pad = HEAD_DIM - ROT_DIM
    cos_h = jnp.concatenate([cos, cos, jnp.ones((t, pad), F32)], axis=1)
    sa_h = jnp.concatenate([-sin, jnp.zeros((t, HEAD_DIM - half), F32)], axis=1)
    sb_h = jnp.concatenate([jnp.zeros((t, half), F32), sin, jnp.zeros((t, pad), F32)], axis=1)
    rep = LANES // HEAD_DIM
    return jnp.tile(cos_h, (1, rep)), jnp.tile(sa_h, (1, rep)), jnp.tile(sb_h, (1, rep))


def kernel(x, c, positions, w_ada, b_ada, g_norm1, w_in, b_in, sinks, ln_v_g, ln_v_b, w_spatial, b_spatial,
           g_attn_out, g_gmlp_out, w_out, g_norm2, w_router_group, b_router_group, w_router_expert,
           b_router_expert, w_e_gate, w_e_up, w_e_down, g_final):
    batch, seq, d = x.shape
    depth = w_ada.shape[0]
    t = batch * seq
    assert d == D_MODEL and seq % max(PROJ_TM, OUT_TM, FINAL_TM, MIX_BLOCKS * BLOCK) == 0 and (2 * t) % MOE_TM == 0

    c_pad = jnp.zeros((SUBLANES, d), F32).at[:batch].set(c)
    mod = _adaln(c_pad, w_ada, b_ada).reshape(depth, SUBLANES, N_MOD, d)
    cos_t, sa_t, sb_t = _rotary_tables(positions)

    n_route = N_GROUPS + N_EXPERTS
    w_router = jnp.concatenate(
        [w_router_group, w_router_expert, jnp.zeros((depth, d, LANES - n_route), F32)], axis=-1)
    b_router = jnp.concatenate(
        [b_router_group, b_router_expert, jnp.zeros((depth, LANES - n_route), F32)], axis=-1)

    xt = x.reshape(t, d)
    prev = None
    for l in range(depth):
        xt, qkv, z = _proj(xt, prev, mod[l], g_norm1[l][None], w_in[l].astype(BF16), b_in[l][None],
                           cos_t, sa_t, sb_t, seq)
        merged = _mixer(qkv, z, sinks[l], w_spatial[l], b_spatial[l].T, ln_v_g[l][None], ln_v_b[l][None],
                        g_attn_out[l][None], g_gmlp_out[l][None], batch, seq)
        xt, h2, ids, wts, counts = _outproj(merged, xt, mod[l], g_norm2[l][None], w_out[l].astype(BF16),
                                            w_router[l].astype(BF16), b_router[l][None], seq)
        first_src, plan, tile_expert, next_expert, n_used = _route_plan(ids, counts, t)
        y2 = _moe(l, h2, first_src, plan, tile_expert, next_expert, n_used, w_e_gate, w_e_up, w_e_down)
        prev = (y2, wts, mod[l])
    out = _final(xt, prev[0], prev[1], prev[2], g_final[None], seq)
    return out.reshape(batch, seq, d)
```

```python
import functools

import jax
import jax.numpy as jnp
import numpy as np
from jax import lax
from jax.experimental import pallas as pl
from jax.experimental.pallas import tpu as pltpu

F32 = jnp.float32
BF16 = jnp.bfloat16

D_MODEL = 2048
ATTN_WIDTH = D_MODEL // 2
HEAD_DIM = 64
N_Q_HEADS = ATTN_WIDTH // HEAD_DIM
N_KV_HEADS = 2
GQA_GROUP = N_Q_HEADS // N_KV_HEADS
WINDOW = 128
BLOCK = 128
ROT_DIM = HEAD_DIM // 4
ROPE_THETA = 500000.0
GMLP_WIDTH = D_MODEL - ATTN_WIDTH
GMLP_GROUPS = 8
GMLP_GROUP_DIM = GMLP_WIDTH // GMLP_GROUPS
Q_COLS = N_Q_HEADS * HEAD_DIM
KV_COLS = N_KV_HEADS * HEAD_DIM
QKV_COLS = Q_COLS + 2 * KV_COLS
IN_COLS = QKV_COLS + 2 * GMLP_WIDTH
N_GROUPS = 4
EXPERTS_PER_GROUP = 8
N_EXPERTS = N_GROUPS * EXPERTS_PER_GROUP
D_EXPERT = D_MODEL // 4
NORM_EPS = 1e-6
N_MOD = 6

LANES = 128
SUBLANES = 8
VMEM_LIMIT_BYTES = 56 * 1024 * 1024

ADA_TN = 1024
PROJ_TM = 256
OUT_TM = 512
MIX_BLOCKS = 2
MOE_TM = 256
MOE_N_CHUNK = 256
MOE_BUFS = 3
FINAL_TM = 512


def _cparams(sem):
    return pltpu.CompilerParams(dimension_semantics=sem, vmem_limit_bytes=VMEM_LIMIT_BYTES)


def _adaln_body(c_ref, w_ref, b_ref, o_ref):
    c = c_ref[...]
    ca = (c * jax.nn.sigmoid(c)).astype(BF16)
    w = w_ref[0].astype(BF16)
    o_ref[0] = jnp.dot(ca, w, preferred_element_type=F32) + b_ref[0]


def _adaln(c_pad, w_ada, b_ada):
    depth, d, n = w_ada.shape
    rows = c_pad.shape[0]
    return pl.pallas_call(
        _adaln_body,
        grid=(depth, n // ADA_TN),
        in_specs=[
            pl.BlockSpec((rows, d), lambda l, j: (0, 0)),
            pl.BlockSpec((1, d, ADA_TN), lambda l, j: (l, 0, j)),
            pl.BlockSpec((1, 1, ADA_TN), lambda l, j: (l, 0, j)),
        ],
        out_specs=pl.BlockSpec((1, rows, ADA_TN), lambda l, j: (l, 0, j)),
        out_shape=jax.ShapeDtypeStruct((depth, rows, n), F32),
        compiler_params=_cparams(("arbitrary", "arbitrary")),
        name="adaln",
    )(c_pad, w_ada, b_ada.reshape(depth, 1, n))


def _rms_mod(x, g, sc, sh):
    ms = jnp.mean(x * x, axis=-1, keepdims=True)
    return (x * lax.rsqrt(ms + NORM_EPS) * g) * (1.0 + sc) + sh


def _pack_bf16_halves(h):
    half = h.shape[1] // 2
    lo = lax.bitcast_convert_type(h[:, :half].astype(BF16).astype(F32), jnp.uint32)
    hi = lax.bitcast_convert_type(h[:, half:].astype(BF16).astype(F32), jnp.uint32)
    return lax.shift_right_logical(lo, jnp.uint32(16)) | (hi & jnp.uint32(0xFFFF0000))


def _unpack_bf16_halves(p):
    lo = lax.bitcast_convert_type(lax.shift_left(p, jnp.uint32(16)), F32).astype(BF16)
    hi = lax.bitcast_convert_type(p & jnp.uint32(0xFFFF0000), F32).astype(BF16)
    return jnp.concatenate([lo, hi], axis=1)


def _combine(x, y0_ref, y1_ref, wt_ref, gate):
    wt = wt_ref[...]
    y = wt[:, 0:1] * y0_ref[...] + wt[:, 1:2] * y1_ref[...]
    return x + (1.0 + gate) * y


def _gelu_exact(z):
    return 0.5 * z * (1.0 + lax.erf(z * (2.0 ** -0.5)))


def _proj_body(has_prev, *refs):
    if has_prev:
        (x_ref, y0_ref, y1_ref, wt_ref, pmod_ref, mod_ref, g_ref, w_ref, b_ref, cos_ref, sa_ref, sb_ref,
         xo_ref, qkv_ref, z_ref) = refs
        x = _combine(x_ref[...], y0_ref, y1_ref, wt_ref, pmod_ref[0, 5:6, :])
        xo_ref[...] = x
    else:
        x_ref, mod_ref, g_ref, w_ref, b_ref, cos_ref, sa_ref, sb_ref, qkv_ref, z_ref = refs
        x = x_ref[...]
    mod = mod_ref[0]
    h = _rms_mod(x, g_ref[...], mod[1:2, :], mod[0:1, :]).astype(BF16)

    qkv = jnp.dot(h, w_ref[:, :QKV_COLS], preferred_element_type=F32) + b_ref[:, :QKV_COLS]
    cos, sa, sb = cos_ref[...], sa_ref[...], sb_ref[...]
    n_rot = (Q_COLS + KV_COLS) // LANES
    for j in range(QKV_COLS // LANES):
        ch = qkv[:, j * LANES:(j + 1) * LANES]
        if j < n_rot:
            ch = ch * cos + pltpu.roll(ch, LANES - ROT_DIM // 2, 1) * sa + pltpu.roll(ch, ROT_DIM // 2, 1) * sb
        if j < Q_COLS // LANES:
            ch = ch * (HEAD_DIM ** -0.5)
        qkv_ref[:, j * LANES:(j + 1) * LANES] = ch.astype(BF16)

    z = jnp.dot(h, w_ref[:, QKV_COLS:], preferred_element_type=F32) + b_ref[:, QKV_COLS:]
    z_ref[...] = _gelu_exact(z)


def _proj(x, prev, mod_l, g1, w_in, b_in, cos_t, sa_t, sb_t, seq):
    t, d = x.shape
    tm = PROJ_TM
    tiles_per_seq = seq // tm
    row = lambda i: (i, 0)
    const = lambda i: (0, 0)
    modmap = lambda i: (i // tiles_per_seq, 0, 0)
    in_specs = [pl.BlockSpec((tm, d), row)]
    args = [x]
    if prev is not None:
        y2, wts, pmod = prev
        in_specs += [pl.BlockSpec((tm, d), row), pl.BlockSpec((tm, d), lambda i: (i + t // tm, 0)),
                     pl.BlockSpec((tm, LANES), row), pl.BlockSpec((1, N_MOD, d), modmap)]
        args += [y2, y2, wts, pmod]
    in_specs += [
        pl.BlockSpec((1, N_MOD, d), modmap),
        pl.BlockSpec((1, d), const),
        pl.BlockSpec((d, IN_COLS), const, pipeline_mode=pl.Buffered(1)),
        pl.BlockSpec((1, IN_COLS), const),
        pl.BlockSpec((tm, LANES), row),
        pl.BlockSpec((tm, LANES), row),
        pl.BlockSpec((tm, LANES), row),
    ]
    args += [mod_l, g1, w_in, b_in, cos_t, sa_t, sb_t]
    out_specs = [pl.BlockSpec((tm, QKV_COLS), row), pl.BlockSpec((tm, 2 * GMLP_WIDTH), row)]
    out_shape = [jax.ShapeDtypeStruct((t, QKV_COLS), BF16), jax.ShapeDtypeStruct((t, 2 * GMLP_WIDTH), F32)]
    if prev is not None:
        out_specs = [pl.BlockSpec((tm, d), row)] + out_specs
        out_shape = [jax.ShapeDtypeStruct((t, d), F32)] + out_shape
    outs = pl.pallas_call(
        functools.partial(_proj_body, prev is not None),
        grid=(t // tm,),
        in_specs=in_specs,
        out_specs=out_specs,
        out_shape=out_shape,
        compiler_params=_cparams(("arbitrary",)),
        name="proj",
    )(*args)
    if prev is None:
        return (x,) + tuple(outs)
    return tuple(outs)


def _mixer_body(sink_ref, q_ref, kvc_ref, kvp_ref, z_ref, ws_ref, bst_ref, lng_ref, lnb_ref, ga_ref, gg_ref,
                o_ref):
    n = pl.program_id(1)
    for sub in range(MIX_BLOCKS):
        r0 = sub * BLOCK
        kvp = kvp_ref[...] if sub == 0 else kvc_ref[r0 - BLOCK:r0, :]
        has_prev = (n > 0) if sub == 0 else None
        _mixer_block(sink_ref, q_ref[r0:r0 + BLOCK, :], kvc_ref[r0:r0 + BLOCK, :], kvp, has_prev, z_ref, r0,
                     ws_ref, bst_ref, lng_ref, lnb_ref, ga_ref, gg_ref, o_ref)


def _mixer_block(sink_ref, q, kvc, kvp, has_prev, z_ref, r0, ws_ref, bst_ref, lng_ref, lnb_ref, ga_ref, gg_ref,
                 o_ref):
    rows = slice(r0, r0 + BLOCK)
    kcat = jnp.concatenate([kvp[:, :KV_COLS], kvc[:, :KV_COLS]], axis=0)
    vcat = jnp.concatenate([kvp[:, KV_COLS:], kvc[:, KV_COLS:]], axis=0)

    lane = lax.broadcasted_iota(jnp.int32, (BLOCK, LANES), 1)
    lo = lane < HEAD_DIM
    kj = lax.broadcasted_iota(jnp.int32, (2 * BLOCK, BLOCK), 0)
    qi = lax.broadcasted_iota(jnp.int32, (2 * BLOCK, BLOCK), 1)
    rel = qi + BLOCK - kj
    valid = (rel >= 0) & (rel < WINDOW)
    if has_prev is not None:
        valid = valid & (has_prev | (kj >= BLOCK))
    row_lo = lax.broadcasted_iota(jnp.int32, (LANES, BLOCK), 0) < HEAD_DIM
    zero_q = jnp.zeros((BLOCK, LANES), BF16)
    nt_dims = (((1,), (1,)), ((), ()))

    attn = []
    chunks_per_kv = GQA_GROUP * HEAD_DIM // LANES
    for hk in range(N_KV_HEADS):
        ksl = kcat[:, hk * HEAD_DIM:(hk + 1) * HEAD_DIM]
        vsl = vcat[:, hk * HEAD_DIM:(hk + 1) * HEAD_DIM]
        kk = jnp.concatenate([ksl, ksl], axis=1)
        vt = jnp.concatenate([vsl, vsl], axis=1).astype(F32).T.astype(BF16)
        for j in range(chunks_per_kv):
            c = hk * chunks_per_kv + j
            qc = q[:, c * LANES:(c + 1) * LANES]
            probs = []
            for half in range(2):
                qm = jnp.where(lo, qc, zero_q) if half == 0 else jnp.where(lo, zero_q, qc)
                s = lax.dot_general(kk, qm, nt_dims, preferred_element_type=F32)
                s = jnp.where(valid, s, -jnp.inf)
                sink = sink_ref[2 * c + half]
                m = jnp.maximum(jnp.max(s, axis=0, keepdims=True), sink)
                p = jnp.exp(s - m)
                denom = jnp.sum(p, axis=0, keepdims=True) + jnp.exp(sink - m)
                probs.append((p * (1.0 / denom)).astype(BF16))
            pn = jnp.concatenate(probs, axis=1)
            ot = jnp.dot(vt, pn, preferred_element_type=F32)
            sel = jnp.where(row_lo, ot[:, :BLOCK], ot[:, BLOCK:])
            attn.append(sel.T)

    ss = attn[0] * attn[0]
    for a in attn[1:]:
        ss = ss + a * a
    scale = lax.rsqrt(jnp.sum(ss, axis=-1, keepdims=True) * (1.0 / ATTN_WIDTH) + NORM_EPS)
    for c, a in enumerate(attn):
        o_ref[rows, c * LANES:(c + 1) * LANES] = (a * scale * ga_ref[:, c * LANES:(c + 1) * LANES]).astype(BF16)

    u = z_ref[rows, :GMLP_WIDTH]
    v = z_ref[rows, GMLP_WIDTH:]
    mu = jnp.mean(v, axis=-1, keepdims=True)
    vc = v - mu
    var = jnp.mean(vc * vc, axis=-1, keepdims=True)
    vn = ((vc * lax.rsqrt(var + NORM_EPS)) * lng_ref[...] + lnb_ref[...]).astype(BF16)
    ti = lax.broadcasted_iota(jnp.int32, (BLOCK, BLOCK), 0)
    si = lax.broadcasted_iota(jnp.int32, (BLOCK, BLOCK), 1)
    tril = si <= ti
    bst = bst_ref[...]
    gm = []
    for h in range(GMLP_GROUPS):
        w = jnp.where(tril, ws_ref[h], 0.0).astype(BF16)
        sp = jnp.dot(w, vn[:, h * LANES:(h + 1) * LANES], preferred_element_type=F32) + bst[:, h:h + 1]
        gm.append(u[:, h * LANES:(h + 1) * LANES] * sp)
    ss = gm[0] * gm[0]
    for a in gm[1:]:
        ss = ss + a * a
    scale = lax.rsqrt(jnp.sum(ss, axis=-1, keepdims=True) * (1.0 / GMLP_WIDTH) + NORM_EPS)
    for c, a in enumerate(gm):
        o_ref[rows, ATTN_WIDTH + c * LANES:ATTN_WIDTH + (c + 1) * LANES] = (
            a * scale * gg_ref[:, c * LANES:(c + 1) * LANES]).astype(BF16)


def _mixer(qkv, z, sinks, w_sp, bst, ln_g, ln_b, g_attn, g_gmlp, batch, seq):
    t = qkv.shape[0]
    nb = seq // BLOCK
    steps = nb // MIX_BLOCKS
    rows = MIX_BLOCKS * BLOCK
    kv_blk = Q_COLS // (2 * KV_COLS)
    cur = lambda b, n, s: (b * steps + n, 0)
    const2 = lambda b, n, s: (0, 0)
    grid_spec = pltpu.PrefetchScalarGridSpec(
        num_scalar_prefetch=1,
        grid=(batch, steps),
        in_specs=[
            pl.BlockSpec((rows, Q_COLS), cur),
            pl.BlockSpec((rows, 2 * KV_COLS), lambda b, n, s: (b * steps + n, kv_blk)),
            pl.BlockSpec((BLOCK, 2 * KV_COLS),
                         lambda b, n, s: (b * nb + jnp.maximum(MIX_BLOCKS * n - 1, 0), kv_blk)),
            pl.BlockSpec((rows, 2 * GMLP_WIDTH), cur),
            pl.BlockSpec((GMLP_GROUPS, BLOCK, BLOCK), lambda b, n, s: (0, 0, 0)),
            pl.BlockSpec((BLOCK, GMLP_GROUPS), const2),
            pl.BlockSpec((1, GMLP_WIDTH), const2),
            pl.BlockSpec((1, GMLP_WIDTH), const2),
            pl.BlockSpec((1, ATTN_WIDTH), const2),
            pl.BlockSpec((1, GMLP_WIDTH), const2),
        ],
        out_specs=pl.BlockSpec((rows, D_MODEL), cur),
    )
    return pl.pallas_call(
        _mixer_body,
        grid_spec=grid_spec,
        out_shape=jax.ShapeDtypeStruct((t, D_MODEL), BF16),
        compiler_params=_cparams(("arbitrary", "arbitrary")),
        name="mixer",
    )(sinks, qkv, qkv, qkv, z, w_sp, bst, ln_g, ln_b, g_attn, g_gmlp)


def _first_index_of_max(vals, lane, width):
    mx = jnp.max(vals, axis=-1, keepdims=True)
    idx = jnp.min(jnp.where(vals == mx, lane, width), axis=-1, keepdims=True)
    return mx, idx


def _outproj_body(m_ref, x_ref, mod_ref, g_ref, w_ref, wr_ref, br_ref,
                  xo_ref, h_ref, ids_ref, wt_ref, cnt_ref, carry_ref):
    i = pl.program_id(0)
    tm = x_ref.shape[0]

    @pl.when(i == 0)
    def _():
        carry_ref[...] = jnp.zeros_like(carry_ref)

    mod = mod_ref[0]
    y = jnp.dot(m_ref[...], w_ref[...], preferred_element_type=F32)
    x = x_ref[...] + (1.0 + mod[2:3, :]) * y
    xo_ref[...] = x
    h = _rms_mod(x, g_ref[...], mod[4:5, :], mod[3:4, :])
    h_ref[...] = _pack_bf16_halves(h)

    logits = jnp.dot(h.astype(BF16), wr_ref[...], preferred_element_type=F32) + br_ref[...]
    lane = lax.broadcasted_iota(jnp.int32, (tm, LANES), 1)
    neg = -jnp.inf
    gl = jnp.where(lane < N_GROUPS, logits, neg)
    ge = jnp.exp(gl - jnp.max(gl, axis=-1, keepdims=True))
    gprob = ge / jnp.sum(ge, axis=-1, keepdims=True)
    g_p, g_idx = _first_index_of_max(gprob, lane, LANES)

    base = N_GROUPS + EXPERTS_PER_GROUP * g_idx
    in_grp = (lane >= base) & (lane < base + EXPERTS_PER_GROUP)
    el = jnp.where(in_grp, logits, neg)
    ee = jnp.exp(el - jnp.max(el, axis=-1, keepdims=True))
    eprob = jnp.where(in_grp, ee / jnp.sum(ee, axis=-1, keepdims=True), -1.0)
    p1, i1 = _first_index_of_max(eprob, lane, LANES)
    eprob2 = jnp.where(lane == i1, -1.0, eprob)
    p2, i2 = _first_index_of_max(eprob2, lane, LANES)
    psum = p1 + p2
    w1 = g_p * (p1 / psum)
    w2 = g_p * (p2 / psum)
    e1 = i1 - N_GROUPS
    e2 = i2 - N_GROUPS

    oh1 = lane == e1
    oh2 = lane == e2
    onehot = jnp.where(oh1 | oh2, 1.0, 0.0)
    ri = lax.broadcasted_iota(jnp.int32, (tm, tm), 0)
    ci = lax.broadcasted_iota(jnp.int32, (tm, tm), 1)
    lower = jnp.where(ci < ri, 1.0, 0.0).astype(BF16)
    before = jnp.dot(lower, onehot.astype(BF16), preferred_element_type=F32) + carry_ref[...]
    r1 = jnp.sum(jnp.where(oh1, before, 0.0), axis=-1, keepdims=True)
    r2 = jnp.sum(jnp.where(oh2, before, 0.0), axis=-1, keepdims=True)
    carry = carry_ref[...] + jnp.sum(onehot, axis=0, keepdims=True)
    carry_ref[...] = carry
    cnt_ref[...] = jnp.broadcast_to(carry, cnt_ref.shape).astype(jnp.int32)

    ids = jnp.where(lane == 0, e1, jnp.where(lane == 1, e2, 0))
    ranks = jnp.where(lane == 2, r1, jnp.where(lane == 3, r2, 0.0)).astype(jnp.int32)
    ids_ref[...] = ids + ranks
    wt_ref[...] = jnp.where(lane == 0, w1, jnp.where(lane == 1, w2, 0.0))


def _outproj(merged, x, mod_l, g2, w_out, w_router, b_router, seq):
    t, d = x.shape
    tm = OUT_TM
    tiles_per_seq = seq // tm
    row = lambda i: (i, 0)
    const = lambda i: (0, 0)
    return pl.pallas_call(
        _outproj_body,
        grid=(t // tm,),
        in_specs=[
            pl.BlockSpec((tm, d), row),
            pl.BlockSpec((tm, d), row),
            pl.BlockSpec((1, N_MOD, d), lambda i: (i // tiles_per_seq, 0, 0)),
            pl.BlockSpec((1, d), const),
            pl.BlockSpec((d, d), const, pipeline_mode=pl.Buffered(1)),
            pl.BlockSpec((d, LANES), const),
            pl.BlockSpec((1, LANES), const),
        ],
        out_specs=[
            pl.BlockSpec((tm, d), row),
            pl.BlockSpec((tm, d // 2), row),
            pl.BlockSpec((tm, LANES), row),
            pl.BlockSpec((tm, LANES), row),
            pl.BlockSpec((SUBLANES, LANES), const),
        ],
        out_shape=[
            jax.ShapeDtypeStruct((t, d), F32),
            jax.ShapeDtypeStruct((t, d // 2), jnp.uint32),
            jax.ShapeDtypeStruct((t, LANES), jnp.int32),
            jax.ShapeDtypeStruct((t, LANES), F32),
            jax.ShapeDtypeStruct((SUBLANES, LANES), jnp.int32),
        ],
        scratch_shapes=[pltpu.VMEM((1, LANES), F32)],
        compiler_params=_cparams(("arbitrary",)),
        name="outproj",
    )(merged, x, mod_l, g2, w_out, w_router, b_router)


def _moe_body(layer, n_slots, te_ref, nx_ref, nu_ref, h_hbm, src_hbm, dst_hbm, wg_hbm, wu_hbm, wd_hbm, y_hbm,
              first_smem, plan_smem, xbuf, obuf, wgs, wus, wds, wgb, wub, wdb,
              first_sem, plan_sem, g_sem, s_sem, w_sem):
    i = pl.program_id(0)
    n_used = nu_ref[0]
    nb, tm, d = obuf.shape
    de = wgb.shape[1]

    def weight_copies(e):
        return (pltpu.make_async_copy(wg_hbm.at[layer, e], wgs, w_sem.at[0]),
                pltpu.make_async_copy(wu_hbm.at[layer, e], wus, w_sem.at[1]),
                pltpu.make_async_copy(wd_hbm.at[layer, e], wds, w_sem.at[2]))
    b0 = i % nb
    b1 = (i + 1) % nb
    b2 = (i + 2) % nb
    p0 = i % 2
    p1 = 1 - p0

    class _PlanCopy:
        def __init__(self, k, slot):
            self.copies = (
                pltpu.make_async_copy(src_hbm.at[k + 2], plan_smem.at[slot, 0], plan_sem.at[slot, 0]),
                pltpu.make_async_copy(dst_hbm.at[k], plan_smem.at[slot, 1], plan_sem.at[slot, 1]))

        def start(self):
            for cp in self.copies:
                cp.start()

        def wait(self):
            for cp in self.copies:
                cp.wait()

    plan_copy = _PlanCopy

    def gather_row(tok, buf, j):
        return pltpu.make_async_copy(h_hbm.at[pl.ds(tok, 1), :], xbuf.at[buf, pl.ds(j, 1), :], g_sem.at[buf])

    def scatter_row(dst, buf, j):
        return pltpu.make_async_copy(obuf.at[buf, pl.ds(j, 1), :], y_hbm.at[pl.ds(dst, 1), :], s_sem.at[buf])

    def wait_gather(buf):
        pltpu.make_async_copy(h_hbm.at[pl.ds(0, tm), :], xbuf.at[buf], g_sem.at[buf]).wait()

    def wait_scatter(buf):
        pltpu.make_async_copy(obuf.at[buf], y_hbm.at[pl.ds(0, tm), :], s_sem.at[buf]).wait()

    @pl.when(i == 0)
    def _():
        first = pltpu.make_async_copy(src_hbm.at[pl.ds(0, 2)], first_smem, first_sem.at[0])
        first.start()
        plan_copy(0, 0).start()
        for cp in weight_copies(te_ref[0]):
            cp.start()
        first.wait()
        for tile in range(2):
            def body(j, carry, tile=tile):
                gather_row(first_smem[tile, j], tile, j).start()
                return carry
            lax.fori_loop(0, tm, body, 0, unroll=8)
        obuf[nb - 1] = jnp.zeros((tm, d), F32)

    @pl.when(i < n_used)
    def _():
        plan_copy(i, p0).wait()
        plan_copy(i + 1, p1).start()
        wait_gather(b0)

        @pl.when(i >= 2)
        def _():
            wait_scatter(b0)

        new_expert = jnp.logical_or(i == 0, te_ref[i] != te_ref[jnp.maximum(i - 1, 0)])

        @pl.when(new_expert)
        def _():
            for cp in weight_copies(te_ref[i]):
                cp.wait()
            wgb[...] = wgs[...].astype(BF16)
            wub[...] = wus[...].astype(BF16)
            wdb[...] = wds[...].astype(BF16)

            @pl.when(nx_ref[i] != te_ref[i])
            def _():
                for cp in weight_copies(nx_ref[i]):
                    cp.start()

        copies = []
        for j in range(tm):
            copies += [(0, j), (1, j)]
        n_up = de // MOE_N_CHUNK
        n_down = d // MOE_N_CHUNK
        n_pieces = 2 * n_up + n_down
        per_piece = -(-len(copies) // n_pieces)

        def issue(piece):
            for kind, j in copies[piece * per_piece:(piece + 1) * per_piece]:
                if kind == 0:
                    gather_row(plan_smem[p0, 0, j], b2, j).start()
                else:
                    scatter_row(plan_smem[p0, 1, j], b2, j).start()

        xb = _unpack_bf16_halves(xbuf[b0])
        piece = 0
        acts = []
        for n in range(n_up):
            cols = slice(n * MOE_N_CHUNK, (n + 1) * MOE_N_CHUNK)
            g = jnp.dot(xb, wgb[:, cols], preferred_element_type=F32)
            issue(piece)
            u = jnp.dot(xb, wub[:, cols], preferred_element_type=F32)
            issue(piece + 1)
            piece += 2
            acts.append(((g * jax.nn.sigmoid(g)) * u).astype(BF16))
        a = jnp.concatenate(acts, axis=1)
        for n in range(n_down):
            cols = slice(n * MOE_N_CHUNK, (n + 1) * MOE_N_CHUNK)
            obuf[b0, :, cols] = jnp.dot(a, wdb[:, cols], preferred_element_type=F32)
            issue(piece)
            piece += 1

    @pl.when(i == n_used)
    def _():
        plan_copy(i, p0).wait()
        wait_gather(b0)
        wait_gather(b1)
        wait_scatter(b1)

        @pl.when(i >= 2)
        def _():
            wait_scatter(b0)

        def body(j, carry):
            scatter_row(plan_smem[p0, 1, j], b2, j).start()
            return carry
        lax.fori_loop(0, tm, body, 0, unroll=8)
        wait_scatter(b2)
        obuf[b0] = jnp.zeros((tm, d), F32)
        for k in range(nb):
            fill = pltpu.make_async_copy(obuf.at[b0], y_hbm.at[pl.ds(n_slots + k * tm, tm), :], s_sem.at[b0])
            fill.start()
            fill.wait()


def _moe(layer, h2, src2d, dst2d, tile_expert, next_expert, n_used, w_gate, w_up, w_down):
    t = h2.shape[0]
    n_slots = 2 * t
    steps, tm = dst2d.shape
    d, de = w_gate.shape[-2:]
    hbm = pl.BlockSpec(memory_space=pl.ANY)
    grid_spec = pltpu.PrefetchScalarGridSpec(
        num_scalar_prefetch=3,
        grid=(steps,),
        in_specs=[hbm] * 6,
        out_specs=hbm,
        scratch_shapes=[
            pltpu.SMEM((2, tm), jnp.int32),
            pltpu.SMEM((2, 2, tm), jnp.int32),
            pltpu.VMEM((MOE_BUFS, tm, d // 2), jnp.uint32),
            pltpu.VMEM((MOE_BUFS, tm, d), F32),
            pltpu.VMEM((d, de), F32),
            pltpu.VMEM((d, de), F32),
            pltpu.VMEM((de, d), F32),
            pltpu.VMEM((d, de), BF16),
            pltpu.VMEM((d, de), BF16),
            pltpu.VMEM((de, d), BF16),
            pltpu.SemaphoreType.DMA((1,)),
            pltpu.SemaphoreType.DMA((2, 2)),
            pltpu.SemaphoreType.DMA((MOE_BUFS,)),
            pltpu.SemaphoreType.DMA((MOE_BUFS,)),
            pltpu.SemaphoreType.DMA((3,)),
        ],
    )
    return pl.pallas_call(
        functools.partial(_moe_body, layer, n_slots),
        grid_spec=grid_spec,
        out_shape=jax.ShapeDtypeStruct((n_slots + MOE_BUFS * tm, d), F32),
        compiler_params=_cparams(("arbitrary",)),
        name="moe",
    )(tile_expert, next_expert, n_used, h2, src2d, dst2d, w_gate, w_up, w_down)


def _route_plan(ids, counts, t):
    tm = MOE_TM
    nt = (2 * t) // tm + N_EXPERTS
    cnt = counts[0, :N_EXPERTS]
    tiles = (cnt + tm - 1) // tm
    tile_end = jnp.cumsum(tiles)
    off = ((tile_end - tiles) * tm).astype(jnp.int32)
    tile_buf = np.arange(-1, nt) % MOE_BUFS
    dump = (2 * t + tile_buf[:, None] * tm + np.arange(tm)[None, :]).astype(np.int32)
    dst2d = _plan_scatter(off, ids[:, 0:4].reshape(-1), jnp.asarray(dump), t)
    src_tok = jnp.where(dst2d[1:] < 2 * t, dst2d[1:] % t, 0)
    src2d = jnp.concatenate([src_tok, jnp.zeros((3, tm), jnp.int32)], axis=0)
    n_used = tile_end[-1]
    steps = jnp.minimum(jnp.arange(nt + 1, dtype=jnp.int32), n_used - 1)
    tile_expert = jnp.sum(tile_end[None, :] <= steps[:, None], axis=1).astype(jnp.int32)
    eid = jnp.arange(N_EXPERTS, dtype=jnp.int32)
    later_used = (tiles[None, :] > 0) & (eid[None, :] > eid[:, None])
    nxt = jnp.min(jnp.where(later_used, eid[None, :], N_EXPERTS), axis=1)
    next_expert = jnp.where(nxt == N_EXPERTS, eid, nxt)[tile_expert].astype(jnp.int32)
    return src2d, dst2d, tile_expert, next_expert, n_used.reshape(1).astype(jnp.int32)


PLAN_CHUNK = 8192


def _plan_body(t, off_ref, ids_hbm, init_hbm, out_hbm, tab, ids_a, ids_b, init_sem, ids_sem, out_sem):
    chunk = ids_a.shape[0]
    n_chunks = (4 * t) // chunk
    bufs = (ids_a, ids_b)

    def ids_copy(c):
        return pltpu.make_async_copy(ids_hbm.at[pl.ds(c * chunk, chunk)], bufs[c % 2], ids_sem.at[c % 2])

    init = pltpu.make_async_copy(init_hbm, tab, init_sem.at[0])
    init.start()
    ids_copy(0).start()
    init.wait()
    for c in range(n_chunks):
        ids_copy(c).wait()
        if c + 1 < n_chunks:
            ids_copy(c + 1).start()
        buf = bufs[c % 2]
        tok0 = c * (chunk // 4)

        def body(tt, carry, buf=buf, tok0=tok0):
            b = 4 * tt
            tok = tok0 + tt
            tab[off_ref[buf[b]] + buf[b + 2]] = tok
            tab[off_ref[buf[b + 1]] + buf[b + 3]] = tok + t
            return carry
        lax.fori_loop(0, chunk // 4, body, 0, unroll=8)
    out = pltpu.make_async_copy(tab, out_hbm, out_sem.at[0])
    out.start()
    out.wait()


def _plan_scatter(off, ids_flat, dump, t):
    rows, tm = dump.shape
    chunk = min(PLAN_CHUNK, 4 * t)
    assert (4 * t) % chunk == 0
    words = -(-rows * tm // 1024) * 1024
    init = jnp.concatenate([dump.reshape(-1), jnp.zeros((words - rows * tm,), jnp.int32)])
    hbm = pl.BlockSpec(memory_space=pl.ANY)
    grid_spec = pltpu.PrefetchScalarGridSpec(
        num_scalar_prefetch=1,
        grid=(1,),
        in_specs=[hbm, hbm],
        out_specs=hbm,
        scratch_shapes=[
            pltpu.SMEM((words,), jnp.int32),
            pltpu.SMEM((chunk,), jnp.int32),
            pltpu.SMEM((chunk,), jnp.int32),
            pltpu.SemaphoreType.DMA((1,)),
            pltpu.SemaphoreType.DMA((2,)),
            pltpu.SemaphoreType.DMA((1,)),
        ],
    )
    flat = pl.pallas_call(
        functools.partial(_plan_body, t),
        grid_spec=grid_spec,
        out_shape=jax.ShapeDtypeStruct((words,), jnp.int32),
        compiler_params=_cparams(("arbitrary",)),
        name="plan",
    )(off + tm, ids_flat, init)
    return flat[:rows * tm].reshape(rows, tm)


def _final_body(x_ref, y0_ref, y1_ref, wt_ref, pmod_ref, g_ref, o_ref):
    x = _combine(x_ref[...], y0_ref, y1_ref, wt_ref, pmod_ref[0, 5:6, :])
    ms = jnp.mean(x * x, axis=-1, keepdims=True)
    o_ref[...] = x * lax.rsqrt(ms + NORM_EPS) * g_ref[...]


def _final(x, y2, wts, pmod, g_final, seq):
    t, d = x.shape
    tm = FINAL_TM
    tiles_per_seq = seq // tm
    row = lambda i: (i, 0)
    return pl.pallas_call(
        _final_body,
        grid=(t // tm,),
        in_specs=[
            pl.BlockSpec((tm, d), row),
            pl.BlockSpec((tm, d), row),
            pl.BlockSpec((tm, d), lambda i: (i + t // tm, 0)),
            pl.BlockSpec((tm, LANES), row),
            pl.BlockSpec((1, N_MOD, d), lambda i: (i // tiles_per_seq, 0, 0)),
            pl.BlockSpec((1, d), lambda i: (0, 0)),
        ],
        out_specs=pl.BlockSpec((tm, d), row),
        out_shape=jax.ShapeDtypeStruct((t, d), F32),
        compiler_params=_cparams(("arbitrary",)),
        name="final",
    )(x, y2, y2, wts, pmod, g_final)


def _rotary_tables(positions):
    half = ROT_DIM // 2
    inv_freq = ROPE_THETA ** (-(jnp.arange(0, ROT_DIM, 2, dtype=F32) / ROT_DIM))
    ang = positions.reshape(-1).astype(F32)[:, None] * inv_freq
    cos, sin = jnp.cos(ang), jnp.sin(ang)
    t = ang.shape[0]
    pad = HEAD_DIM - ROT_DIM
    cos_h = jnp.concatenate([cos, cos, jnp.ones((t, pad), F32)], axis=1)
    sa_h = jnp.concatenate([-sin, jnp.zeros((t, HEAD_DIM - half), F32)], axis=1)
    sb_h = jnp.concatenate([jnp.zeros((t, half), F32), sin, jnp.zeros((t, pad), F32)], axis=1)
    rep = LANES // HEAD_DIM
    return jnp.tile(cos_h, (1, rep)), jnp.tile(sa_h, (1, rep)), jnp.tile(sb_h, (1, rep))


def kernel(x, c, positions, w_ada, b_ada, g_norm1, w_in, b_in, sinks, ln_v_g, ln_v_b, w_spatial, b_spatial,
           g_attn_out, g_gmlp_out, w_out, g_norm2, w_router_group, b_router_group, w_router_expert,
           b_router_expert, w_e_gate, w_e_up, w_e_down, g_final):
    batch, seq, d = x.shape
    depth = w_ada.shape[0]
    t = batch * seq
    assert d == D_MODEL and seq % max(PROJ_TM, OUT_TM, FINAL_TM, MIX_BLOCKS * BLOCK) == 0 and (2 * t) % MOE_TM == 0

    c_pad = jnp.zeros((SUBLANES, d), F32).at[:batch].set(c)
    mod = _adaln(c_pad, w_ada, b_ada).reshape(depth, SUBLANES, N_MOD, d)
    cos_t, sa_t, sb_t = _rotary_tables(positions)

    n_route = N_GROUPS + N_EXPERTS
    w_router = jnp.concatenate(
        [w_router_group, w_router_expert, jnp.zeros((depth, d, LANES - n_route), F32)], axis=-1)
    b_router = jnp.concatenate(
        [b_router_group, b_router_expert, jnp.zeros((depth, LANES - n_route), F32)], axis=-1)

    xt = x.reshape(t, d)
    prev = None
    for l in range(depth):
        xt, qkv, z = _proj(xt, prev, mod[l], g_norm1[l][None], w_in[l].astype(BF16), b_in[l][None],
                           cos_t, sa_t, sb_t, seq)
        merged = _mixer(qkv, z, sinks[l], w_spatial[l], b_spatial[l].T, ln_v_g[l][None], ln_v_b[l][None],
                        g_attn_out[l][None], g_gmlp_out[l][None], batch, seq)
        xt, h2, ids, wts, counts = _outproj(merged, xt, mod[l], g_norm2[l][None], w_out[l].astype(BF16),
                                            w_router[l].astype(BF16), b_router[l][None], seq)
        src2d, dst2d, tile_expert, next_expert, n_used = _route_plan(ids, counts, t)
        y2 = _moe(l, h2, src2d, dst2d, tile_expert, next_expert, n_used, w_e_gate, w_e_up, w_e_down)
        prev = (y2, wts, mod[l])
    out = _final(xt, prev[0], prev[1], prev[2], g_final[None], seq)
    return out.reshape(batch, seq, d)
```

```python
import functools

import jax
import jax.numpy as jnp
import numpy as np
from jax import lax
from jax.experimental import pallas as pl
from jax.experimental.pallas import tpu as pltpu

F32 = jnp.float32
BF16 = jnp.bfloat16

D_MODEL = 2048
ATTN_WIDTH = D_MODEL // 2
HEAD_DIM = 64
N_Q_HEADS = ATTN_WIDTH // HEAD_DIM
N_KV_HEADS = 2
GQA_GROUP = N_Q_HEADS // N_KV_HEADS
WINDOW = 128
BLOCK = 128
ROT_DIM = HEAD_DIM // 4
ROPE_THETA = 500000.0
GMLP_WIDTH = D_MODEL - ATTN_WIDTH
GMLP_GROUPS = 8
GMLP_GROUP_DIM = GMLP_WIDTH // GMLP_GROUPS
Q_COLS = N_Q_HEADS * HEAD_DIM
KV_COLS = N_KV_HEADS * HEAD_DIM
QKV_COLS = Q_COLS + 2 * KV_COLS
IN_COLS = QKV_COLS + 2 * GMLP_WIDTH
N_GROUPS = 4
EXPERTS_PER_GROUP = 8
N_EXPERTS = N_GROUPS * EXPERTS_PER_GROUP
D_EXPERT = D_MODEL // 4
NORM_EPS = 1e-6
N_MOD = 6

LANES = 128
SUBLANES = 8
VMEM_LIMIT_BYTES = 56 * 1024 * 1024
TOKEN_TILE_ROWS = D_MODEL // 2 // LANES

ADA_TN = 1024
PROJ_TM = 256
OUT_TM = 512
MIX_BLOCKS = 2
MOE_TM = 256
MOE_N_CHUNK = 256
MOE_BUFS = 3
FINAL_TM = 512


def _cparams(sem):
    return pltpu.CompilerParams(dimension_semantics=sem, vmem_limit_bytes=VMEM_LIMIT_BYTES)


def _adaln_body(c_ref, w_ref, b_ref, o_ref):
    c = c_ref[...]
    ca = (c * jax.nn.sigmoid(c)).astype(BF16)
    w = w_ref[0].astype(BF16)
    o_ref[0] = jnp.dot(ca, w, preferred_element_type=F32) + b_ref[0]


def _adaln(c_pad, w_ada, b_ada):
    depth, d, n = w_ada.shape
    rows = c_pad.shape[0]
    return pl.pallas_call(
        _adaln_body,
        grid=(depth, n // ADA_TN),
        in_specs=[
            pl.BlockSpec((rows, d), lambda l, j: (0, 0)),
            pl.BlockSpec((1, d, ADA_TN), lambda l, j: (l, 0, j)),
            pl.BlockSpec((1, 1, ADA_TN), lambda l, j: (l, 0, j)),
        ],
        out_specs=pl.BlockSpec((1, rows, ADA_TN), lambda l, j: (l, 0, j)),
        out_shape=jax.ShapeDtypeStruct((depth, rows, n), F32),
        compiler_params=_cparams(("arbitrary", "arbitrary")),
        name="adaln",
    )(c_pad, w_ada, b_ada.reshape(depth, 1, n))


def _rms_mod(x, g, sc, sh):
    ms = jnp.mean(x * x, axis=-1, keepdims=True)
    return (x * lax.rsqrt(ms + NORM_EPS) * g) * (1.0 + sc) + sh


def _pack_bf16_halves(h):
    half = h.shape[1] // 2
    lo = lax.bitcast_convert_type(h[:, :half].astype(BF16).astype(F32), jnp.uint32)
    hi = lax.bitcast_convert_type(h[:, half:].astype(BF16).astype(F32), jnp.uint32)
    return lax.shift_right_logical(lo, jnp.uint32(16)) | (hi & jnp.uint32(0xFFFF0000))


def _unpack_bf16_halves(p):
    lo = lax.bitcast_convert_type(lax.shift_left(p, jnp.uint32(16)), F32).astype(BF16)
    hi = lax.bitcast_convert_type(p & jnp.uint32(0xFFFF0000), F32).astype(BF16)
    return jnp.concatenate([lo, hi], axis=1)


def _combine(x, y0_ref, y1_ref, wt_ref, gate):
    wt = wt_ref[...]
    y = wt[:, 0:1] * y0_ref[...] + wt[:, 1:2] * y1_ref[...]
    return x + (1.0 + gate) * y


def _gelu_exact(z):
    return 0.5 * z * (1.0 + lax.erf(z * (2.0 ** -0.5)))


def _proj_body(has_prev, *refs):
    if has_prev:
        (x_ref, y0_ref, y1_ref, wt_ref, pmod_ref, mod_ref, g_ref, w_ref, b_ref, cos_ref, sa_ref, sb_ref,
         xo_ref, qkv_ref, z_ref) = refs
        x = _combine(x_ref[...], y0_ref, y1_ref, wt_ref, pmod_ref[0, 5:6, :])
        xo_ref[...] = x
    else:
        x_ref, mod_ref, g_ref, w_ref, b_ref, cos_ref, sa_ref, sb_ref, qkv_ref, z_ref = refs
        x = x_ref[...]
    mod = mod_ref[0]
    h = _rms_mod(x, g_ref[...], mod[1:2, :], mod[0:1, :]).astype(BF16)

    qkv = jnp.dot(h, w_ref[:, :QKV_COLS], preferred_element_type=F32) + b_ref[:, :QKV_COLS]
    cos, sa, sb = cos_ref[...], sa_ref[...], sb_ref[...]
    n_rot = (Q_COLS + KV_COLS) // LANES
    for j in range(QKV_COLS // LANES):
        ch = qkv[:, j * LANES:(j + 1) * LANES]
        if j < n_rot:
            ch = ch * cos + pltpu.roll(ch, LANES - ROT_DIM // 2, 1) * sa + pltpu.roll(ch, ROT_DIM // 2, 1) * sb
        if j < Q_COLS // LANES:
            ch = ch * (HEAD_DIM ** -0.5)
        qkv_ref[:, j * LANES:(j + 1) * LANES] = ch.astype(BF16)

    z = jnp.dot(h, w_ref[:, QKV_COLS:], preferred_element_type=F32) + b_ref[:, QKV_COLS:]
    z_ref[...] = _gelu_exact(z)


def _proj(x, prev, mod_l, g1, w_in, b_in, cos_t, sa_t, sb_t, seq):
    t, d = x.shape
    tm = PROJ_TM
    tiles_per_seq = seq // tm
    row = lambda i: (i, 0)
    const = lambda i: (0, 0)
    modmap = lambda i: (i // tiles_per_seq, 0, 0)
    in_specs = [pl.BlockSpec((tm, d), row)]
    args = [x]
    if prev is not None:
        y2, wts, pmod = prev
        in_specs += [pl.BlockSpec((tm, d), row), pl.BlockSpec((tm, d), lambda i: (i + t // tm, 0)),
                     pl.BlockSpec((tm, LANES), row), pl.BlockSpec((1, N_MOD, d), modmap)]
        args += [y2, y2, wts, pmod]
    in_specs += [
        pl.BlockSpec((1, N_MOD, d), modmap),
        pl.BlockSpec((1, d), const),
        pl.BlockSpec((d, IN_COLS), const, pipeline_mode=pl.Buffered(1)),
        pl.BlockSpec((1, IN_COLS), const),
        pl.BlockSpec((tm, LANES), row),
        pl.BlockSpec((tm, LANES), row),
        pl.BlockSpec((tm, LANES), row),
    ]
    args += [mod_l, g1, w_in, b_in, cos_t, sa_t, sb_t]
    out_specs = [pl.BlockSpec((tm, QKV_COLS), row), pl.BlockSpec((tm, 2 * GMLP_WIDTH), row)]
    out_shape = [jax.ShapeDtypeStruct((t, QKV_COLS), BF16), jax.ShapeDtypeStruct((t, 2 * GMLP_WIDTH), F32)]
    if prev is not None:
        out_specs = [pl.BlockSpec((tm, d), row)] + out_specs
        out_shape = [jax.ShapeDtypeStruct((t, d), F32)] + out_shape
    outs = pl.pallas_call(
        functools.partial(_proj_body, prev is not None),
        grid=(t // tm,),
        in_specs=in_specs,
        out_specs=out_specs,
        out_shape=out_shape,
        compiler_params=_cparams(("arbitrary",)),
        name="proj",
    )(*args)
    if prev is None:
        return (x,) + tuple(outs)
    return tuple(outs)


def _mixer_body(sink_ref, q_ref, kvc_ref, kvp_ref, z_ref, ws_ref, bst_ref, lng_ref, lnb_ref, ga_ref, gg_ref,
                o_ref):
    n = pl.program_id(1)
    for sub in range(MIX_BLOCKS):
        r0 = sub * BLOCK
        kvp = kvp_ref[...] if sub == 0 else kvc_ref[r0 - BLOCK:r0, :]
        has_prev = (n > 0) if sub == 0 else None
        _mixer_block(sink_ref, q_ref[r0:r0 + BLOCK, :], kvc_ref[r0:r0 + BLOCK, :], kvp, has_prev, z_ref, r0,
                     ws_ref, bst_ref, lng_ref, lnb_ref, ga_ref, gg_ref, o_ref)


def _mixer_block(sink_ref, q, kvc, kvp, has_prev, z_ref, r0, ws_ref, bst_ref, lng_ref, lnb_ref, ga_ref, gg_ref,
                 o_ref):
    rows = slice(r0, r0 + BLOCK)
    kcat = jnp.concatenate([kvp[:, :KV_COLS], kvc[:, :KV_COLS]], axis=0)
    vcat = jnp.concatenate([kvp[:, KV_COLS:], kvc[:, KV_COLS:]], axis=0)

    lane = lax.broadcasted_iota(jnp.int32, (BLOCK, LANES), 1)
    lo = lane < HEAD_DIM
    kj = lax.broadcasted_iota(jnp.int32, (2 * BLOCK, BLOCK), 0)
    qi = lax.broadcasted_iota(jnp.int32, (2 * BLOCK, BLOCK), 1)
    rel = qi + BLOCK - kj
    valid = (rel >= 0) & (rel < WINDOW)
    if has_prev is not None:
        valid = valid & (has_prev | (kj >= BLOCK))
    row_lo = lax.broadcasted_iota(jnp.int32, (LANES, BLOCK), 0) < HEAD_DIM
    zero_q = jnp.zeros((BLOCK, LANES), BF16)
    nt_dims = (((1,), (1,)), ((), ()))

    attn = []
    chunks_per_kv = GQA_GROUP * HEAD_DIM // LANES
    for hk in range(N_KV_HEADS):
        ksl = kcat[:, hk * HEAD_DIM:(hk + 1) * HEAD_DIM]
        vsl = vcat[:, hk * HEAD_DIM:(hk + 1) * HEAD_DIM]
        kk = jnp.concatenate([ksl, ksl], axis=1)
        vt = jnp.concatenate([vsl, vsl], axis=1).astype(F32).T.astype(BF16)
        for j in range(chunks_per_kv):
            c = hk * chunks_per_kv + j
            qc = q[:, c * LANES:(c + 1) * LANES]
            probs = []
            for half in range(2):
                qm = jnp.where(lo, qc, zero_q) if half == 0 else jnp.where(lo, zero_q, qc)
                s = lax.dot_general(kk, qm, nt_dims, preferred_element_type=F32)
                s = jnp.where(valid, s, -jnp.inf)
                sink = sink_ref[2 * c + half]
                m = jnp.maximum(jnp.max(s, axis=0, keepdims=True), sink)
                p = jnp.exp(s - m)
                denom = jnp.sum(p, axis=0, keepdims=True) + jnp.exp(sink - m)
                probs.append((p * (1.0 / denom)).astype(BF16))
            pn = jnp.concatenate(probs, axis=1)
            ot = jnp.dot(vt, pn, preferred_element_type=F32)
            sel = jnp.where(row_lo, ot[:, :BLOCK], ot[:, BLOCK:])
            attn.append(sel.T)

    ss = attn[0] * attn[0]
    for a in attn[1:]:
        ss = ss + a * a
    scale = lax.rsqrt(jnp.sum(ss, axis=-1, keepdims=True) * (1.0 / ATTN_WIDTH) + NORM_EPS)
    for c, a in enumerate(attn):
        o_ref[rows, c * LANES:(c + 1) * LANES] = (a * scale * ga_ref[:, c * LANES:(c + 1) * LANES]).astype(BF16)

    u = z_ref[rows, :GMLP_WIDTH]
    v = z_ref[rows, GMLP_WIDTH:]
    mu = jnp.mean(v, axis=-1, keepdims=True)
    vc = v - mu
    var = jnp.mean(vc * vc, axis=-1, keepdims=True)
    vn = ((vc * lax.rsqrt(var + NORM_EPS)) * lng_ref[...] + lnb_ref[...]).astype(BF16)
    ti = lax.broadcasted_iota(jnp.int32, (BLOCK, BLOCK), 0)
    si = lax.broadcasted_iota(jnp.int32, (BLOCK, BLOCK), 1)
    tril = si <= ti
    bst = bst_ref[...]
    gm = []
    for h in range(GMLP_GROUPS):
        w = jnp.where(tril, ws_ref[h], 0.0).astype(BF16)
        sp = jnp.dot(w, vn[:, h * LANES:(h + 1) * LANES], preferred_element_type=F32) + bst[:, h:h + 1]
        gm.append(u[:, h * LANES:(h + 1) * LANES] * sp)
    ss = gm[0] * gm[0]
    for a in gm[1:]:
        ss = ss + a * a
    scale = lax.rsqrt(jnp.sum(ss, axis=-1, keepdims=True) * (1.0 / GMLP_WIDTH) + NORM_EPS)
    for c, a in enumerate(gm):
        o_ref[rows, ATTN_WIDTH + c * LANES:ATTN_WIDTH + (c + 1) * LANES] = (
            a * scale * gg_ref[:, c * LANES:(c + 1) * LANES]).astype(BF16)


def _mixer(qkv, z, sinks, w_sp, bst, ln_g, ln_b, g_attn, g_gmlp, batch, seq):
    t = qkv.shape[0]
    nb = seq // BLOCK
    steps = nb // MIX_BLOCKS
    rows = MIX_BLOCKS * BLOCK
    kv_blk = Q_COLS // (2 * KV_COLS)
    cur = lambda b, n, s: (b * steps + n, 0)
    const2 = lambda b, n, s: (0, 0)
    grid_spec = pltpu.PrefetchScalarGridSpec(
        num_scalar_prefetch=1,
        grid=(batch, steps),
        in_specs=[
            pl.BlockSpec((rows, Q_COLS), cur),
            pl.BlockSpec((rows, 2 * KV_COLS), lambda b, n, s: (b * steps + n, kv_blk)),
            pl.BlockSpec((BLOCK, 2 * KV_COLS),
                         lambda b, n, s: (b * nb + jnp.maximum(MIX_BLOCKS * n - 1, 0), kv_blk)),
            pl.BlockSpec((rows, 2 * GMLP_WIDTH), cur),
            pl.BlockSpec((GMLP_GROUPS, BLOCK, BLOCK), lambda b, n, s: (0, 0, 0)),
            pl.BlockSpec((BLOCK, GMLP_GROUPS), const2),
            pl.BlockSpec((1, GMLP_WIDTH), const2),
            pl.BlockSpec((1, GMLP_WIDTH), const2),
            pl.BlockSpec((1, ATTN_WIDTH), const2),
            pl.BlockSpec((1, GMLP_WIDTH), const2),
        ],
        out_specs=pl.BlockSpec((rows, D_MODEL), cur),
    )
    return pl.pallas_call(
        _mixer_body,
        grid_spec=grid_spec,
        out_shape=jax.ShapeDtypeStruct((t, D_MODEL), BF16),
        compiler_params=_cparams(("arbitrary", "arbitrary")),
        name="mixer",
    )(sinks, qkv, qkv, qkv, z, w_sp, bst, ln_g, ln_b, g_attn, g_gmlp)


def _first_index_of_max(vals, lane, width):
    mx = jnp.max(vals, axis=-1, keepdims=True)
    idx = jnp.min(jnp.where(vals == mx, lane, width), axis=-1, keepdims=True)
    return mx, idx


def _outproj_body(m_ref, x_ref, mod_ref, g_ref, w_ref, wr_ref, br_ref,
                  xo_ref, h_ref, ids_ref, wt_ref, cnt_ref, carry_ref):
    i = pl.program_id(0)
    tm = x_ref.shape[0]

    @pl.when(i == 0)
    def _():
        carry_ref[...] = jnp.zeros_like(carry_ref)

    mod = mod_ref[0]
    y = jnp.dot(m_ref[...], w_ref[...], preferred_element_type=F32)
    x = x_ref[...] + (1.0 + mod[2:3, :]) * y
    xo_ref[...] = x
    h = _rms_mod(x, g_ref[...], mod[4:5, :], mod[3:4, :])
    packed = _pack_bf16_halves(h)
    for c in range(TOKEN_TILE_ROWS):
        h_ref[pl.ds(c, tm, stride=TOKEN_TILE_ROWS), :] = packed[:, c * LANES:(c + 1) * LANES]

    logits = jnp.dot(h.astype(BF16), wr_ref[...], preferred_element_type=F32) + br_ref[...]
    lane = lax.broadcasted_iota(jnp.int32, (tm, LANES), 1)
    neg = -jnp.inf
    gl = jnp.where(lane < N_GROUPS, logits, neg)
    ge = jnp.exp(gl - jnp.max(gl, axis=-1, keepdims=True))
    gprob = ge / jnp.sum(ge, axis=-1, keepdims=True)
    g_p, g_idx = _first_index_of_max(gprob, lane, LANES)

    base = N_GROUPS + EXPERTS_PER_GROUP * g_idx
    in_grp = (lane >= base) & (lane < base + EXPERTS_PER_GROUP)
    el = jnp.where(in_grp, logits, neg)
    ee = jnp.exp(el - jnp.max(el, axis=-1, keepdims=True))
    eprob = jnp.where(in_grp, ee / jnp.sum(ee, axis=-1, keepdims=True), -1.0)
    p1, i1 = _first_index_of_max(eprob, lane, LANES)
    eprob2 = jnp.where(lane == i1, -1.0, eprob)
    p2, i2 = _first_index_of_max(eprob2, lane, LANES)
    psum = p1 + p2
    w1 = g_p * (p1 / psum)
    w2 = g_p * (p2 / psum)
    e1 = i1 - N_GROUPS
    e2 = i2 - N_GROUPS

    oh1 = lane == e1
    oh2 = lane == e2
    onehot = jnp.where(oh1 | oh2, 1.0, 0.0)
    ri = lax.broadcasted_iota(jnp.int32, (tm, tm), 0)
    ci = lax.broadcasted_iota(jnp.int32, (tm, tm), 1)
    lower = jnp.where(ci < ri, 1.0, 0.0).astype(BF16)
    before = jnp.dot(lower, onehot.astype(BF16), preferred_element_type=F32) + carry_ref[...]
    r1 = jnp.sum(jnp.where(oh1, before, 0.0), axis=-1, keepdims=True)
    r2 = jnp.sum(jnp.where(oh2, before, 0.0), axis=-1, keepdims=True)
    carry = carry_ref[...] + jnp.sum(onehot, axis=0, keepdims=True)
    carry_ref[...] = carry
    cnt_ref[...] = jnp.broadcast_to(carry, cnt_ref.shape).astype(jnp.int32)

    ids = jnp.where(lane == 0, e1, jnp.where(lane == 1, e2, 0))
    ranks = jnp.where(lane == 2, r1, jnp.where(lane == 3, r2, 0.0)).astype(jnp.int32)
    ids_ref[...] = ids + ranks
    wt_ref[...] = jnp.where(lane == 0, w1, jnp.where(lane == 1, w2, 0.0))


def _outproj(merged, x, mod_l, g2, w_out, w_router, b_router, seq):
    t, d = x.shape
    tm = OUT_TM
    tiles_per_seq = seq // tm
    row = lambda i: (i, 0)
    const = lambda i: (0, 0)
    return pl.pallas_call(
        _outproj_body,
        grid=(t // tm,),
        in_specs=[
            pl.BlockSpec((tm, d), row),
            pl.BlockSpec((tm, d), row),
            pl.BlockSpec((1, N_MOD, d), lambda i: (i // tiles_per_seq, 0, 0)),
            pl.BlockSpec((1, d), const),
            pl.BlockSpec((d, d), const, pipeline_mode=pl.Buffered(1)),
            pl.BlockSpec((d, LANES), const),
            pl.BlockSpec((1, LANES), const),
        ],
        out_specs=[
            pl.BlockSpec((tm, d), row),
            pl.BlockSpec((tm * TOKEN_TILE_ROWS, LANES), row),
            pl.BlockSpec((tm, LANES), row),
            pl.BlockSpec((tm, LANES), row),
            pl.BlockSpec((SUBLANES, LANES), const),
        ],
        out_shape=[
            jax.ShapeDtypeStruct((t, d), F32),
            jax.ShapeDtypeStruct((t * TOKEN_TILE_ROWS, LANES), jnp.uint32),
            jax.ShapeDtypeStruct((t, LANES), jnp.int32),
            jax.ShapeDtypeStruct((t, LANES), F32),
            jax.ShapeDtypeStruct((SUBLANES, LANES), jnp.int32),
        ],
        scratch_shapes=[pltpu.VMEM((1, LANES), F32)],
        compiler_params=_cparams(("arbitrary",)),
        name="outproj",
    )(merged, x, mod_l, g2, w_out, w_router, b_router)


def _moe_body(layer, n_slots, te_ref, nx_ref, nu_ref, h_hbm, src_hbm, dst_hbm, wg_hbm, wu_hbm, wd_hbm, y_hbm,
              first_smem, plan_smem, xbuf, obuf, wgs, wus, wds, wgb, wub, wdb,
              first_sem, plan_sem, g_sem, s_sem, w_sem):
    i = pl.program_id(0)
    n_used = nu_ref[0]
    nb, tm, d = obuf.shape
    de = wgb.shape[1]

    def weight_copies(e):
        return (pltpu.make_async_copy(wg_hbm.at[layer, e], wgs, w_sem.at[0]),
                pltpu.make_async_copy(wu_hbm.at[layer, e], wus, w_sem.at[1]),
                pltpu.make_async_copy(wd_hbm.at[layer, e], wds, w_sem.at[2]))
    b0 = i % nb
    b1 = (i + 1) % nb
    b2 = (i + 2) % nb
    p0 = i % 2
    p1 = 1 - p0

    class _PlanCopy:
        def __init__(self, k, slot):
            self.copies = (
                pltpu.make_async_copy(src_hbm.at[k + 2], plan_smem.at[slot, 0], plan_sem.at[slot, 0]),
                pltpu.make_async_copy(dst_hbm.at[k], plan_smem.at[slot, 1], plan_sem.at[slot, 1]))

        def start(self):
            for cp in self.copies:
                cp.start()

        def wait(self):
            for cp in self.copies:
                cp.wait()

    plan_copy = _PlanCopy

    xrows = tm * TOKEN_TILE_ROWS

    def gather_row(src, buf, j):
        return pltpu.make_async_copy(
            h_hbm.at[pl.ds(pl.multiple_of(src, TOKEN_TILE_ROWS), TOKEN_TILE_ROWS), :],
            xbuf.at[pl.ds(pl.multiple_of(buf * xrows + j * TOKEN_TILE_ROWS, TOKEN_TILE_ROWS), TOKEN_TILE_ROWS), :],
            g_sem.at[buf])

    def scatter_row(dst, buf, j):
        return pltpu.make_async_copy(obuf.at[buf, pl.ds(j, 1), :], y_hbm.at[pl.ds(dst, 1), :], s_sem.at[buf])

    def wait_gather(buf):
        pltpu.make_async_copy(h_hbm.at[pl.ds(0, xrows), :], xbuf.at[pl.ds(pl.multiple_of(buf * xrows, xrows), xrows), :],
                              g_sem.at[buf]).wait()

    def wait_scatter(buf):
        pltpu.make_async_copy(obuf.at[buf], y_hbm.at[pl.ds(0, tm), :], s_sem.at[buf]).wait()

    @pl.when(i == 0)
    def _():
        first = pltpu.make_async_copy(src_hbm.at[pl.ds(0, 2)], first_smem, first_sem.at[0])
        first.start()
        plan_copy(0, 0).start()
        for cp in weight_copies(te_ref[0]):
            cp.start()
        first.wait()
        for tile in range(2):
            def body(j, carry, tile=tile):
                gather_row(first_smem[tile, j], tile, j).start()
                return carry
            lax.fori_loop(0, tm, body, 0, unroll=8)
        obuf[nb - 1] = jnp.zeros((tm, d), F32)

    @pl.when(i < n_used)
    def _():
        plan_copy(i, p0).wait()
        plan_copy(i + 1, p1).start()
        wait_gather(b0)

        @pl.when(i >= 2)
        def _():
            wait_scatter(b0)

        new_expert = jnp.logical_or(i == 0, te_ref[i] != te_ref[jnp.maximum(i - 1, 0)])

        @pl.when(new_expert)
        def _():
            for cp in weight_copies(te_ref[i]):
                cp.wait()
            wgb[...] = wgs[...].astype(BF16)
            wub[...] = wus[...].astype(BF16)
            wdb[...] = wds[...].astype(BF16)

            @pl.when(nx_ref[i] != te_ref[i])
            def _():
                for cp in weight_copies(nx_ref[i]):
                    cp.start()

        copies = []
        for j in range(tm):
            copies += [(0, j), (1, j)]
        n_up = de // MOE_N_CHUNK
        n_down = d // MOE_N_CHUNK
        n_pieces = 2 * n_up + n_down
        per_piece = -(-len(copies) // n_pieces)

        def issue(piece):
            for kind, j in copies[piece * per_piece:(piece + 1) * per_piece]:
                if kind == 0:
                    gather_row(plan_smem[p0, 0, j], b2, j).start()
                else:
                    scatter_row(plan_smem[p0, 1, j], b2, j).start()

        xbase = pl.multiple_of(b0 * xrows, xrows)
        xb = _unpack_bf16_halves(jnp.concatenate(
            [xbuf[pl.ds(xbase + c, tm, stride=TOKEN_TILE_ROWS), :] for c in range(TOKEN_TILE_ROWS)], axis=1))
        piece = 0
        acts = []
        for n in range(n_up):
            cols = slice(n * MOE_N_CHUNK, (n + 1) * MOE_N_CHUNK)
            g = jnp.dot(xb, wgb[:, cols], preferred_element_type=F32)
            issue(piece)
            u = jnp.dot(xb, wub[:, cols], preferred_element_type=F32)
            issue(piece + 1)
            piece += 2
            acts.append(((g * jax.nn.sigmoid(g)) * u).astype(BF16))
        a = jnp.concatenate(acts, axis=1)
        for n in range(n_down):
            cols = slice(n * MOE_N_CHUNK, (n + 1) * MOE_N_CHUNK)
            obuf[b0, :, cols] = jnp.dot(a, wdb[:, cols], preferred_element_type=F32)
            issue(piece)
            piece += 1

    @pl.when(i == n_used)
    def _():
        plan_copy(i, p0).wait()
        wait_gather(b0)
        wait_gather(b1)
        wait_scatter(b1)

        @pl.when(i >= 2)
        def _():
            wait_scatter(b0)

        def body(j, carry):
            scatter_row(plan_smem[p0, 1, j], b2, j).start()
            return carry
        lax.fori_loop(0, tm, body, 0, unroll=8)
        wait_scatter(b2)
        obuf[b0] = jnp.zeros((tm, d), F32)
        for k in range(nb):
            fill = pltpu.make_async_copy(obuf.at[b0], y_hbm.at[pl.ds(n_slots + k * tm, tm), :], s_sem.at[b0])
            fill.start()
            fill.wait()


def _moe(layer, h2, src2d, dst2d, tile_expert, next_expert, n_used, w_gate, w_up, w_down):
    t = h2.shape[0] // TOKEN_TILE_ROWS
    n_slots = 2 * t
    steps, tm = dst2d.shape
    d, de = w_gate.shape[-2:]
    hbm = pl.BlockSpec(memory_space=pl.ANY)
    grid_spec = pltpu.PrefetchScalarGridSpec(
        num_scalar_prefetch=3,
        grid=(steps,),
        in_specs=[hbm] * 6,
        out_specs=hbm,
        scratch_shapes=[
            pltpu.SMEM((2, tm), jnp.int32),
            pltpu.SMEM((2, 2, tm), jnp.int32),
            pltpu.VMEM((MOE_BUFS * tm * TOKEN_TILE_ROWS, LANES), jnp.uint32),
            pltpu.VMEM((MOE_BUFS, tm, d), F32),
            pltpu.VMEM((d, de), F32),
            pltpu.VMEM((d, de), F32),
            pltpu.VMEM((de, d), F32),
            pltpu.VMEM((d, de), BF16),
            pltpu.VMEM((d, de), BF16),
            pltpu.VMEM((de, d), BF16),
            pltpu.SemaphoreType.DMA((1,)),
            pltpu.SemaphoreType.DMA((2, 2)),
            pltpu.SemaphoreType.DMA((MOE_BUFS,)),
            pltpu.SemaphoreType.DMA((MOE_BUFS,)),
            pltpu.SemaphoreType.DMA((3,)),
        ],
    )
    return pl.pallas_call(
        functools.partial(_moe_body, layer, n_slots),
        grid_spec=grid_spec,
        out_shape=jax.ShapeDtypeStruct((n_slots + MOE_BUFS * tm, d), F32),
        compiler_params=_cparams(("arbitrary",)),
        name="moe",
    )(tile_expert, next_expert, n_used, h2, src2d, dst2d, w_gate, w_up, w_down)


def _route_plan(ids, counts, t):
    tm = MOE_TM
    nt = (2 * t) // tm + N_EXPERTS
    cnt = counts[0, :N_EXPERTS]
    tiles = (cnt + tm - 1) // tm
    tile_end = jnp.cumsum(tiles)
    off = ((tile_end - tiles) * tm).astype(jnp.int32)
    tile_buf = np.arange(-1, nt) % MOE_BUFS
    dump = (2 * t + tile_buf[:, None] * tm + np.arange(tm)[None, :]).astype(np.int32)
    dst2d = _plan_scatter(off, ids[:, 0:4].reshape(-1), jnp.asarray(dump), t)
    slot = dst2d[1:]
    src_tok = jnp.where(slot < t, slot, jnp.where(slot < 2 * t, slot - t, 0))
    src2d = jnp.concatenate([src_tok, jnp.zeros((3, tm), jnp.int32)], axis=0) * TOKEN_TILE_ROWS
    n_used = tile_end[-1]
    steps = jnp.minimum(jnp.arange(nt + 1, dtype=jnp.int32), n_used - 1)
    tile_expert = jnp.sum(tile_end[None, :] <= steps[:, None], axis=1).astype(jnp.int32)
    eid = jnp.arange(N_EXPERTS, dtype=jnp.int32)
    later_used = (tiles[None, :] > 0) & (eid[None, :] > eid[:, None])
    nxt = jnp.min(jnp.where(later_used, eid[None, :], N_EXPERTS), axis=1)
    next_expert = jnp.where(nxt == N_EXPERTS, eid, nxt)[tile_expert].astype(jnp.int32)
    return src2d, dst2d, tile_expert, next_expert, n_used.reshape(1).astype(jnp.int32)


PLAN_CHUNK = 8192


def _plan_body(t, off_ref, ids_hbm, init_hbm, out_hbm, tab, ids_a, ids_b, init_sem, ids_sem, out_sem):
    chunk = ids_a.shape[0]
    n_chunks = (4 * t) // chunk
    bufs = (ids_a, ids_b)

    def ids_copy(c):
        return pltpu.make_async_copy(ids_hbm.at[pl.ds(c * chunk, chunk)], bufs[c % 2], ids_sem.at[c % 2])

    init = pltpu.make_async_copy(init_hbm, tab, init_sem.at[0])
    init.start()
    ids_copy(0).start()
    init.wait()
    for c in range(n_chunks):
        ids_copy(c).wait()
        if c + 1 < n_chunks:
            ids_copy(c + 1).start()
        buf = bufs[c % 2]
        tok0 = c * (chunk // 4)

        def body(tt, carry, buf=buf, tok0=tok0):
            b = 4 * tt
            tok = tok0 + tt
            tab[off_ref[buf[b]] + buf[b + 2]] = tok
            tab[off_ref[buf[b + 1]] + buf[b + 3]] = tok + t
            return carry
        lax.fori_loop(0, chunk // 4, body, 0, unroll=8)
    out = pltpu.make_async_copy(tab, out_hbm, out_sem.at[0])
    out.start()
    out.wait()


def _plan_scatter(off, ids_flat, dump, t):
    rows, tm = dump.shape
    chunk = min(PLAN_CHUNK, 4 * t)
    assert (4 * t) % chunk == 0
    words = -(-rows * tm // 1024) * 1024
    init = jnp.concatenate([dump.reshape(-1), jnp.zeros((words - rows * tm,), jnp.int32)])
    hbm = pl.BlockSpec(memory_space=pl.ANY)
    grid_spec = pltpu.PrefetchScalarGridSpec(
        num_scalar_prefetch=1,
        grid=(1,),
        in_specs=[hbm, hbm],
        out_specs=hbm,
        scratch_shapes=[
            pltpu.SMEM((words,), jnp.int32),
            pltpu.SMEM((chunk,), jnp.int32),
            pltpu.SMEM((chunk,), jnp.int32),
            pltpu.SemaphoreType.DMA((1,)),
            pltpu.SemaphoreType.DMA((2,)),
            pltpu.SemaphoreType.DMA((1,)),
        ],
    )
    flat = pl.pallas_call(
        functools.partial(_plan_body, t),
        grid_spec=grid_spec,
        out_shape=jax.ShapeDtypeStruct((words,), jnp.int32),
        compiler_params=_cparams(("arbitrary",)),
        name="plan",
    )(off + tm, ids_flat, init)
    return flat[:rows * tm].reshape(rows, tm)


def _final_body(x_ref, y0_ref, y1_ref, wt_ref, pmod_ref, g_ref, o_ref):
    x = _combine(x_ref[...], y0_ref, y1_ref, wt_ref, pmod_ref[0, 5:6, :])
    ms = jnp.mean(x * x, axis=-1, keepdims=True)
    o_ref[...] = x * lax.rsqrt(ms + NORM_EPS) * g_ref[...]


def _final(x, y2, wts, pmod, g_final, seq):
    t, d = x.shape
    tm = FINAL_TM
    tiles_per_seq = seq // tm
    row = lambda i: (i, 0)
    return pl.pallas_call(
        _final_body,
        grid=(t // tm,),
        in_specs=[
            pl.BlockSpec((tm, d), row),
            pl.BlockSpec((tm, d), row),
            pl.BlockSpec((tm, d), lambda i: (i + t // tm, 0)),
            pl.BlockSpec((tm, LANES), row),
            pl.BlockSpec((1, N_MOD, d), lambda i: (i // tiles_per_seq, 0, 0)),
            pl.BlockSpec((1, d), lambda i: (0, 0)),
        ],
        out_specs=pl.BlockSpec((tm, d), row),
        out_shape=jax.ShapeDtypeStruct((t, d), F32),
        compiler_params=_cparams(("arbitrary",)),
        name="final",
    )(x, y2, y2, wts, pmod, g_final)


def _rotary_tables(positions):
    half = ROT_DIM // 2
    inv_freq = ROPE_THETA ** (-(jnp.arange(0, ROT_DIM, 2, dtype=F32) / ROT_DIM))
    ang = positions.reshape(-1).astype(F32)[:, None] * inv_freq
    cos, sin = jnp.cos(ang), jnp.sin(ang)
    t = ang.shape[0]
    pad = HEAD_DIM - ROT_DIM
    cos_h = jnp.concatenate([cos, cos, jnp.ones((t, pad), F32)], axis=1)
    sa_h = jnp.concatenate([-sin, jnp.zeros((t, HEAD_DIM - half), F32)], axis=1)
    sb_h = jnp.concatenate([jnp.zeros((t, half), F32), sin, jnp.zeros((t, pad), F32)], axis=1)
    rep = LANES // HEAD_DIM
    return jnp.tile(cos_h, (1, rep)), jnp.tile(sa_h, (1, rep)), jnp.tile(sb_h, (1, rep))


def kernel(x, c, positions, w_ada, b_ada, g_norm1, w_in, b_in, sinks, ln_v_g, ln_v_b, w_spatial, b_spatial,
           g_attn_out, g_gmlp_out, w_out, g_norm2, w_router_group, b_router_group, w_router_expert,
           b_router_expert, w_e_gate, w_e_up, w_e_down, g_final):
    batch, seq, d = x.shape
    depth = w_ada.shape[0]
    t = batch * seq
    assert d == D_MODEL and seq % max(PROJ_TM, OUT_TM, FINAL_TM, MIX_BLOCKS * BLOCK) == 0 and (2 * t) % MOE_TM == 0

    c_pad = jnp.zeros((SUBLANES, d), F32).at[:batch].set(c)
    mod = _adaln(c_pad, w_ada, b_ada).reshape(depth, SUBLANES, N_MOD, d)
    cos_t, sa_t, sb_t = _rotary_tables(positions)

    n_route = N_GROUPS + N_EXPERTS
    w_router = jnp.concatenate(
        [w_router_group, w_router_expert, jnp.zeros((depth, d, LANES - n_route), F32)], axis=-1)
    b_router = jnp.concatenate(
        [b_router_group, b_router_expert, jnp.zeros((depth, LANES - n_route), F32)], axis=-1)

    xt = x.reshape(t, d)
    prev = None
    for l in range(depth):
        xt, qkv, z = _proj(xt, prev, mod[l], g_norm1[l][None], w_in[l].astype(BF16), b_in[l][None],
                           cos_t, sa_t, sb_t, seq)
        merged = _mixer(qkv, z, sinks[l], w_spatial[l], b_spatial[l].T, ln_v_g[l][None], ln_v_b[l][None],
                        g_attn_out[l][None], g_gmlp_out[l][None], batch, seq)
        xt, h2, ids, wts, counts = _outproj(merged, xt, mod[l], g_norm2[l][None], w_out[l].astype(BF16),
                                            w_router[l].astype(BF16), b_router[l][None], seq)
        src2d, dst2d, tile_expert, next_expert, n_used = _route_plan(ids, counts, t)
        y2 = _moe(l, h2, src2d, dst2d, tile_expert, next_expert, n_used, w_e_gate, w_e_up, w_e_down)
        prev = (y2, wts, mod[l])
    out = _final(xt, prev[0], prev[1], prev[2], g_final[None], seq)
    return out.reshape(batch, seq, d)
```

```python
import functools

import jax
import jax.numpy as jnp
import numpy as np
from jax import lax
from jax.experimental import pallas as pl
from jax.experimental.pallas import tpu as pltpu

F32 = jnp.float32
BF16 = jnp.bfloat16

D_MODEL = 2048
ATTN_WIDTH = D_MODEL // 2
HEAD_DIM = 64
N_Q_HEADS = ATTN_WIDTH // HEAD_DIM
N_KV_HEADS = 2
GQA_GROUP = N_Q_HEADS // N_KV_HEADS
WINDOW = 128
BLOCK = 128
ROT_DIM = HEAD_DIM // 4
ROPE_THETA = 500000.0
GMLP_WIDTH = D_MODEL - ATTN_WIDTH
GMLP_GROUPS = 8
GMLP_GROUP_DIM = GMLP_WIDTH // GMLP_GROUPS
Q_COLS = N_Q_HEADS * HEAD_DIM
KV_COLS = N_KV_HEADS * HEAD_DIM
QKV_COLS = Q_COLS + 2 * KV_COLS
IN_COLS = QKV_COLS + 2 * GMLP_WIDTH
N_GROUPS = 4
EXPERTS_PER_GROUP = 8
N_EXPERTS = N_GROUPS * EXPERTS_PER_GROUP
D_EXPERT = D_MODEL // 4
NORM_EPS = 1e-6
N_MOD = 6

LANES = 128
SUBLANES = 8
VMEM_LIMIT_BYTES = 56 * 1024 * 1024
TOKEN_TILE_ROWS = D_MODEL // 2 // LANES

ADA_TN = 1024
PROJ_TM = 256
OUT_TM = 512
MIX_BLOCKS = 2
MOE_TM = 256
MOE_N_CHUNK = 256
MOE_BUFS = 3
FINAL_TM = 512


def _cparams(sem):
    return pltpu.CompilerParams(dimension_semantics=sem, vmem_limit_bytes=VMEM_LIMIT_BYTES)


def _adaln_body(c_ref, w_ref, b_ref, o_ref):
    c = c_ref[...]
    ca = (c * jax.nn.sigmoid(c)).astype(BF16)
    w = w_ref[0].astype(BF16)
    o_ref[0] = jnp.dot(ca, w, preferred_element_type=F32) + b_ref[0]


def _adaln(c_pad, w_ada, b_ada):
    depth, d, n = w_ada.shape
    rows = c_pad.shape[0]
    return pl.pallas_call(
        _adaln_body,
        grid=(depth, n // ADA_TN),
        in_specs=[
            pl.BlockSpec((rows, d), lambda l, j: (0, 0)),
            pl.BlockSpec((1, d, ADA_TN), lambda l, j: (l, 0, j)),
            pl.BlockSpec((1, 1, ADA_TN), lambda l, j: (l, 0, j)),
        ],
        out_specs=pl.BlockSpec((1, rows, ADA_TN), lambda l, j: (l, 0, j)),
        out_shape=jax.ShapeDtypeStruct((depth, rows, n), F32),
        compiler_params=_cparams(("arbitrary", "arbitrary")),
        name="adaln",
    )(c_pad, w_ada, b_ada.reshape(depth, 1, n))


def _rms_mod(x, g, sc, sh):
    ms = jnp.mean(x * x, axis=-1, keepdims=True)
    return (x * lax.rsqrt(ms + NORM_EPS) * g) * (1.0 + sc) + sh


def _pack_bf16_halves(h):
    half = h.shape[1] // 2
    return _pack_bf16_pair(h[:, :half], h[:, half:])


def _pack_bf16_pair(lo, hi):
    lo = lax.bitcast_convert_type(lo.astype(BF16).astype(F32), jnp.uint32)
    hi = lax.bitcast_convert_type(hi.astype(BF16).astype(F32), jnp.uint32)
    return lax.shift_right_logical(lo, jnp.uint32(16)) | (hi & jnp.uint32(0xFFFF0000))


def _load_token_tiles(ref, base, n):
    return jnp.concatenate(
        [ref[pl.ds(base + c, n, stride=TOKEN_TILE_ROWS), :] for c in range(TOKEN_TILE_ROWS)], axis=1)


def _unpack_bf16_halves(p):
    lo = lax.bitcast_convert_type(lax.shift_left(p, jnp.uint32(16)), F32).astype(BF16)
    hi = lax.bitcast_convert_type(p & jnp.uint32(0xFFFF0000), F32).astype(BF16)
    return jnp.concatenate([lo, hi], axis=1)


def _combine(x, y0_ref, y1_ref, wt_ref, gate):
    wt = wt_ref[...]
    n = x.shape[0]
    y0 = _unpack_bf16_halves(_load_token_tiles(y0_ref, 0, n)).astype(F32)
    y1 = _unpack_bf16_halves(_load_token_tiles(y1_ref, 0, n)).astype(F32)
    y = wt[:, 0:1] * y0 + wt[:, 1:2] * y1
    return x + (1.0 + gate) * y


def _gelu_exact(z):
    return 0.5 * z * (1.0 + lax.erf(z * (2.0 ** -0.5)))


def _proj_body(has_prev, *refs):
    if has_prev:
        (x_ref, y0_ref, y1_ref, wt_ref, pmod_ref, mod_ref, g_ref, w_ref, b_ref, cos_ref, sa_ref, sb_ref,
         xo_ref, qkv_ref, z_ref) = refs
        x = _combine(x_ref[...], y0_ref, y1_ref, wt_ref, pmod_ref[0, 5:6, :])
        xo_ref[...] = x
    else:
        x_ref, mod_ref, g_ref, w_ref, b_ref, cos_ref, sa_ref, sb_ref, qkv_ref, z_ref = refs
        x = x_ref[...]
    mod = mod_ref[0]
    h = _rms_mod(x, g_ref[...], mod[1:2, :], mod[0:1, :]).astype(BF16)

    qkv = jnp.dot(h, w_ref[:, :QKV_COLS], preferred_element_type=F32) + b_ref[:, :QKV_COLS]
    cos, sa, sb = cos_ref[...], sa_ref[...], sb_ref[...]
    n_rot = (Q_COLS + KV_COLS) // LANES
    for j in range(QKV_COLS // LANES):
        ch = qkv[:, j * LANES:(j + 1) * LANES]
        if j < n_rot:
            ch = ch * cos + pltpu.roll(ch, LANES - ROT_DIM // 2, 1) * sa + pltpu.roll(ch, ROT_DIM // 2, 1) * sb
        if j < Q_COLS // LANES:
            ch = ch * (HEAD_DIM ** -0.5)
        qkv_ref[:, j * LANES:(j + 1) * LANES] = ch.astype(BF16)

    z = jnp.dot(h, w_ref[:, QKV_COLS:], preferred_element_type=F32) + b_ref[:, QKV_COLS:]
    z_ref[...] = _gelu_exact(z)


def _proj(x, prev, mod_l, g1, w_in, b_in, cos_t, sa_t, sb_t, seq):
    t, d = x.shape
    tm = PROJ_TM
    tiles_per_seq = seq // tm
    row = lambda i: (i, 0)
    const = lambda i: (0, 0)
    modmap = lambda i: (i // tiles_per_seq, 0, 0)
    in_specs = [pl.BlockSpec((tm, d), row)]
    args = [x]
    if prev is not None:
        y2, wts, pmod = prev
        in_specs += [pl.BlockSpec((tm * TOKEN_TILE_ROWS, LANES), row),
                     pl.BlockSpec((tm * TOKEN_TILE_ROWS, LANES), lambda i: (i + t // tm, 0)),
                     pl.BlockSpec((tm, LANES), row), pl.BlockSpec((1, N_MOD, d), modmap)]
        args += [y2, y2, wts, pmod]
    in_specs += [
        pl.BlockSpec((1, N_MOD, d), modmap),
        pl.BlockSpec((1, d), const),
        pl.BlockSpec((d, IN_COLS), const, pipeline_mode=pl.Buffered(1)),
        pl.BlockSpec((1, IN_COLS), const),
        pl.BlockSpec((tm, LANES), row),
        pl.BlockSpec((tm, LANES), row),
        pl.BlockSpec((tm, LANES), row),
    ]
    args += [mod_l, g1, w_in, b_in, cos_t, sa_t, sb_t]
    out_specs = [pl.BlockSpec((tm, QKV_COLS), row), pl.BlockSpec((tm, 2 * GMLP_WIDTH), row)]
    out_shape = [jax.ShapeDtypeStruct((t, QKV_COLS), BF16), jax.ShapeDtypeStruct((t, 2 * GMLP_WIDTH), F32)]
    if prev is not None:
        out_specs = [pl.BlockSpec((tm, d), row)] + out_specs
        out_shape = [jax.ShapeDtypeStruct((t, d), F32)] + out_shape
    outs = pl.pallas_call(
        functools.partial(_proj_body, prev is not None),
        grid=(t // tm,),
        in_specs=in_specs,
        out_specs=out_specs,
        out_shape=out_shape,
        compiler_params=_cparams(("arbitrary",)),
        name="proj",
    )(*args)
    if prev is None:
        return (x,) + tuple(outs)
    return tuple(outs)


def _mixer_body(sink_ref, q_ref, kvc_ref, kvp_ref, z_ref, ws_ref, bst_ref, lng_ref, lnb_ref, ga_ref, gg_ref,
                o_ref):
    n = pl.program_id(1)
    for sub in range(MIX_BLOCKS):
        r0 = sub * BLOCK
        kvp = kvp_ref[...] if sub == 0 else kvc_ref[r0 - BLOCK:r0, :]
        has_prev = (n > 0) if sub == 0 else None
        _mixer_block(sink_ref, q_ref[r0:r0 + BLOCK, :], kvc_ref[r0:r0 + BLOCK, :], kvp, has_prev, z_ref, r0,
                     ws_ref, bst_ref, lng_ref, lnb_ref, ga_ref, gg_ref, o_ref)


def _mixer_block(sink_ref, q, kvc, kvp, has_prev, z_ref, r0, ws_ref, bst_ref, lng_ref, lnb_ref, ga_ref, gg_ref,
                 o_ref):
    rows = slice(r0, r0 + BLOCK)
    kcat = jnp.concatenate([kvp[:, :KV_COLS], kvc[:, :KV_COLS]], axis=0)
    vcat = jnp.concatenate([kvp[:, KV_COLS:], kvc[:, KV_COLS:]], axis=0)

    lane = lax.broadcasted_iota(jnp.int32, (BLOCK, LANES), 1)
    lo = lane < HEAD_DIM
    kj = lax.broadcasted_iota(jnp.int32, (2 * BLOCK, BLOCK), 0)
    qi = lax.broadcasted_iota(jnp.int32, (2 * BLOCK, BLOCK), 1)
    rel = qi + BLOCK - kj
    valid = (rel >= 0) & (rel < WINDOW)
    if has_prev is not None:
        valid = valid & (has_prev | (kj >= BLOCK))
    row_lo = lax.broadcasted_iota(jnp.int32, (LANES, BLOCK), 0) < HEAD_DIM
    zero_q = jnp.zeros((BLOCK, LANES), BF16)
    nt_dims = (((1,), (1,)), ((), ()))

    attn = []
    chunks_per_kv = GQA_GROUP * HEAD_DIM // LANES
    for hk in range(N_KV_HEADS):
        ksl = kcat[:, hk * HEAD_DIM:(hk + 1) * HEAD_DIM]
        vsl = vcat[:, hk * HEAD_DIM:(hk + 1) * HEAD_DIM]
        kk = jnp.concatenate([ksl, ksl], axis=1)
        vt = jnp.concatenate([vsl, vsl], axis=1).astype(F32).T.astype(BF16)
        for j in range(chunks_per_kv):
            c = hk * chunks_per_kv + j
            qc = q[:, c * LANES:(c + 1) * LANES]
            probs = []
            for half in range(2):
                qm = jnp.where(lo, qc, zero_q) if half == 0 else jnp.where(lo, zero_q, qc)
                s = lax.dot_general(kk, qm, nt_dims, preferred_element_type=F32)
                s = jnp.where(valid, s, -jnp.inf)
                sink = sink_ref[2 * c + half]
                m = jnp.maximum(jnp.max(s, axis=0, keepdims=True), sink)
                p = jnp.exp(s - m)
                denom = jnp.sum(p, axis=0, keepdims=True) + jnp.exp(sink - m)
                probs.append((p * (1.0 / denom)).astype(BF16))
            pn = jnp.concatenate(probs, axis=1)
            ot = jnp.dot(vt, pn, preferred_element_type=F32)
            sel = jnp.where(row_lo, ot[:, :BLOCK], ot[:, BLOCK:])
            attn.append(sel.T)

    ss = attn[0] * attn[0]
    for a in attn[1:]:
        ss = ss + a * a
    scale = lax.rsqrt(jnp.sum(ss, axis=-1, keepdims=True) * (1.0 / ATTN_WIDTH) + NORM_EPS)
    for c, a in enumerate(attn):
        o_ref[rows, c * LANES:(c + 1) * LANES] = (a * scale * ga_ref[:, c * LANES:(c + 1) * LANES]).astype(BF16)

    u = z_ref[rows, :GMLP_WIDTH]
    v = z_ref[rows, GMLP_WIDTH:]
    mu = jnp.mean(v, axis=-1, keepdims=True)
    vc = v - mu
    var = jnp.mean(vc * vc, axis=-1, keepdims=True)
    vn = ((vc * lax.rsqrt(var + NORM_EPS)) * lng_ref[...] + lnb_ref[...]).astype(BF16)
    ti = lax.broadcasted_iota(jnp.int32, (BLOCK, BLOCK), 0)
    si = lax.broadcasted_iota(jnp.int32, (BLOCK, BLOCK), 1)
    tril = si <= ti
    bst = bst_ref[...]
    gm = []
    for h in range(GMLP_GROUPS):
        w = jnp.where(tril, ws_ref[h], 0.0).astype(BF16)
        sp = jnp.dot(w, vn[:, h * LANES:(h + 1) * LANES], preferred_element_type=F32) + bst[:, h:h + 1]
        gm.append(u[:, h * LANES:(h + 1) * LANES] * sp)
    ss = gm[0] * gm[0]
    for a in gm[1:]:
        ss = ss + a * a
    scale = lax.rsqrt(jnp.sum(ss, axis=-1, keepdims=True) * (1.0 / GMLP_WIDTH) + NORM_EPS)
    for c, a in enumerate(gm):
        o_ref[rows, ATTN_WIDTH + c * LANES:ATTN_WIDTH + (c + 1) * LANES] = (
            a * scale * gg_ref[:, c * LANES:(c + 1) * LANES]).astype(BF16)


def _mixer(qkv, z, sinks, w_sp, bst, ln_g, ln_b, g_attn, g_gmlp, batch, seq):
    t = qkv.shape[0]
    nb = seq // BLOCK
    steps = nb // MIX_BLOCKS
    rows = MIX_BLOCKS * BLOCK
    kv_blk = Q_COLS // (2 * KV_COLS)
    cur = lambda b, n, s: (b * steps + n, 0)
    const2 = lambda b, n, s: (0, 0)
    grid_spec = pltpu.PrefetchScalarGridSpec(
        num_scalar_prefetch=1,
        grid=(batch, steps),
        in_specs=[
            pl.BlockSpec((rows, Q_COLS), cur),
            pl.BlockSpec((rows, 2 * KV_COLS), lambda b, n, s: (b * steps + n, kv_blk)),
            pl.BlockSpec((BLOCK, 2 * KV_COLS),
                         lambda b, n, s: (b * nb + jnp.maximum(MIX_BLOCKS * n - 1, 0), kv_blk)),
            pl.BlockSpec((rows, 2 * GMLP_WIDTH), cur),
            pl.BlockSpec((GMLP_GROUPS, BLOCK, BLOCK), lambda b, n, s: (0, 0, 0)),
            pl.BlockSpec((BLOCK, GMLP_GROUPS), const2),
            pl.BlockSpec((1, GMLP_WIDTH), const2),
            pl.BlockSpec((1, GMLP_WIDTH), const2),
            pl.BlockSpec((1, ATTN_WIDTH), const2),
            pl.BlockSpec((1, GMLP_WIDTH), const2),
        ],
        out_specs=pl.BlockSpec((rows, D_MODEL), cur),
    )
    return pl.pallas_call(
        _mixer_body,
        grid_spec=grid_spec,
        out_shape=jax.ShapeDtypeStruct((t, D_MODEL), BF16),
        compiler_params=_cparams(("arbitrary", "arbitrary")),
        name="mixer",
    )(sinks, qkv, qkv, qkv, z, w_sp, bst, ln_g, ln_b, g_attn, g_gmlp)


def _first_index_of_max(vals, lane, width):
    mx = jnp.max(vals, axis=-1, keepdims=True)
    idx = jnp.min(jnp.where(vals == mx, lane, width), axis=-1, keepdims=True)
    return mx, idx


def _outproj_body(m_ref, x_ref, mod_ref, g_ref, w_ref, wr_ref, br_ref,
                  xo_ref, h_ref, ids_ref, wt_ref, cnt_ref, carry_ref):
    i = pl.program_id(0)
    tm = x_ref.shape[0]

    @pl.when(i == 0)
    def _():
        carry_ref[...] = jnp.zeros_like(carry_ref)

    mod = mod_ref[0]
    y = jnp.dot(m_ref[...], w_ref[...], preferred_element_type=F32)
    x = x_ref[...] + (1.0 + mod[2:3, :]) * y
    xo_ref[...] = x
    h = _rms_mod(x, g_ref[...], mod[4:5, :], mod[3:4, :])
    packed = _pack_bf16_halves(h)
    for c in range(TOKEN_TILE_ROWS):
        h_ref[pl.ds(c, tm, stride=TOKEN_TILE_ROWS), :] = packed[:, c * LANES:(c + 1) * LANES]

    logits = jnp.dot(h.astype(BF16), wr_ref[...], preferred_element_type=F32) + br_ref[...]
    lane = lax.broadcasted_iota(jnp.int32, (tm, LANES), 1)
    neg = -jnp.inf
    gl = jnp.where(lane < N_GROUPS, logits, neg)
    ge = jnp.exp(gl - jnp.max(gl, axis=-1, keepdims=True))
    gprob = ge / jnp.sum(ge, axis=-1, keepdims=True)
    g_p, g_idx = _first_index_of_max(gprob, lane, LANES)

    base = N_GROUPS + EXPERTS_PER_GROUP * g_idx
    in_grp = (lane >= base) & (lane < base + EXPERTS_PER_GROUP)
    el = jnp.where(in_grp, logits, neg)
    ee = jnp.exp(el - jnp.max(el, axis=-1, keepdims=True))
    eprob = jnp.where(in_grp, ee / jnp.sum(ee, axis=-1, keepdims=True), -1.0)
    p1, i1 = _first_index_of_max(eprob, lane, LANES)
    eprob2 = jnp.where(lane == i1, -1.0, eprob)
    p2, i2 = _first_index_of_max(eprob2, lane, LANES)
    psum = p1 + p2
    w1 = g_p * (p1 / psum)
    w2 = g_p * (p2 / psum)
    e1 = i1 - N_GROUPS
    e2 = i2 - N_GROUPS

    oh1 = lane == e1
    oh2 = lane == e2
    onehot = jnp.where(oh1 | oh2, 1.0, 0.0)
    ri = lax.broadcasted_iota(jnp.int32, (tm, tm), 0)
    ci = lax.broadcasted_iota(jnp.int32, (tm, tm), 1)
    lower = jnp.where(ci < ri, 1.0, 0.0).astype(BF16)
    before = jnp.dot(lower, onehot.astype(BF16), preferred_element_type=F32) + carry_ref[...]
    r1 = jnp.sum(jnp.where(oh1, before, 0.0), axis=-1, keepdims=True)
    r2 = jnp.sum(jnp.where(oh2, before, 0.0), axis=-1, keepdims=True)
    carry = carry_ref[...] + jnp.sum(onehot, axis=0, keepdims=True)
    carry_ref[...] = carry
    cnt_ref[...] = jnp.broadcast_to(carry, cnt_ref.shape).astype(jnp.int32)

    ids = jnp.where(lane == 0, e1, jnp.where(lane == 1, e2, 0))
    ranks = jnp.where(lane == 2, r1, jnp.where(lane == 3, r2, 0.0)).astype(jnp.int32)
    ids_ref[...] = ids + ranks
    wt_ref[...] = jnp.where(lane == 0, w1, jnp.where(lane == 1, w2, 0.0))


def _outproj(merged, x, mod_l, g2, w_out, w_router, b_router, seq):
    t, d = x.shape
    tm = OUT_TM
    tiles_per_seq = seq // tm
    row = lambda i: (i, 0)
    const = lambda i: (0, 0)
    return pl.pallas_call(
        _outproj_body,
        grid=(t // tm,),
        in_specs=[
            pl.BlockSpec((tm, d), row),
            pl.BlockSpec((tm, d), row),
            pl.BlockSpec((1, N_MOD, d), lambda i: (i // tiles_per_seq, 0, 0)),
            pl.BlockSpec((1, d), const),
            pl.BlockSpec((d, d), const, pipeline_mode=pl.Buffered(1)),
            pl.BlockSpec((d, LANES), const),
            pl.BlockSpec((1, LANES), const),
        ],
        out_specs=[
            pl.BlockSpec((tm, d), row),
            pl.BlockSpec((tm * TOKEN_TILE_ROWS, LANES), row),
            pl.BlockSpec((tm, LANES), row),
            pl.BlockSpec((tm, LANES), row),
            pl.BlockSpec((SUBLANES, LANES), const),
        ],
        out_shape=[
            jax.ShapeDtypeStruct((t, d), F32),
            jax.ShapeDtypeStruct((t * TOKEN_TILE_ROWS, LANES), jnp.uint32),
            jax.ShapeDtypeStruct((t, LANES), jnp.int32),
            jax.ShapeDtypeStruct((t, LANES), F32),
            jax.ShapeDtypeStruct((SUBLANES, LANES), jnp.int32),
        ],
        scratch_shapes=[pltpu.VMEM((1, LANES), F32)],
        compiler_params=_cparams(("arbitrary",)),
        name="outproj",
    )(merged, x, mod_l, g2, w_out, w_router, b_router)


def _moe_body(layer, n_slots, te_ref, nx_ref, nu_ref, h_hbm, src_hbm, dst_hbm, wg_hbm, wu_hbm, wd_hbm, y_hbm,
              first_smem, plan_smem, xbuf, obuf, wgs, wus, wds, wgb, wub, wdb,
              first_sem, plan_sem, g_sem, s_sem, w_sem):
    i = pl.program_id(0)
    n_used = nu_ref[0]
    nb = MOE_BUFS
    tm = plan_smem.shape[2]
    d, de = wgb.shape

    def weight_copies(e):
        return (pltpu.make_async_copy(wg_hbm.at[layer, e], wgs, w_sem.at[0]),
                pltpu.make_async_copy(wu_hbm.at[layer, e], wus, w_sem.at[1]),
                pltpu.make_async_copy(wd_hbm.at[layer, e], wds, w_sem.at[2]))
    b0 = i % nb
    b1 = (i + 1) % nb
    b2 = (i + 2) % nb
    p0 = i % 2
    p1 = 1 - p0

    class _PlanCopy:
        def __init__(self, k, slot):
            self.copies = (
                pltpu.make_async_copy(src_hbm.at[k + 2], plan_smem.at[slot, 0], plan_sem.at[slot, 0]),
                pltpu.make_async_copy(dst_hbm.at[k], plan_smem.at[slot, 1], plan_sem.at[slot, 1]))

        def start(self):
            for cp in self.copies:
                cp.start()

        def wait(self):
            for cp in self.copies:
                cp.wait()

    plan_copy = _PlanCopy

    xrows = tm * TOKEN_TILE_ROWS

    def token_tile(row):
        return pl.ds(pl.multiple_of(row, TOKEN_TILE_ROWS), TOKEN_TILE_ROWS)

    def tile_buf(buf):
        return pl.ds(pl.multiple_of(buf * xrows, xrows), xrows)

    def gather_row(src, buf, j):
        return pltpu.make_async_copy(h_hbm.at[token_tile(src), :],
                                     xbuf.at[token_tile(buf * xrows + j * TOKEN_TILE_ROWS), :], g_sem.at[buf])

    def scatter_row(dst, buf, j):
        return pltpu.make_async_copy(obuf.at[token_tile(buf * xrows + j * TOKEN_TILE_ROWS), :],
                                     y_hbm.at[token_tile(dst), :], s_sem.at[buf])

    def wait_gather(buf):
        pltpu.make_async_copy(h_hbm.at[pl.ds(0, xrows), :], xbuf.at[tile_buf(buf), :], g_sem.at[buf]).wait()

    def wait_scatter(buf):
        pltpu.make_async_copy(obuf.at[tile_buf(buf), :], y_hbm.at[pl.ds(0, xrows), :], s_sem.at[buf]).wait()

    @pl.when(i == 0)
    def _():
        first = pltpu.make_async_copy(src_hbm.at[pl.ds(0, 2)], first_smem, first_sem.at[0])
        first.start()
        plan_copy(0, 0).start()
        for cp in weight_copies(te_ref[0]):
            cp.start()
        first.wait()
        for tile in range(2):
            def body(j, carry, tile=tile):
                gather_row(first_smem[tile, j], tile, j).start()
                return carry
            lax.fori_loop(0, tm, body, 0, unroll=8)
        obuf[pl.ds((nb - 1) * xrows, xrows), :] = jnp.zeros((xrows, LANES), jnp.uint32)

    @pl.when(i < n_used)
    def _():
        plan_copy(i, p0).wait()
        plan_copy(i + 1, p1).start()
        wait_gather(b0)

        @pl.when(i >= 2)
        def _():
            wait_scatter(b0)

        new_expert = jnp.logical_or(i == 0, te_ref[i] != te_ref[jnp.maximum(i - 1, 0)])

        @pl.when(new_expert)
        def _():
            for cp in weight_copies(te_ref[i]):
                cp.wait()
            wgb[...] = wgs[...].astype(BF16)
            wub[...] = wus[...].astype(BF16)
            wdb[...] = wds[...].astype(BF16)

            @pl.when(nx_ref[i] != te_ref[i])
            def _():
                for cp in weight_copies(nx_ref[i]):
                    cp.start()

        copies = []
        for j in range(tm):
            copies += [(0, j), (1, j)]
        n_up = de // MOE_N_CHUNK
        n_down = d // MOE_N_CHUNK
        n_pieces = 2 * n_up + n_down
        per_piece = -(-len(copies) // n_pieces)

        def issue(piece):
            for kind, j in copies[piece * per_piece:(piece + 1) * per_piece]:
                if kind == 0:
                    gather_row(plan_smem[p0, 0, j], b2, j).start()
                else:
                    scatter_row(plan_smem[p0, 1, j], b2, j).start()

        xbase = pl.multiple_of(b0 * xrows, xrows)
        xb = _unpack_bf16_halves(_load_token_tiles(xbuf, xbase, tm))
        piece = 0
        acts = []
        for n in range(n_up):
            cols = slice(n * MOE_N_CHUNK, (n + 1) * MOE_N_CHUNK)
            g = jnp.dot(xb, wgb[:, cols], preferred_element_type=F32)
            issue(piece)
            u = jnp.dot(xb, wub[:, cols], preferred_element_type=F32)
            issue(piece + 1)
            piece += 2
            acts.append(((g * jax.nn.sigmoid(g)) * u).astype(BF16))
        a = jnp.concatenate(acts, axis=1)
        for n in range(n_down // 2):
            cols_lo = slice(n * MOE_N_CHUNK, (n + 1) * MOE_N_CHUNK)
            cols_hi = slice(d // 2 + n * MOE_N_CHUNK, d // 2 + (n + 1) * MOE_N_CHUNK)
            y_lo = jnp.dot(a, wdb[:, cols_lo], preferred_element_type=F32)
            issue(piece)
            y_hi = jnp.dot(a, wdb[:, cols_hi], preferred_element_type=F32)
            issue(piece + 1)
            piece += 2
            packed = _pack_bf16_pair(y_lo, y_hi)
            for c in range(MOE_N_CHUNK // LANES):
                obuf[pl.ds(xbase + n * (MOE_N_CHUNK // LANES) + c, tm, stride=TOKEN_TILE_ROWS), :] = (
                    packed[:, c * LANES:(c + 1) * LANES])

    @pl.when(i == n_used)
    def _():
        plan_copy(i, p0).wait()
        wait_gather(b0)
        wait_gather(b1)
        wait_scatter(b1)

        @pl.when(i >= 2)
        def _():
            wait_scatter(b0)

        def body(j, carry):
            scatter_row(plan_smem[p0, 1, j], b2, j).start()
            return carry
        lax.fori_loop(0, tm, body, 0, unroll=8)
        wait_scatter(b2)
        obuf[tile_buf(b0), :] = jnp.zeros((xrows, LANES), jnp.uint32)
        for k in range(nb):
            fill = pltpu.make_async_copy(obuf.at[tile_buf(b0), :],
                                         y_hbm.at[pl.ds((n_slots + k * tm) * TOKEN_TILE_ROWS, xrows), :], s_sem.at[b0])
            fill.start()
            fill.wait()


def _moe(layer, h2, src2d, dst2d, tile_expert, next_expert, n_used, w_gate, w_up, w_down):
    t = h2.shape[0] // TOKEN_TILE_ROWS
    n_slots = 2 * t
    steps, tm = dst2d.shape
    d, de = w_gate.shape[-2:]
    hbm = pl.BlockSpec(memory_space=pl.ANY)
    grid_spec = pltpu.PrefetchScalarGridSpec(
        num_scalar_prefetch=3,
        grid=(steps,),
        in_specs=[hbm] * 6,
        out_specs=hbm,
        scratch_shapes=[
            pltpu.SMEM((2, tm), jnp.int32),
            pltpu.SMEM((2, 2, tm), jnp.int32),
            pltpu.VMEM((MOE_BUFS * tm * TOKEN_TILE_ROWS, LANES), jnp.uint32),
            pltpu.VMEM((MOE_BUFS * tm * TOKEN_TILE_ROWS, LANES), jnp.uint32),
            pltpu.VMEM((d, de), F32),
            pltpu.VMEM((d, de), F32),
            pltpu.VMEM((de, d), F32),
            pltpu.VMEM((d, de), BF16),
            pltpu.VMEM((d, de), BF16),
            pltpu.VMEM((de, d), BF16),
            pltpu.SemaphoreType.DMA((1,)),
            pltpu.SemaphoreType.DMA((2, 2)),
            pltpu.SemaphoreType.DMA((MOE_BUFS,)),
            pltpu.SemaphoreType.DMA((MOE_BUFS,)),
            pltpu.SemaphoreType.DMA((3,)),
        ],
    )
    return pl.pallas_call(
        functools.partial(_moe_body, layer, n_slots),
        grid_spec=grid_spec,
        out_shape=jax.ShapeDtypeStruct(((n_slots + MOE_BUFS * tm) * TOKEN_TILE_ROWS, LANES), jnp.uint32),
        compiler_params=_cparams(("arbitrary",)),
        name="moe",
    )(tile_expert, next_expert, n_used, h2, src2d, dst2d, w_gate, w_up, w_down)


def _route_plan(ids, counts, t):
    tm = MOE_TM
    nt = (2 * t) // tm + N_EXPERTS
    cnt = counts[0, :N_EXPERTS]
    tiles = (cnt + tm - 1) // tm
    tile_end = jnp.cumsum(tiles)
    off = ((tile_end - tiles) * tm).astype(jnp.int32)
    tile_buf = np.arange(-1, nt) % MOE_BUFS
    dump = (2 * t + tile_buf[:, None] * tm + np.arange(tm)[None, :]).astype(np.int32)
    dst2d = _plan_scatter(off, ids[:, 0:4].reshape(-1), jnp.asarray(dump), t)
    slot = dst2d[1:]
    src_tok = jnp.where(slot < t, slot, jnp.where(slot < 2 * t, slot - t, 0))
    src2d = jnp.concatenate([src_tok, jnp.zeros((3, tm), jnp.int32)], axis=0) * TOKEN_TILE_ROWS
    n_used = tile_end[-1]
    steps = jnp.minimum(jnp.arange(nt + 1, dtype=jnp.int32), n_used - 1)
    tile_expert = jnp.sum(tile_end[None, :] <= steps[:, None], axis=1).astype(jnp.int32)
    eid = jnp.arange(N_EXPERTS, dtype=jnp.int32)
    later_used = (tiles[None, :] > 0) & (eid[None, :] > eid[:, None])
    nxt = jnp.min(jnp.where(later_used, eid[None, :], N_EXPERTS), axis=1)
    next_expert = jnp.where(nxt == N_EXPERTS, eid, nxt)[tile_expert].astype(jnp.int32)
    return src2d, dst2d * TOKEN_TILE_ROWS, tile_expert, next_expert, n_used.reshape(1).astype(jnp.int32)


PLAN_CHUNK = 8192


def _plan_body(t, off_ref, ids_hbm, init_hbm, out_hbm, tab, ids_a, ids_b, init_sem, ids_sem, out_sem):
    chunk = ids_a.shape[0]
    n_chunks = (4 * t) // chunk
    bufs = (ids_a, ids_b)

    def ids_copy(c):
        return pltpu.make_async_copy(ids_hbm.at[pl.ds(c * chunk, chunk)], bufs[c % 2], ids_sem.at[c % 2])

    init = pltpu.make_async_copy(init_hbm, tab, init_sem.at[0])
    init.start()
    ids_copy(0).start()
    init.wait()
    for c in range(n_chunks):
        ids_copy(c).wait()
        if c + 1 < n_chunks:
            ids_copy(c + 1).start()
        buf = bufs[c % 2]
        tok0 = c * (chunk // 4)

        def body(tt, carry, buf=buf, tok0=tok0):
            b = 4 * tt
            tok = tok0 + tt
            tab[off_ref[buf[b]] + buf[b + 2]] = tok
            tab[off_ref[buf[b + 1]] + buf[b + 3]] = tok + t
            return carry
        lax.fori_loop(0, chunk // 4, body, 0, unroll=8)
    out = pltpu.make_async_copy(tab, out_hbm, out_sem.at[0])
    out.start()
    out.wait()


def _plan_scatter(off, ids_flat, dump, t):
    rows, tm = dump.shape
    chunk = min(PLAN_CHUNK, 4 * t)
    assert (4 * t) % chunk == 0
    words = -(-rows * tm // 1024) * 1024
    init = jnp.concatenate([dump.reshape(-1), jnp.zeros((words - rows * tm,), jnp.int32)])
    hbm = pl.BlockSpec(memory_space=pl.ANY)
    grid_spec = pltpu.PrefetchScalarGridSpec(
        num_scalar_prefetch=1,
        grid=(1,),
        in_specs=[hbm, hbm],
        out_specs=hbm,
        scratch_shapes=[
            pltpu.SMEM((words,), jnp.int32),
            pltpu.SMEM((chunk,), jnp.int32),
            pltpu.SMEM((chunk,), jnp.int32),
            pltpu.SemaphoreType.DMA((1,)),
            pltpu.SemaphoreType.DMA((2,)),
            pltpu.SemaphoreType.DMA((1,)),
        ],
    )
    flat = pl.pallas_call(
        functools.partial(_plan_body, t),
        grid_spec=grid_spec,
        out_shape=jax.ShapeDtypeStruct((words,), jnp.int32),
        compiler_params=_cparams(("arbitrary",)),
        name="plan",
    )(off + tm, ids_flat, init)
    return flat[:rows * tm].reshape(rows, tm)


def _final_body(x_ref, y0_ref, y1_ref, wt_ref, pmod_ref, g_ref, o_ref):
    x = _combine(x_ref[...], y0_ref, y1_ref, wt_ref, pmod_ref[0, 5:6, :])
    ms = jnp.mean(x * x, axis=-1, keepdims=True)
    o_ref[...] = x * lax.rsqrt(ms + NORM_EPS) * g_ref[...]


def _final(x, y2, wts, pmod, g_final, seq):
    t, d = x.shape
    tm = FINAL_TM
    tiles_per_seq = seq // tm
    row = lambda i: (i, 0)
    return pl.pallas_call(
        _final_body,
        grid=(t // tm,),
        in_specs=[
            pl.BlockSpec((tm, d), row),
            pl.BlockSpec((tm * TOKEN_TILE_ROWS, LANES), row),
            pl.BlockSpec((tm * TOKEN_TILE_ROWS, LANES), lambda i: (i + t // tm, 0)),
            pl.BlockSpec((tm, LANES), row),
            pl.BlockSpec((1, N_MOD, d), lambda i: (i // tiles_per_seq, 0, 0)),
            pl.BlockSpec((1, d), lambda i: (0, 0)),
        ],
        out_specs=pl.BlockSpec((tm, d), row),
        out_shape=jax.ShapeDtypeStruct((t, d), F32),
        compiler_params=_cparams(("arbitrary",)),
        name="final",
    )(x, y2, y2, wts, pmod, g_final)


def _rotary_tables(positions):
    half = ROT_DIM // 2
    inv_freq = ROPE_THETA ** (-(jnp.arange(0, ROT_DIM, 2, dtype=F32) / ROT_DIM))
    ang = positions.reshape(-1).astype(F32)[:, None] * inv_freq
    cos, sin = jnp.cos(ang), jnp.sin(ang)
    t = ang.shape[0]
    pad = HEAD_DIM - ROT_DIM
    cos_h = jnp.concatenate([cos, cos, jnp.ones((t, pad), F32)], axis=1)
    sa_h = jnp.concatenate([-sin, jnp.zeros((t, HEAD_DIM - half), F32)], axis=1)
    sb_h = jnp.concatenate([jnp.zeros((t, half), F32), sin, jnp.zeros((t, pad), F32)], axis=1)
    rep = LANES // HEAD_DIM
    return jnp.tile(cos_h, (1, rep)), jnp.tile(sa_h, (1, rep)), jnp.tile(sb_h, (1, rep))


def kernel(x, c, positions, w_ada, b_ada, g_norm1, w_in, b_in, sinks, ln_v_g, ln_v_b, w_spatial, b_spatial,
           g_attn_out, g_gmlp_out, w_out, g_norm2, w_router_group, b_router_group, w_router_expert,
           b_router_expert, w_e_gate, w_e_up, w_e_down, g_final):
    batch, seq, d = x.shape
    depth = w_ada.shape[0]
    t = batch * seq
    assert d == D_MODEL and seq % max(PROJ_TM, OUT_TM, FINAL_TM, MIX_BLOCKS * BLOCK) == 0 and (2 * t) % MOE_TM == 0

    c_pad = jnp.zeros((SUBLANES, d), F32).at[:batch].set(c)
    mod = _adaln(c_pad, w_ada, b_ada).reshape(depth, SUBLANES, N_MOD, d)
    cos_t, sa_t, sb_t = _rotary_tables(positions)

    n_route = N_GROUPS + N_EXPERTS
    w_router = jnp.concatenate(
        [w_router_group, w_router_expert, jnp.zeros((depth, d, LANES - n_route), F32)], axis=-1)
    b_router = jnp.concatenate(
        [b_router_group, b_router_expert, jnp.zeros((depth, LANES - n_route), F32)], axis=-1)

    xt = x.reshape(t, d)
    prev = None
    for l in range(depth):
        xt, qkv, z = _proj(xt, prev, mod[l], g_norm1[l][None], w_in[l].astype(BF16), b_in[l][None],
                           cos_t, sa_t, sb_t, seq)
        merged = _mixer(qkv, z, sinks[l], w_spatial[l], b_spatial[l].T, ln_v_g[l][None], ln_v_b[l][None],
                        g_attn_out[l][None], g_gmlp_out[l][None], batch, seq)
        xt, h2, ids, wts, counts = _outproj(merged, xt, mod[l], g_norm2[l][None], w_out[l].astype(BF16),
                                            w_router[l].astype(BF16), b_router[l][None], seq)
        src2d, dst2d, tile_expert, next_expert, n_used = _route_plan(ids, counts, t)
        y2 = _moe(l, h2, src2d, dst2d, tile_expert, next_expert, n_used, w_e_gate, w_e_up, w_e_down)
        prev = (y2, wts, mod[l])
    out = _final(xt, prev[0], prev[1], prev[2], g_final[None], seq)
    return out.reshape(batch, seq, d)
```

```python
import functools

import jax
import jax.numpy as jnp
import numpy as np
from jax import lax
from jax.experimental import pallas as pl
from jax.experimental.pallas import tpu as pltpu

F32 = jnp.float32
BF16 = jnp.bfloat16

D_MODEL = 2048
ATTN_WIDTH = D_MODEL // 2
HEAD_DIM = 64
N_Q_HEADS = ATTN_WIDTH // HEAD_DIM
N_KV_HEADS = 2
GQA_GROUP = N_Q_HEADS // N_KV_HEADS
WINDOW = 128
BLOCK = 128
ROT_DIM = HEAD_DIM // 4
ROPE_THETA = 500000.0
GMLP_WIDTH = D_MODEL - ATTN_WIDTH
GMLP_GROUPS = 8
GMLP_GROUP_DIM = GMLP_WIDTH // GMLP_GROUPS
Q_COLS = N_Q_HEADS * HEAD_DIM
KV_COLS = N_KV_HEADS * HEAD_DIM
QKV_COLS = Q_COLS + 2 * KV_COLS
IN_COLS = QKV_COLS + 2 * GMLP_WIDTH
N_GROUPS = 4
EXPERTS_PER_GROUP = 8
N_EXPERTS = N_GROUPS * EXPERTS_PER_GROUP
D_EXPERT = D_MODEL // 4
NORM_EPS = 1e-6
N_MOD = 6

LANES = 128
SUBLANES = 8
VMEM_LIMIT_BYTES = 56 * 1024 * 1024
TOKEN_TILE_ROWS = D_MODEL // 2 // LANES

ADA_TN = 1024
PROJ_TM = 256
OUT_TM = 512
MIX_BLOCKS = 4
MOE_TM = 256
MOE_N_CHUNK = 256
MOE_BUFS = 3
FINAL_TM = 512


def _cparams(sem):
    return pltpu.CompilerParams(dimension_semantics=sem, vmem_limit_bytes=VMEM_LIMIT_BYTES)


def _adaln_body(c_ref, w_ref, b_ref, o_ref):
    c = c_ref[...]
    ca = (c * jax.nn.sigmoid(c)).astype(BF16)
    w = w_ref[0].astype(BF16)
    o_ref[0] = jnp.dot(ca, w, preferred_element_type=F32) + b_ref[0]


def _adaln(c_pad, w_ada, b_ada):
    depth, d, n = w_ada.shape
    rows = c_pad.shape[0]
    return pl.pallas_call(
        _adaln_body,
        grid=(depth, n // ADA_TN),
        in_specs=[
            pl.BlockSpec((rows, d), lambda l, j: (0, 0)),
            pl.BlockSpec((1, d, ADA_TN), lambda l, j: (l, 0, j)),
            pl.BlockSpec((1, 1, ADA_TN), lambda l, j: (l, 0, j)),
        ],
        out_specs=pl.BlockSpec((1, rows, ADA_TN), lambda l, j: (l, 0, j)),
        out_shape=jax.ShapeDtypeStruct((depth, rows, n), F32),
        compiler_params=_cparams(("arbitrary", "arbitrary")),
        name="adaln",
    )(c_pad, w_ada, b_ada.reshape(depth, 1, n))


def _rms_mod(x, g, sc, sh):
    ms = jnp.mean(x * x, axis=-1, keepdims=True)
    return (x * lax.rsqrt(ms + NORM_EPS) * g) * (1.0 + sc) + sh


def _pack_bf16_halves(h):
    half = h.shape[1] // 2
    return _pack_bf16_pair(h[:, :half], h[:, half:])


def _pack_bf16_pair(lo, hi):
    lo = lax.bitcast_convert_type(lo.astype(BF16).astype(F32), jnp.uint32)
    hi = lax.bitcast_convert_type(hi.astype(BF16).astype(F32), jnp.uint32)
    return lax.shift_right_logical(lo, jnp.uint32(16)) | (hi & jnp.uint32(0xFFFF0000))


def _load_token_tiles(ref, base, n):
    return jnp.concatenate(
        [ref[pl.ds(base + c, n, stride=TOKEN_TILE_ROWS), :] for c in range(TOKEN_TILE_ROWS)], axis=1)


def _unpack_bf16_halves(p):
    lo = lax.bitcast_convert_type(lax.shift_left(p, jnp.uint32(16)), F32).astype(BF16)
    hi = lax.bitcast_convert_type(p & jnp.uint32(0xFFFF0000), F32).astype(BF16)
    return jnp.concatenate([lo, hi], axis=1)


def _combine(x, y0_ref, y1_ref, wt_ref, gate):
    wt = wt_ref[...]
    n = x.shape[0]
    y0 = _unpack_bf16_halves(_load_token_tiles(y0_ref, 0, n)).astype(F32)
    y1 = _unpack_bf16_halves(_load_token_tiles(y1_ref, 0, n)).astype(F32)
    y = wt[:, 0:1] * y0 + wt[:, 1:2] * y1
    return x + (1.0 + gate) * y


def _gelu_exact(z):
    return 0.5 * z * (1.0 + lax.erf(z * (2.0 ** -0.5)))


def _proj_body(has_prev, *refs):
    if has_prev:
        (x_ref, y0_ref, y1_ref, wt_ref, pmod_ref, mod_ref, g_ref, w_ref, b_ref, cos_ref, sa_ref, sb_ref,
         xo_ref, qkv_ref, z_ref) = refs
        x = _combine(x_ref[...], y0_ref, y1_ref, wt_ref, pmod_ref[0, 5:6, :])
        xo_ref[...] = x
    else:
        x_ref, mod_ref, g_ref, w_ref, b_ref, cos_ref, sa_ref, sb_ref, qkv_ref, z_ref = refs
        x = x_ref[...]
    mod = mod_ref[0]
    h = _rms_mod(x, g_ref[...], mod[1:2, :], mod[0:1, :]).astype(BF16)

    qkv = jnp.dot(h, w_ref[:, :QKV_COLS], preferred_element_type=F32) + b_ref[:, :QKV_COLS]
    cos, sa, sb = cos_ref[...], sa_ref[...], sb_ref[...]
    n_rot = (Q_COLS + KV_COLS) // LANES
    for j in range(QKV_COLS // LANES):
        ch = qkv[:, j * LANES:(j + 1) * LANES]
        if j < n_rot:
            ch = ch * cos + pltpu.roll(ch, LANES - ROT_DIM // 2, 1) * sa + pltpu.roll(ch, ROT_DIM // 2, 1) * sb
        if j < Q_COLS // LANES:
            ch = ch * (HEAD_DIM ** -0.5)
        qkv_ref[:, j * LANES:(j + 1) * LANES] = ch.astype(BF16)

    z = jnp.dot(h, w_ref[:, QKV_COLS:], preferred_element_type=F32) + b_ref[:, QKV_COLS:]
    z_ref[...] = _gelu_exact(z)


def _proj(x, prev, mod_l, g1, w_in, b_in, cos_t, sa_t, sb_t, seq):
    t, d = x.shape
    tm = PROJ_TM
    tiles_per_seq = seq // tm
    row = lambda i: (i, 0)
    const = lambda i: (0, 0)
    modmap = lambda i: (i // tiles_per_seq, 0, 0)
    in_specs = [pl.BlockSpec((tm, d), row)]
    args = [x]
    if prev is not None:
        y2, wts, pmod = prev
        in_specs += [pl.BlockSpec((tm * TOKEN_TILE_ROWS, LANES), row),
                     pl.BlockSpec((tm * TOKEN_TILE_ROWS, LANES), lambda i: (i + t // tm, 0)),
                     pl.BlockSpec((tm, LANES), row), pl.BlockSpec((1, N_MOD, d), modmap)]
        args += [y2, y2, wts, pmod]
    in_specs += [
        pl.BlockSpec((1, N_MOD, d), modmap),
        pl.BlockSpec((1, d), const),
        pl.BlockSpec((d, IN_COLS), const, pipeline_mode=pl.Buffered(1)),
        pl.BlockSpec((1, IN_COLS), const),
        pl.BlockSpec((tm, LANES), row),
        pl.BlockSpec((tm, LANES), row),
        pl.BlockSpec((tm, LANES), row),
    ]
    args += [mod_l, g1, w_in, b_in, cos_t, sa_t, sb_t]
    out_specs = [pl.BlockSpec((tm, QKV_COLS), row), pl.BlockSpec((tm, 2 * GMLP_WIDTH), row)]
    out_shape = [jax.ShapeDtypeStruct((t, QKV_COLS), BF16), jax.ShapeDtypeStruct((t, 2 * GMLP_WIDTH), F32)]
    if prev is not None:
        out_specs = [pl.BlockSpec((tm, d), row)] + out_specs
        out_shape = [jax.ShapeDtypeStruct((t, d), F32)] + out_shape
    outs = pl.pallas_call(
        functools.partial(_proj_body, prev is not None),
        grid=(t // tm,),
        in_specs=in_specs,
        out_specs=out_specs,
        out_shape=out_shape,
        compiler_params=_cparams(("arbitrary",)),
        name="proj",
    )(*args)
    if prev is None:
        return (x,) + tuple(outs)
    return tuple(outs)


def _mixer_body(sink_ref, q_ref, kvc_ref, kvp_ref, z_ref, ws_ref, bst_ref, lng_ref, lnb_ref, ga_ref, gg_ref,
                o_ref):
    n = pl.program_id(1)
    for sub in range(MIX_BLOCKS):
        r0 = sub * BLOCK
        kvp = kvp_ref[...] if sub == 0 else kvc_ref[r0 - BLOCK:r0, :]
        has_prev = (n > 0) if sub == 0 else None
        _mixer_block(sink_ref, q_ref[r0:r0 + BLOCK, :], kvc_ref[r0:r0 + BLOCK, :], kvp, has_prev, z_ref, r0,
                     ws_ref, bst_ref, lng_ref, lnb_ref, ga_ref, gg_ref, o_ref)


def _mixer_block(sink_ref, q, kvc, kvp, has_prev, z_ref, r0, ws_ref, bst_ref, lng_ref, lnb_ref, ga_ref, gg_ref,
                 o_ref):
    rows = slice(r0, r0 + BLOCK)
    kcat = jnp.concatenate([kvp[:, :KV_COLS], kvc[:, :KV_COLS]], axis=0)
    vcat = jnp.concatenate([kvp[:, KV_COLS:], kvc[:, KV_COLS:]], axis=0)

    lane = lax.broadcasted_iota(jnp.int32, (BLOCK, LANES), 1)
    lo = lane < HEAD_DIM
    kj = lax.broadcasted_iota(jnp.int32, (2 * BLOCK, BLOCK), 0)
    qi = lax.broadcasted_iota(jnp.int32, (2 * BLOCK, BLOCK), 1)
    rel = qi + BLOCK - kj
    valid = (rel >= 0) & (rel < WINDOW)
    if has_prev is not None:
        valid = valid & (has_prev | (kj >= BLOCK))
    row_lo = lax.broadcasted_iota(jnp.int32, (LANES, BLOCK), 0) < HEAD_DIM
    zero_q = jnp.zeros((BLOCK, LANES), BF16)
    nt_dims = (((1,), (1,)), ((), ()))

    attn = []
    chunks_per_kv = GQA_GROUP * HEAD_DIM // LANES
    for hk in range(N_KV_HEADS):
        ksl = kcat[:, hk * HEAD_DIM:(hk + 1) * HEAD_DIM]
        vsl = vcat[:, hk * HEAD_DIM:(hk + 1) * HEAD_DIM]
        kk = jnp.concatenate([ksl, ksl], axis=1)
        vt = jnp.concatenate([vsl, vsl], axis=1).astype(F32).T.astype(BF16)
        for j in range(chunks_per_kv):
            c = hk * chunks_per_kv + j
            qc = q[:, c * LANES:(c + 1) * LANES]
            probs = []
            for half in range(2):
                qm = jnp.where(lo, qc, zero_q) if half == 0 else jnp.where(lo, zero_q, qc)
                s = lax.dot_general(kk, qm, nt_dims, preferred_element_type=F32)
                s = jnp.where(valid, s, -jnp.inf)
                sink = sink_ref[2 * c + half]
                m = jnp.maximum(jnp.max(s, axis=0, keepdims=True), sink)
                p = jnp.exp(s - m)
                denom = jnp.sum(p, axis=0, keepdims=True) + jnp.exp(sink - m)
                probs.append((p * (1.0 / denom)).astype(BF16))
            pn = jnp.concatenate(probs, axis=1)
            ot = jnp.dot(vt, pn, preferred_element_type=F32)
            sel = jnp.where(row_lo, ot[:, :BLOCK], ot[:, BLOCK:])
            attn.append(sel.T)

    ss = attn[0] * attn[0]
    for a in attn[1:]:
        ss = ss + a * a
    scale = lax.rsqrt(jnp.sum(ss, axis=-1, keepdims=True) * (1.0 / ATTN_WIDTH) + NORM_EPS)
    for c, a in enumerate(attn):
        o_ref[rows, c * LANES:(c + 1) * LANES] = (a * scale * ga_ref[:, c * LANES:(c + 1) * LANES]).astype(BF16)

    u = z_ref[rows, :GMLP_WIDTH]
    v = z_ref[rows, GMLP_WIDTH:]
    mu = jnp.mean(v, axis=-1, keepdims=True)
    vc = v - mu
    var = jnp.mean(vc * vc, axis=-1, keepdims=True)
    vn = ((vc * lax.rsqrt(var + NORM_EPS)) * lng_ref[...] + lnb_ref[...]).astype(BF16)
    ti = lax.broadcasted_iota(jnp.int32, (BLOCK, BLOCK), 0)
    si = lax.broadcasted_iota(jnp.int32, (BLOCK, BLOCK), 1)
    tril = si <= ti
    bst = bst_ref[...]
    gm = []
    for h in range(GMLP_GROUPS):
        w = jnp.where(tril, ws_ref[h], 0.0).astype(BF16)
        sp = jnp.dot(w, vn[:, h * LANES:(h + 1) * LANES], preferred_element_type=F32) + bst[:, h:h + 1]
        gm.append(u[:, h * LANES:(h + 1) * LANES] * sp)
    ss = gm[0] * gm[0]
    for a in gm[1:]:
        ss = ss + a * a
    scale = lax.rsqrt(jnp.sum(ss, axis=-1, keepdims=True) * (1.0 / GMLP_WIDTH) + NORM_EPS)
    for c, a in enumerate(gm):
        o_ref[rows, ATTN_WIDTH + c * LANES:ATTN_WIDTH + (c + 1) * LANES] = (
            a * scale * gg_ref[:, c * LANES:(c + 1) * LANES]).astype(BF16)


def _mixer(qkv, z, sinks, w_sp, bst, ln_g, ln_b, g_attn, g_gmlp, batch, seq):
    t = qkv.shape[0]
    nb = seq // BLOCK
    steps = nb // MIX_BLOCKS
    rows = MIX_BLOCKS * BLOCK
    kv_blk = Q_COLS // (2 * KV_COLS)
    cur = lambda b, n, s: (b * steps + n, 0)
    const2 = lambda b, n, s: (0, 0)
    grid_spec = pltpu.PrefetchScalarGridSpec(
        num_scalar_prefetch=1,
        grid=(batch, steps),
        in_specs=[
            pl.BlockSpec((rows, Q_COLS), cur),
            pl.BlockSpec((rows, 2 * KV_COLS), lambda b, n, s: (b * steps + n, kv_blk)),
            pl.BlockSpec((BLOCK, 2 * KV_COLS),
                         lambda b, n, s: (b * nb + jnp.maximum(MIX_BLOCKS * n - 1, 0), kv_blk)),
            pl.BlockSpec((rows, 2 * GMLP_WIDTH), cur),
            pl.BlockSpec((GMLP_GROUPS, BLOCK, BLOCK), lambda b, n, s: (0, 0, 0)),
            pl.BlockSpec((BLOCK, GMLP_GROUPS), const2),
            pl.BlockSpec((1, GMLP_WIDTH), const2),
            pl.BlockSpec((1, GMLP_WIDTH), const2),
            pl.BlockSpec((1, ATTN_WIDTH), const2),
            pl.BlockSpec((1, GMLP_WIDTH), const2),
        ],
        out_specs=pl.BlockSpec((rows, D_MODEL), cur),
    )
    return pl.pallas_call(
        _mixer_body,
        grid_spec=grid_spec,
        out_shape=jax.ShapeDtypeStruct((t, D_MODEL), BF16),
        compiler_params=_cparams(("arbitrary", "arbitrary")),
        name="mixer",
    )(sinks, qkv, qkv, qkv, z, w_sp, bst, ln_g, ln_b, g_attn, g_gmlp)


def _first_index_of_max(vals, lane, width):
    mx = jnp.max(vals, axis=-1, keepdims=True)
    idx = jnp.min(jnp.where(vals == mx, lane, width), axis=-1, keepdims=True)
    return mx, idx


def _outproj_body(m_ref, x_ref, mod_ref, g_ref, w_ref, wr_ref, br_ref,
                  xo_ref, h_ref, ids_ref, wt_ref, cnt_ref, carry_ref):
    i = pl.program_id(0)
    tm = x_ref.shape[0]

    @pl.when(i == 0)
    def _():
        carry_ref[...] = jnp.zeros_like(carry_ref)

    mod = mod_ref[0]
    y = jnp.dot(m_ref[...], w_ref[...], preferred_element_type=F32)
    x = x_ref[...] + (1.0 + mod[2:3, :]) * y
    xo_ref[...] = x
    h = _rms_mod(x, g_ref[...], mod[4:5, :], mod[3:4, :])
    packed = _pack_bf16_halves(h)
    for c in range(TOKEN_TILE_ROWS):
        h_ref[pl.ds(c, tm, stride=TOKEN_TILE_ROWS), :] = packed[:, c * LANES:(c + 1) * LANES]

    logits = jnp.dot(h.astype(BF16), wr_ref[...], preferred_element_type=F32) + br_ref[...]
    lane = lax.broadcasted_iota(jnp.int32, (tm, LANES), 1)
    neg = -jnp.inf
    gl = jnp.where(lane < N_GROUPS, logits, neg)
    ge = jnp.exp(gl - jnp.max(gl, axis=-1, keepdims=True))
    gprob = ge / jnp.sum(ge, axis=-1, keepdims=True)
    g_p, g_idx = _first_index_of_max(gprob, lane, LANES)

    base = N_GROUPS + EXPERTS_PER_GROUP * g_idx
    in_grp = (lane >= base) & (lane < base + EXPERTS_PER_GROUP)
    el = jnp.where(in_grp, logits, neg)
    ee = jnp.exp(el - jnp.max(el, axis=-1, keepdims=True))
    eprob = jnp.where(in_grp, ee / jnp.sum(ee, axis=-1, keepdims=True), -1.0)
    p1, i1 = _first_index_of_max(eprob, lane, LANES)
    eprob2 = jnp.where(lane == i1, -1.0, eprob)
    p2, i2 = _first_index_of_max(eprob2, lane, LANES)
    psum = p1 + p2
    w1 = g_p * (p1 / psum)
    w2 = g_p * (p2 / psum)
    e1 = i1 - N_GROUPS
    e2 = i2 - N_GROUPS

    oh1 = lane == e1
    oh2 = lane == e2
    onehot = jnp.where(oh1 | oh2, 1.0, 0.0)
    ri = lax.broadcasted_iota(jnp.int32, (tm, tm), 0)
    ci = lax.broadcasted_iota(jnp.int32, (tm, tm), 1)
    lower = jnp.where(ci < ri, 1.0, 0.0).astype(BF16)
    before = jnp.dot(lower, onehot.astype(BF16), preferred_element_type=F32) + carry_ref[...]
    r1 = jnp.sum(jnp.where(oh1, before, 0.0), axis=-1, keepdims=True)
    r2 = jnp.sum(jnp.where(oh2, before, 0.0), axis=-1, keepdims=True)
    carry = carry_ref[...] + jnp.sum(onehot, axis=0, keepdims=True)
    carry_ref[...] = carry
    cnt_ref[...] = jnp.broadcast_to(carry, cnt_ref.shape).astype(jnp.int32)

    ids = jnp.where(lane == 0, e1, jnp.where(lane == 1, e2, 0))
    ranks = jnp.where(lane == 0, r1, jnp.where(lane == 1, r2, 0.0)).astype(jnp.int32)
    ids_ref[...] = lax.shift_left(ids, ROUTE_SHIFT) | ranks
    wt_ref[...] = jnp.where(lane == 0, w1, jnp.where(lane == 1, w2, 0.0))


def _outproj(merged, x, mod_l, g2, w_out, w_router, b_router, seq):
    t, d = x.shape
    tm = OUT_TM
    tiles_per_seq = seq // tm
    row = lambda i: (i, 0)
    const = lambda i: (0, 0)
    return pl.pallas_call(
        _outproj_body,
        grid=(t // tm,),
        in_specs=[
            pl.BlockSpec((tm, d), row),
            pl.BlockSpec((tm, d), row),
            pl.BlockSpec((1, N_MOD, d), lambda i: (i // tiles_per_seq, 0, 0)),
            pl.BlockSpec((1, d), const),
            pl.BlockSpec((d, d), const, pipeline_mode=pl.Buffered(1)),
            pl.BlockSpec((d, LANES), const),
            pl.BlockSpec((1, LANES), const),
        ],
        out_specs=[
            pl.BlockSpec((tm, d), row),
            pl.BlockSpec((tm * TOKEN_TILE_ROWS, LANES), row),
            pl.BlockSpec((tm, LANES), row),
            pl.BlockSpec((tm, LANES), row),
            pl.BlockSpec((SUBLANES, LANES), const),
        ],
        out_shape=[
            jax.ShapeDtypeStruct((t, d), F32),
            jax.ShapeDtypeStruct((t * TOKEN_TILE_ROWS, LANES), jnp.uint32),
            jax.ShapeDtypeStruct((t, LANES), jnp.int32),
            jax.ShapeDtypeStruct((t, LANES), F32),
            jax.ShapeDtypeStruct((SUBLANES, LANES), jnp.int32),
        ],
        scratch_shapes=[pltpu.VMEM((1, LANES), F32)],
        compiler_params=_cparams(("arbitrary",)),
        name="outproj",
    )(merged, x, mod_l, g2, w_out, w_router, b_router)


def _moe_body(layer, n_slots, te_ref, nx_ref, nu_ref, h_hbm, src_hbm, dst_hbm, wg_hbm, wu_hbm, wd_hbm, y_hbm,
              first_smem, plan_smem, xbuf, obuf, wgs, wus, wds, wgb, wub, wdb,
              first_sem, plan_sem, g_sem, s_sem, w_sem):
    i = pl.program_id(0)
    n_used = nu_ref[0]
    nb = MOE_BUFS
    tm = plan_smem.shape[2]
    d, de = wgb.shape

    def weight_copies(e):
        return (pltpu.make_async_copy(wg_hbm.at[layer, e], wgs, w_sem.at[0]),
                pltpu.make_async_copy(wu_hbm.at[layer, e], wus, w_sem.at[1]),
                pltpu.make_async_copy(wd_hbm.at[layer, e], wds, w_sem.at[2]))
    b0 = i % nb
    b1 = (i + 1) % nb
    b2 = (i + 2) % nb
    p0 = i % 2
    p1 = 1 - p0

    class _PlanCopy:
        def __init__(self, k, slot):
            self.copies = (
                pltpu.make_async_copy(src_hbm.at[k + 2], plan_smem.at[slot, 0], plan_sem.at[slot, 0]),
                pltpu.make_async_copy(dst_hbm.at[k], plan_smem.at[slot, 1], plan_sem.at[slot, 1]))

        def start(self):
            for cp in self.copies:
                cp.start()

        def wait(self):
            for cp in self.copies:
                cp.wait()

    plan_copy = _PlanCopy

    xrows = tm * TOKEN_TILE_ROWS

    def token_tile(row):
        return pl.ds(pl.multiple_of(row, TOKEN_TILE_ROWS), TOKEN_TILE_ROWS)

    def tile_buf(buf):
        return pl.ds(pl.multiple_of(buf * xrows, xrows), xrows)

    def gather_row(src, buf, j):
        return pltpu.make_async_copy(h_hbm.at[token_tile(src), :],
                                     xbuf.at[token_tile(buf * xrows + j * TOKEN_TILE_ROWS), :], g_sem.at[buf])

    def scatter_row(dst, buf, j):
        return pltpu.make_async_copy(obuf.at[token_tile(buf * xrows + j * TOKEN_TILE_ROWS), :],
                                     y_hbm.at[token_tile(dst), :], s_sem.at[buf])

    def wait_gather(buf):
        pltpu.make_async_copy(h_hbm.at[pl.ds(0, xrows), :], xbuf.at[tile_buf(buf), :], g_sem.at[buf]).wait()

    def wait_scatter(buf):
        pltpu.make_async_copy(obuf.at[tile_buf(buf), :], y_hbm.at[pl.ds(0, xrows), :], s_sem.at[buf]).wait()

    @pl.when(i == 0)
    def _():
        first = pltpu.make_async_copy(src_hbm.at[pl.ds(0, 2)], first_smem, first_sem.at[0])
        first.start()
        plan_copy(0, 0).start()
        for cp in weight_copies(te_ref[0]):
            cp.start()
        first.wait()
        for tile in range(2):
            def body(j, carry, tile=tile):
                gather_row(first_smem[tile, j], tile, j).start()
                return carry
            lax.fori_loop(0, tm, body, 0, unroll=8)
        obuf[pl.ds((nb - 1) * xrows, xrows), :] = jnp.zeros((xrows, LANES), jnp.uint32)

    @pl.when(i < n_used)
    def _():
        plan_copy(i, p0).wait()
        plan_copy(i + 1, p1).start()
        wait_gather(b0)

        @pl.when(i >= 2)
        def _():
            wait_scatter(b0)

        new_expert = jnp.logical_or(i == 0, te_ref[i] != te_ref[jnp.maximum(i - 1, 0)])

        @pl.when(new_expert)
        def _():
            for cp in weight_copies(te_ref[i]):
                cp.wait()
            wgb[...] = wgs[...].astype(BF16)
            wub[...] = wus[...].astype(BF16)
            wdb[...] = wds[...].astype(BF16)

            @pl.when(nx_ref[i] != te_ref[i])
            def _():
                for cp in weight_copies(nx_ref[i]):
                    cp.start()

        copies = []
        for j in range(tm):
            copies += [(0, j), (1, j)]
        n_up = de // MOE_N_CHUNK
        n_down = d // MOE_N_CHUNK
        n_pieces = 2 * n_up + n_down
        per_piece = -(-len(copies) // n_pieces)

        def issue(piece):
            for kind, j in copies[piece * per_piece:(piece + 1) * per_piece]:
                if kind == 0:
                    gather_row(plan_smem[p0, 0, j], b2, j).start(priority=1)
                else:
                    scatter_row(plan_smem[p0, 1, j], b2, j).start()

        xbase = pl.multiple_of(b0 * xrows, xrows)
        xb = _unpack_bf16_halves(_load_token_tiles(xbuf, xbase, tm))
        piece = 0
        acts = []
        for n in range(n_up):
            cols = slice(n * MOE_N_CHUNK, (n + 1) * MOE_N_CHUNK)
            g = jnp.dot(xb, wgb[:, cols], preferred_element_type=F32)
            issue(piece)
            u = jnp.dot(xb, wub[:, cols], preferred_element_type=F32)
            issue(piece + 1)
            piece += 2
            acts.append(((g * jax.nn.sigmoid(g)) * u).astype(BF16))
        a = jnp.concatenate(acts, axis=1)
        for n in range(n_down // 2):
            cols_lo = slice(n * MOE_N_CHUNK, (n + 1) * MOE_N_CHUNK)
            cols_hi = slice(d // 2 + n * MOE_N_CHUNK, d // 2 + (n + 1) * MOE_N_CHUNK)
            y_lo = jnp.dot(a, wdb[:, cols_lo], preferred_element_type=F32)
            issue(piece)
            y_hi = jnp.dot(a, wdb[:, cols_hi], preferred_element_type=F32)
            issue(piece + 1)
            piece += 2
            packed = _pack_bf16_pair(y_lo, y_hi)
            for c in range(MOE_N_CHUNK // LANES):
                obuf[pl.ds(xbase + n * (MOE_N_CHUNK // LANES) + c, tm, stride=TOKEN_TILE_ROWS), :] = (
                    packed[:, c * LANES:(c + 1) * LANES])

    @pl.when(i == n_used)
    def _():
        plan_copy(i, p0).wait()
        wait_gather(b0)
        wait_gather(b1)
        wait_scatter(b1)

        @pl.when(i >= 2)
        def _():
            wait_scatter(b0)

        def body(j, carry):
            scatter_row(plan_smem[p0, 1, j], b2, j).start()
            return carry
        lax.fori_loop(0, tm, body, 0, unroll=8)
        wait_scatter(b2)
        obuf[tile_buf(b0), :] = jnp.zeros((xrows, LANES), jnp.uint32)
        for k in range(nb):
            fill = pltpu.make_async_copy(obuf.at[tile_buf(b0), :],
                                         y_hbm.at[pl.ds((n_slots + k * tm) * TOKEN_TILE_ROWS, xrows), :], s_sem.at[b0])
            fill.start()
            fill.wait()


def _moe(layer, h2, src2d, dst2d, tile_expert, next_expert, n_used, w_gate, w_up, w_down):
    t = h2.shape[0] // TOKEN_TILE_ROWS
    n_slots = 2 * t
    steps, tm = dst2d.shape
    d, de = w_gate.shape[-2:]
    hbm = pl.BlockSpec(memory_space=pl.ANY)
    grid_spec = pltpu.PrefetchScalarGridSpec(
        num_scalar_prefetch=3,
        grid=(steps,),
        in_specs=[hbm] * 6,
        out_specs=hbm,
        scratch_shapes=[
            pltpu.SMEM((2, tm), jnp.int32),
            pltpu.SMEM((2, 2, tm), jnp.int32),
            pltpu.VMEM((MOE_BUFS * tm * TOKEN_TILE_ROWS, LANES), jnp.uint32),
            pltpu.VMEM((MOE_BUFS * tm * TOKEN_TILE_ROWS, LANES), jnp.uint32),
            pltpu.VMEM((d, de), F32),
            pltpu.VMEM((d, de), F32),
            pltpu.VMEM((de, d), F32),
            pltpu.VMEM((d, de), BF16),
            pltpu.VMEM((d, de), BF16),
            pltpu.VMEM((de, d), BF16),
            pltpu.SemaphoreType.DMA((1,)),
            pltpu.SemaphoreType.DMA((2, 2)),
            pltpu.SemaphoreType.DMA((MOE_BUFS,)),
            pltpu.SemaphoreType.DMA((MOE_BUFS,)),
            pltpu.SemaphoreType.DMA((3,)),
        ],
    )
    return pl.pallas_call(
        functools.partial(_moe_body, layer, n_slots),
        grid_spec=grid_spec,
        out_shape=jax.ShapeDtypeStruct(((n_slots + MOE_BUFS * tm) * TOKEN_TILE_ROWS, LANES), jnp.uint32),
        compiler_params=_cparams(("arbitrary",)),
        name="moe",
    )(tile_expert, next_expert, n_used, h2, src2d, dst2d, w_gate, w_up, w_down)


def _route_plan(ids, counts, t):
    tm = MOE_TM
    nt = (2 * t) // tm + N_EXPERTS
    cnt = counts[0, :N_EXPERTS]
    tiles = (cnt + tm - 1) // tm
    tile_end = jnp.cumsum(tiles)
    off = ((tile_end - tiles) * tm).astype(jnp.int32)
    tile_buf = np.arange(-1, nt) % MOE_BUFS
    dump = (2 * t + tile_buf[:, None] * tm + np.arange(tm)[None, :]).astype(np.int32)
    dst2d = _plan_scatter(off, ids[:, 0:2].reshape(-1), jnp.asarray(dump), t)
    slot = dst2d[1:]
    src_tok = jnp.where(slot < t, slot, jnp.where(slot < 2 * t, slot - t, 0))
    src2d = jnp.concatenate([src_tok, jnp.zeros((3, tm), jnp.int32)], axis=0) * TOKEN_TILE_ROWS
    n_used = tile_end[-1]
    steps = jnp.minimum(jnp.arange(nt + 1, dtype=jnp.int32), n_used - 1)
    tile_expert = jnp.sum(tile_end[None, :] <= steps[:, None], axis=1).astype(jnp.int32)
    eid = jnp.arange(N_EXPERTS, dtype=jnp.int32)
    later_used = (tiles[None, :] > 0) & (eid[None, :] > eid[:, None])
    nxt = jnp.min(jnp.where(later_used, eid[None, :], N_EXPERTS), axis=1)
    next_expert = jnp.where(nxt == N_EXPERTS, eid, nxt)[tile_expert].astype(jnp.int32)
    return src2d, dst2d * TOKEN_TILE_ROWS, tile_expert, next_expert, n_used.reshape(1).astype(jnp.int32)


PLAN_CHUNK = 8192
ROUTE_SHIFT = 20


def _plan_body(t, off_ref, ids_hbm, init_hbm, out_hbm, tab, ids_a, ids_b, init_sem, ids_sem, out_sem):
    chunk = ids_a.shape[0]
    n_chunks = (2 * t) // chunk
    bufs = (ids_a, ids_b)

    def ids_copy(c):
        return pltpu.make_async_copy(ids_hbm.at[pl.ds(c * chunk, chunk)], bufs[c % 2], ids_sem.at[c % 2])

    init = pltpu.make_async_copy(init_hbm, tab, init_sem.at[0])
    init.start()
    ids_copy(0).start()
    init.wait()
    for c in range(n_chunks):
        ids_copy(c).wait()
        if c + 1 < n_chunks:
            ids_copy(c + 1).start()
        buf = bufs[c % 2]
        tok0 = c * (chunk // 2)

        def body(tt, carry, buf=buf, tok0=tok0):
            tok = tok0 + tt
            for k in range(2):
                word = buf[2 * tt + k]
                expert = lax.shift_right_logical(word, ROUTE_SHIFT)
                rank = word & ((1 << ROUTE_SHIFT) - 1)
                tab[off_ref[expert] + rank] = tok + k * t
            return carry
        lax.fori_loop(0, chunk // 2, body, 0, unroll=8)
    out = pltpu.make_async_copy(tab, out_hbm, out_sem.at[0])
    out.start()
    out.wait()


def _plan_scatter(off, ids_flat, dump, t):
    rows, tm = dump.shape
    chunk = min(PLAN_CHUNK, 2 * t)
    assert (2 * t) % chunk == 0
    words = -(-rows * tm // 1024) * 1024
    init = jnp.concatenate([dump.reshape(-1), jnp.zeros((words - rows * tm,), jnp.int32)])
    hbm = pl.BlockSpec(memory_space=pl.ANY)
    grid_spec = pltpu.PrefetchScalarGridSpec(
        num_scalar_prefetch=1,
        grid=(1,),
        in_specs=[hbm, hbm],
        out_specs=hbm,
        scratch_shapes=[
            pltpu.SMEM((words,), jnp.int32),
            pltpu.SMEM((chunk,), jnp.int32),
            pltpu.SMEM((chunk,), jnp.int32),
            pltpu.SemaphoreType.DMA((1,)),
            pltpu.SemaphoreType.DMA((2,)),
            pltpu.SemaphoreType.DMA((1,)),
        ],
    )
    flat = pl.pallas_call(
        functools.partial(_plan_body, t),
        grid_spec=grid_spec,
        out_shape=jax.ShapeDtypeStruct((words,), jnp.int32),
        compiler_params=_cparams(("arbitrary",)),
        name="plan",
    )(off + tm, ids_flat, init)
    return flat[:rows * tm].reshape(rows, tm)


def _final_body(x_ref, y0_ref, y1_ref, wt_ref, pmod_ref, g_ref, o_ref):
    x = _combine(x_ref[...], y0_ref, y1_ref, wt_ref, pmod_ref[0, 5:6, :])
    ms = jnp.mean(x * x, axis=-1, keepdims=True)
    o_ref[...] = x * lax.rsqrt(ms + NORM_EPS) * g_ref[...]


def _final(x, y2, wts, pmod, g_final, seq):
    t, d = x.shape
    tm = FINAL_TM
    tiles_per_seq = seq // tm
    row = lambda i: (i, 0)
    return pl.pallas_call(
        _final_body,
        grid=(t // tm,),
        in_specs=[
            pl.BlockSpec((tm, d), row),
            pl.BlockSpec((tm * TOKEN_TILE_ROWS, LANES), row),
            pl.BlockSpec((tm * TOKEN_TILE_ROWS, LANES), lambda i: (i + t // tm, 0)),
            pl.BlockSpec((tm, LANES), row),
            pl.BlockSpec((1, N_MOD, d), lambda i: (i // tiles_per_seq, 0, 0)),
            pl.BlockSpec((1, d), lambda i: (0, 0)),
        ],
        out_specs=pl.BlockSpec((tm, d), row),
        out_shape=jax.ShapeDtypeStruct((t, d), F32),
        compiler_params=_cparams(("arbitrary",)),
        name="final",
    )(x, y2, y2, wts, pmod, g_final)


def _rotary_tables(positions):
    half = ROT_DIM // 2
    inv_freq = ROPE_THETA ** (-(jnp.arange(0, ROT_DIM, 2, dtype=F32) / ROT_DIM))
    ang = positions.reshape(-1).astype(F32)[:, None] * inv_freq
    cos, sin = jnp.cos(ang), jnp.sin(ang)
    t = ang.shape[0]
    pad = HEAD_DIM - ROT_DIM
    cos_h = jnp.concatenate([cos, cos, jnp.ones((t, pad), F32)], axis=1)
    sa_h = jnp.concatenate([-sin, jnp.zeros((t, HEAD_DIM - half), F32)], axis=1)
    sb_h = jnp.concatenate([jnp.zeros((t, half), F32), sin, jnp.zeros((t, pad), F32)], axis=1)
    rep = LANES // HEAD_DIM
    return jnp.tile(cos_h, (1, rep)), jnp.tile(sa_h, (1, rep)), jnp.tile(sb_h, (1, rep))


def kernel(x, c, positions, w_ada, b_ada, g_norm1, w_in, b_in, sinks, ln_v_g, ln_v_b, w_spatial, b_spatial,
           g_attn_out, g_gmlp_out, w_out, g_norm2, w_router_group, b_router_group, w_router_expert,
           b_router_expert, w_e_gate, w_e_up, w_e_down, g_final):
    batch, seq, d = x.shape
    depth = w_ada.shape[0]
    t = batch * seq
    assert d == D_MODEL and seq % max(PROJ_TM, OUT_TM, FINAL_TM, MIX_BLOCKS * BLOCK) == 0 and (2 * t) % MOE_TM == 0

    c_pad = jnp.zeros((SUBLANES, d), F32).at[:batch].set(c)
    mod = _adaln(c_pad, w_ada, b_ada).reshape(depth, SUBLANES, N_MOD, d)
    cos_t, sa_t, sb_t = _rotary_tables(positions)

    n_route = N_GROUPS + N_EXPERTS
    w_router = jnp.concatenate(
        [w_router_group, w_router_expert, jnp.zeros((depth, d, LANES - n_route), F32)], axis=-1)
    b_router = jnp.concatenate(
        [b_router_group, b_router_expert, jnp.zeros((depth, LANES - n_route), F32)], axis=-1)

    xt = x.reshape(t, d)
    prev = None
    for l in range(depth):
        xt, qkv, z = _proj(xt, prev, mod[l], g_norm1[l][None], w_in[l].astype(BF16), b_in[l][None],
                           cos_t, sa_t, sb_t, seq)
        merged = _mixer(qkv, z, sinks[l], w_spatial[l], b_spatial[l].T, ln_v_g[l][None], ln_v_b[l][None],
                        g_attn_out[l][None], g_gmlp_out[l][None], batch, seq)
        xt, h2, ids, wts, counts = _outproj(merged, xt, mod[l], g_norm2[l][None], w_out[l].astype(BF16),
                                            w_router[l].astype(BF16), b_router[l][None], seq)
        src2d, dst2d, tile_expert, next_expert, n_used = _route_plan(ids, counts, t)
        y2 = _moe(l, h2, src2d, dst2d, tile_expert, next_expert, n_used, w_e_gate, w_e_up, w_e_down)
        prev = (y2, wts, mod[l])
    out = _final(xt, prev[0], prev[1], prev[2], g_final[None], seq)
    return out.reshape(batch, seq, d)
```

```python
import functools

import jax
import jax.numpy as jnp
import numpy as np
from jax import lax
from jax.experimental import pallas as pl
from jax.experimental.pallas import tpu as pltpu

F32 = jnp.float32
BF16 = jnp.bfloat16

D_MODEL = 2048
ATTN_WIDTH = D_MODEL // 2
HEAD_DIM = 64
N_Q_HEADS = ATTN_WIDTH // HEAD_DIM
N_KV_HEADS = 2
GQA_GROUP = N_Q_HEADS // N_KV_HEADS
WINDOW = 128
BLOCK = 128
ROT_DIM = HEAD_DIM // 4
ROPE_THETA = 500000.0
GMLP_WIDTH = D_MODEL - ATTN_WIDTH
GMLP_GROUPS = 8
GMLP_GROUP_DIM = GMLP_WIDTH // GMLP_GROUPS
Q_COLS = N_Q_HEADS * HEAD_DIM
KV_COLS = N_KV_HEADS * HEAD_DIM
QKV_COLS = Q_COLS + 2 * KV_COLS
IN_COLS = QKV_COLS + 2 * GMLP_WIDTH
N_GROUPS = 4
EXPERTS_PER_GROUP = 8
N_EXPERTS = N_GROUPS * EXPERTS_PER_GROUP
D_EXPERT = D_MODEL // 4
NORM_EPS = 1e-6
N_MOD = 6

LANES = 128
SUBLANES = 8
VMEM_LIMIT_BYTES = 56 * 1024 * 1024
TOKEN_TILE_ROWS = D_MODEL // 2 // LANES

ADA_TN = 1024
PROJ_TM = 256
OUT_TM = 512
MIX_BLOCKS = 4
MOE_TM = 256
MOE_N_CHUNK = 256
MOE_BUFS = 3
FINAL_TM = 512


def _cparams(sem):
    return pltpu.CompilerParams(dimension_semantics=sem, vmem_limit_bytes=VMEM_LIMIT_BYTES)


def _adaln_body(c_ref, w_ref, b_ref, o_ref):
    c = c_ref[...]
    ca = (c * jax.nn.sigmoid(c)).astype(BF16)
    w = w_ref[0].astype(BF16)
    o_ref[0] = jnp.dot(ca, w, preferred_element_type=F32) + b_ref[0]


def _adaln(c_pad, w_ada, b_ada):
    depth, d, n = w_ada.shape
    rows = c_pad.shape[0]
    return pl.pallas_call(
        _adaln_body,
        grid=(depth, n // ADA_TN),
        in_specs=[
            pl.BlockSpec((rows, d), lambda l, j: (0, 0)),
            pl.BlockSpec((1, d, ADA_TN), lambda l, j: (l, 0, j)),
            pl.BlockSpec((1, 1, ADA_TN), lambda l, j: (l, 0, j)),
        ],
        out_specs=pl.BlockSpec((1, rows, ADA_TN), lambda l, j: (l, 0, j)),
        out_shape=jax.ShapeDtypeStruct((depth, rows, n), F32),
        compiler_params=_cparams(("arbitrary", "arbitrary")),
        name="adaln",
    )(c_pad, w_ada, b_ada.reshape(depth, 1, n))


def _rms_mod(x, g, sc, sh):
    ms = jnp.mean(x * x, axis=-1, keepdims=True)
    return (x * lax.rsqrt(ms + NORM_EPS) * g) * (1.0 + sc) + sh


def _pack_bf16_halves(h):
    half = h.shape[1] // 2
    return _pack_bf16_pair(h[:, :half], h[:, half:])


def _pack_bf16_pair(lo, hi):
    lo = lax.bitcast_convert_type(lo.astype(BF16).astype(F32), jnp.uint32)
    hi = lax.bitcast_convert_type(hi.astype(BF16).astype(F32), jnp.uint32)
    return lax.shift_right_logical(lo, jnp.uint32(16)) | (hi & jnp.uint32(0xFFFF0000))


def _load_token_tiles(ref, base, n):
    return jnp.concatenate(
        [ref[pl.ds(base + c, n, stride=TOKEN_TILE_ROWS), :] for c in range(TOKEN_TILE_ROWS)], axis=1)


def _unpack_bf16_halves(p):
    lo = lax.bitcast_convert_type(lax.shift_left(p, jnp.uint32(16)), F32).astype(BF16)
    hi = lax.bitcast_convert_type(p & jnp.uint32(0xFFFF0000), F32).astype(BF16)
    return jnp.concatenate([lo, hi], axis=1)


def _combine(x, y0_ref, y1_ref, wt_ref, gate):
    wt = wt_ref[...]
    n = x.shape[0]
    y0 = _unpack_bf16_halves(_load_token_tiles(y0_ref, 0, n)).astype(F32)
    y1 = _unpack_bf16_halves(_load_token_tiles(y1_ref, 0, n)).astype(F32)
    y = wt[:, 0:1] * y0 + wt[:, 1:2] * y1
    return x + (1.0 + gate) * y


def _gelu_exact(z):
    return 0.5 * z * (1.0 + lax.erf(z * (2.0 ** -0.5)))


def _proj_body(has_prev, *refs):
    if has_prev:
        (x_ref, y0_ref, y1_ref, wt_ref, pmod_ref, mod_ref, g_ref, w_ref, b_ref, cos_ref, sa_ref, sb_ref,
         xo_ref, qkv_ref, z_ref) = refs
        x = _combine(x_ref[...], y0_ref, y1_ref, wt_ref, pmod_ref[0, 5:6, :])
        xo_ref[...] = x
    else:
        x_ref, mod_ref, g_ref, w_ref, b_ref, cos_ref, sa_ref, sb_ref, qkv_ref, z_ref = refs
        x = x_ref[...]
    mod = mod_ref[0]
    h = _rms_mod(x, g_ref[...], mod[1:2, :], mod[0:1, :]).astype(BF16)

    qkv = jnp.dot(h, w_ref[:, :QKV_COLS], preferred_element_type=F32) + b_ref[:, :QKV_COLS]
    cos, sa, sb = cos_ref[...], sa_ref[...], sb_ref[...]
    n_rot = (Q_COLS + KV_COLS) // LANES
    for j in range(QKV_COLS // LANES):
        ch = qkv[:, j * LANES:(j + 1) * LANES]
        if j < n_rot:
            ch = ch * cos + pltpu.roll(ch, LANES - ROT_DIM // 2, 1) * sa + pltpu.roll(ch, ROT_DIM // 2, 1) * sb
        if j < Q_COLS // LANES:
            ch = ch * (HEAD_DIM ** -0.5)
        qkv_ref[:, j * LANES:(j + 1) * LANES] = ch.astype(BF16)

    z = jnp.dot(h, w_ref[:, QKV_COLS:], preferred_element_type=F32) + b_ref[:, QKV_COLS:]
    z_ref[...] = _gelu_exact(z)


def _proj(x, prev, mod_l, g1, w_in, b_in, cos_t, sa_t, sb_t, seq):
    t, d = x.shape
    tm = PROJ_TM
    tiles_per_seq = seq // tm
    row = lambda i: (i, 0)
    const = lambda i: (0, 0)
    modmap = lambda i: (i // tiles_per_seq, 0, 0)
    in_specs = [pl.BlockSpec((tm, d), row)]
    args = [x]
    if prev is not None:
        y2, wts, pmod = prev
        in_specs += [pl.BlockSpec((tm * TOKEN_TILE_ROWS, LANES), row),
                     pl.BlockSpec((tm * TOKEN_TILE_ROWS, LANES), lambda i: (i + t // tm, 0)),
                     pl.BlockSpec((tm, LANES), row), pl.BlockSpec((1, N_MOD, d), modmap)]
        args += [y2, y2, wts, pmod]
    in_specs += [
        pl.BlockSpec((1, N_MOD, d), modmap),
        pl.BlockSpec((1, d), const),
        pl.BlockSpec((d, IN_COLS), const, pipeline_mode=pl.Buffered(1)),
        pl.BlockSpec((1, IN_COLS), const),
        pl.BlockSpec((tm, LANES), row),
        pl.BlockSpec((tm, LANES), row),
        pl.BlockSpec((tm, LANES), row),
    ]
    args += [mod_l, g1, w_in, b_in, cos_t, sa_t, sb_t]
    out_specs = [pl.BlockSpec((tm, QKV_COLS), row), pl.BlockSpec((tm, 2 * GMLP_WIDTH), row)]
    out_shape = [jax.ShapeDtypeStruct((t, QKV_COLS), BF16), jax.ShapeDtypeStruct((t, 2 * GMLP_WIDTH), F32)]
    if prev is not None:
        out_specs = [pl.BlockSpec((tm, d), row)] + out_specs
        out_shape = [jax.ShapeDtypeStruct((t, d), F32)] + out_shape
    outs = pl.pallas_call(
        functools.partial(_proj_body, prev is not None),
        grid=(t // tm,),
        in_specs=in_specs,
        out_specs=out_specs,
        out_shape=out_shape,
        compiler_params=_cparams(("arbitrary",)),
        name="proj",
    )(*args)
    if prev is None:
        return (x,) + tuple(outs)
    return tuple(outs)


def _mixer_body(sink_ref, q_ref, kvc_ref, kvp_ref, z_ref, ws_ref, bst_ref, lng_ref, lnb_ref, ga_ref, gg_ref,
                o_ref):
    n = pl.program_id(1)
    for sub in range(MIX_BLOCKS):
        r0 = sub * BLOCK
        kvp = kvp_ref[...] if sub == 0 else kvc_ref[r0 - BLOCK:r0, :]
        has_prev = (n > 0) if sub == 0 else None
        _mixer_block(sink_ref, q_ref[r0:r0 + BLOCK, :], kvc_ref[r0:r0 + BLOCK, :], kvp, has_prev, z_ref, r0,
                     ws_ref, bst_ref, lng_ref, lnb_ref, ga_ref, gg_ref, o_ref)


def _mixer_block(sink_ref, q, kvc, kvp, has_prev, z_ref, r0, ws_ref, bst_ref, lng_ref, lnb_ref, ga_ref, gg_ref,
                 o_ref):
    rows = slice(r0, r0 + BLOCK)
    kcat = jnp.concatenate([kvp[:, :KV_COLS], kvc[:, :KV_COLS]], axis=0)
    vcat = jnp.concatenate([kvp[:, KV_COLS:], kvc[:, KV_COLS:]], axis=0)

    lane = lax.broadcasted_iota(jnp.int32, (BLOCK, LANES), 1)
    lo = lane < HEAD_DIM
    kj = lax.broadcasted_iota(jnp.int32, (2 * BLOCK, BLOCK), 0)
    qi = lax.broadcasted_iota(jnp.int32, (2 * BLOCK, BLOCK), 1)
    rel = qi + BLOCK - kj
    valid = (rel >= 0) & (rel < WINDOW)
    if has_prev is not None:
        valid = valid & (has_prev | (kj >= BLOCK))
    row_lo = lax.broadcasted_iota(jnp.int32, (LANES, BLOCK), 0) < HEAD_DIM
    zero_q = jnp.zeros((BLOCK, LANES), BF16)
    nt_dims = (((1,), (1,)), ((), ()))

    attn = []
    chunks_per_kv = GQA_GROUP * HEAD_DIM // LANES
    for hk in range(N_KV_HEADS):
        ksl = kcat[:, hk * HEAD_DIM:(hk + 1) * HEAD_DIM]
        vsl = vcat[:, hk * HEAD_DIM:(hk + 1) * HEAD_DIM]
        kk = jnp.concatenate([ksl, ksl], axis=1)
        vt = jnp.concatenate([vsl, vsl], axis=1).astype(F32).T.astype(BF16)
        for j in range(chunks_per_kv):
            c = hk * chunks_per_kv + j
            qc = q[:, c * LANES:(c + 1) * LANES]
            probs = []
            for half in range(2):
                qm = jnp.where(lo, qc, zero_q) if half == 0 else jnp.where(lo, zero_q, qc)
                s = lax.dot_general(kk, qm, nt_dims, preferred_element_type=F32)
                s = jnp.where(valid, s, -jnp.inf)
                sink = sink_ref[2 * c + half]
                m = jnp.maximum(jnp.max(s, axis=0, keepdims=True), sink)
                p = jnp.exp(s - m)
                denom = jnp.sum(p, axis=0, keepdims=True) + jnp.exp(sink - m)
                probs.append((p * (1.0 / denom)).astype(BF16))
            pn = jnp.concatenate(probs, axis=1)
            ot = jnp.dot(vt, pn, preferred_element_type=F32)
            sel = jnp.where(row_lo, ot[:, :BLOCK], ot[:, BLOCK:])
            attn.append(sel.T)

    ss = attn[0] * attn[0]
    for a in attn[1:]:
        ss = ss + a * a
    scale = lax.rsqrt(jnp.sum(ss, axis=-1, keepdims=True) * (1.0 / ATTN_WIDTH) + NORM_EPS)
    for c, a in enumerate(attn):
        o_ref[rows, c * LANES:(c + 1) * LANES] = (a * scale * ga_ref[:, c * LANES:(c + 1) * LANES]).astype(BF16)

    u = z_ref[rows, :GMLP_WIDTH]
    v = z_ref[rows, GMLP_WIDTH:]
    mu = jnp.mean(v, axis=-1, keepdims=True)
    vc = v - mu
    var = jnp.mean(vc * vc, axis=-1, keepdims=True)
    vn = ((vc * lax.rsqrt(var + NORM_EPS)) * lng_ref[...] + lnb_ref[...]).astype(BF16)
    ti = lax.broadcasted_iota(jnp.int32, (BLOCK, BLOCK), 0)
    si = lax.broadcasted_iota(jnp.int32, (BLOCK, BLOCK), 1)
    tril = si <= ti
    bst = bst_ref[...]
    gm = []
    for h in range(GMLP_GROUPS):
        w = jnp.where(tril, ws_ref[h], 0.0).astype(BF16)
        sp = jnp.dot(w, vn[:, h * LANES:(h + 1) * LANES], preferred_element_type=F32) + bst[:, h:h + 1]
        gm.append(u[:, h * LANES:(h + 1) * LANES] * sp)
    ss = gm[0] * gm[0]
    for a in gm[1:]:
        ss = ss + a * a
    scale = lax.rsqrt(jnp.sum(ss, axis=-1, keepdims=True) * (1.0 / GMLP_WIDTH) + NORM_EPS)
    for c, a in enumerate(gm):
        o_ref[rows, ATTN_WIDTH + c * LANES:ATTN_WIDTH + (c + 1) * LANES] = (
            a * scale * gg_ref[:, c * LANES:(c + 1) * LANES]).astype(BF16)


def _mixer(qkv, z, sinks, w_sp, bst, ln_g, ln_b, g_attn, g_gmlp, batch, seq):
    t = qkv.shape[0]
    nb = seq // BLOCK
    steps = nb // MIX_BLOCKS
    rows = MIX_BLOCKS * BLOCK
    kv_blk = Q_COLS // (2 * KV_COLS)
    cur = lambda b, n, s: (b * steps + n, 0)
    const2 = lambda b, n, s: (0, 0)
    grid_spec = pltpu.PrefetchScalarGridSpec(
        num_scalar_prefetch=1,
        grid=(batch, steps),
        in_specs=[
            pl.BlockSpec((rows, Q_COLS), cur),
            pl.BlockSpec((rows, 2 * KV_COLS), lambda b, n, s: (b * steps + n, kv_blk)),
            pl.BlockSpec((BLOCK, 2 * KV_COLS),
                         lambda b, n, s: (b * nb + jnp.maximum(MIX_BLOCKS * n - 1, 0), kv_blk)),
            pl.BlockSpec((rows, 2 * GMLP_WIDTH), cur),
            pl.BlockSpec((GMLP_GROUPS, BLOCK, BLOCK), lambda b, n, s: (0, 0, 0)),
            pl.BlockSpec((BLOCK, GMLP_GROUPS), const2),
            pl.BlockSpec((1, GMLP_WIDTH), const2),
            pl.BlockSpec((1, GMLP_WIDTH), const2),
            pl.BlockSpec((1, ATTN_WIDTH), const2),
            pl.BlockSpec((1, GMLP_WIDTH), const2),
        ],
        out_specs=pl.BlockSpec((rows, D_MODEL), cur),
    )
    return pl.pallas_call(
        _mixer_body,
        grid_spec=grid_spec,
        out_shape=jax.ShapeDtypeStruct((t, D_MODEL), BF16),
        compiler_params=_cparams(("arbitrary", "arbitrary")),
        name="mixer",
    )(sinks, qkv, qkv, qkv, z, w_sp, bst, ln_g, ln_b, g_attn, g_gmlp)


def _first_index_of_max(vals, lane, width):
    mx = jnp.max(vals, axis=-1, keepdims=True)
    idx = jnp.min(jnp.where(vals == mx, lane, width), axis=-1, keepdims=True)
    return mx, idx


def _outproj_body(m_ref, x_ref, mod_ref, g_ref, w_ref, wr_ref, br_ref,
                  xo_ref, h_ref, ids_ref, wt_ref, cnt_ref, carry_ref):
    i = pl.program_id(0)
    tm = x_ref.shape[0]

    @pl.when(i == 0)
    def _():
        carry_ref[...] = jnp.zeros_like(carry_ref)

    mod = mod_ref[0]
    y = jnp.dot(m_ref[...], w_ref[...], preferred_element_type=F32)
    x = x_ref[...] + (1.0 + mod[2:3, :]) * y
    xo_ref[...] = x
    h = _rms_mod(x, g_ref[...], mod[4:5, :], mod[3:4, :])
    packed = _pack_bf16_halves(h)
    for c in range(TOKEN_TILE_ROWS):
        h_ref[pl.ds(c, tm, stride=TOKEN_TILE_ROWS), :] = packed[:, c * LANES:(c + 1) * LANES]

    logits = jnp.dot(h.astype(BF16), wr_ref[...], preferred_element_type=F32) + br_ref[...]
    lane = lax.broadcasted_iota(jnp.int32, (tm, LANES), 1)
    neg = -jnp.inf
    gl = jnp.where(lane < N_GROUPS, logits, neg)
    ge = jnp.exp(gl - jnp.max(gl, axis=-1, keepdims=True))
    gprob = ge / jnp.sum(ge, axis=-1, keepdims=True)
    g_p, g_idx = _first_index_of_max(gprob, lane, LANES)

    base = N_GROUPS + EXPERTS_PER_GROUP * g_idx
    in_grp = (lane >= base) & (lane < base + EXPERTS_PER_GROUP)
    el = jnp.where(in_grp, logits, neg)
    ee = jnp.exp(el - jnp.max(el, axis=-1, keepdims=True))
    eprob = jnp.where(in_grp, ee / jnp.sum(ee, axis=-1, keepdims=True), -1.0)
    p1, i1 = _first_index_of_max(eprob, lane, LANES)
    eprob2 = jnp.where(lane == i1, -1.0, eprob)
    p2, i2 = _first_index_of_max(eprob2, lane, LANES)
    psum = p1 + p2
    w1 = g_p * (p1 / psum)
    w2 = g_p * (p2 / psum)
    e1 = i1 - N_GROUPS
    e2 = i2 - N_GROUPS

    oh1 = lane == e1
    oh2 = lane == e2
    onehot = jnp.where(oh1 | oh2, 1.0, 0.0)
    ri = lax.broadcasted_iota(jnp.int32, (tm, tm), 0)
    ci = lax.broadcasted_iota(jnp.int32, (tm, tm), 1)
    lower = jnp.where(ci < ri, 1.0, 0.0).astype(BF16)
    before = jnp.dot(lower, onehot.astype(BF16), preferred_element_type=F32) + carry_ref[...]
    r1 = jnp.sum(jnp.where(oh1, before, 0.0), axis=-1, keepdims=True)
    r2 = jnp.sum(jnp.where(oh2, before, 0.0), axis=-1, keepdims=True)
    carry = carry_ref[...] + jnp.sum(onehot, axis=0, keepdims=True)
    carry_ref[...] = carry
    cnt_ref[...] = jnp.broadcast_to(carry, cnt_ref.shape).astype(jnp.int32)

    ids = jnp.where(lane == 0, e1, jnp.where(lane == 1, e2, 0))
    ranks = jnp.where(lane == 0, r1, jnp.where(lane == 1, r2, 0.0)).astype(jnp.int32)
    ids_ref[...] = lax.shift_left(ids, ROUTE_SHIFT) | ranks
    wt_ref[...] = jnp.where(lane == 0, w1, jnp.where(lane == 1, w2, 0.0))


def _outproj(merged, x, mod_l, g2, w_out, w_router, b_router, seq):
    t, d = x.shape
    tm = OUT_TM
    tiles_per_seq = seq // tm
    row = lambda i: (i, 0)
    const = lambda i: (0, 0)
    return pl.pallas_call(
        _outproj_body,
        grid=(t // tm,),
        in_specs=[
            pl.BlockSpec((tm, d), row),
            pl.BlockSpec((tm, d), row),
            pl.BlockSpec((1, N_MOD, d), lambda i: (i // tiles_per_seq, 0, 0)),
            pl.BlockSpec((1, d), const),
            pl.BlockSpec((d, d), const, pipeline_mode=pl.Buffered(1)),
            pl.BlockSpec((d, LANES), const),
            pl.BlockSpec((1, LANES), const),
        ],
        out_specs=[
            pl.BlockSpec((tm, d), row),
            pl.BlockSpec((tm * TOKEN_TILE_ROWS, LANES), row),
            pl.BlockSpec((tm, LANES), row),
            pl.BlockSpec((tm, LANES), row),
            pl.BlockSpec((SUBLANES, LANES), const),
        ],
        out_shape=[
            jax.ShapeDtypeStruct((t, d), F32),
            jax.ShapeDtypeStruct((t * TOKEN_TILE_ROWS, LANES), jnp.uint32),
            jax.ShapeDtypeStruct((t, LANES), jnp.int32),
            jax.ShapeDtypeStruct((t, LANES), F32),
            jax.ShapeDtypeStruct((SUBLANES, LANES), jnp.int32),
        ],
        scratch_shapes=[pltpu.VMEM((1, LANES), F32)],
        compiler_params=_cparams(("arbitrary",)),
        name="outproj",
    )(merged, x, mod_l, g2, w_out, w_router, b_router)


def _moe_body(layer, n_slots, te_ref, nx_ref, nu_ref, h_hbm, src_hbm, dst_hbm, wg_hbm, wu_hbm, wd_hbm, y_hbm,
              first_smem, plan_smem, xbuf, obuf, wgs, wus, wds, wgb, wub, wdb,
              first_sem, plan_sem, g_sem, s_sem, w_sem):
    i = pl.program_id(0)
    n_used = nu_ref[0]
    nb = MOE_BUFS
    tm = plan_smem.shape[2]
    d, de = wgb.shape

    def weight_copies(e):
        return (pltpu.make_async_copy(wg_hbm.at[layer, e], wgs, w_sem.at[0]),
                pltpu.make_async_copy(wu_hbm.at[layer, e], wus, w_sem.at[1]),
                pltpu.make_async_copy(wd_hbm.at[layer, e], wds, w_sem.at[2]))
    b0 = i % nb
    b1 = (i + 1) % nb
    b2 = (i + 2) % nb
    p0 = i % 2
    p1 = 1 - p0

    class _PlanCopy:
        def __init__(self, k, slot):
            self.copies = (
                pltpu.make_async_copy(src_hbm.at[k + 3], plan_smem.at[slot, 0], plan_sem.at[slot, 0]),
                pltpu.make_async_copy(dst_hbm.at[k], plan_smem.at[slot, 1], plan_sem.at[slot, 1]))

        def start(self):
            for cp in self.copies:
                cp.start()

        def wait(self):
            for cp in self.copies:
                cp.wait()

    plan_copy = _PlanCopy

    xrows = tm * TOKEN_TILE_ROWS

    def token_tile(row):
        return pl.ds(pl.multiple_of(row, TOKEN_TILE_ROWS), TOKEN_TILE_ROWS)

    def tile_buf(buf):
        return pl.ds(pl.multiple_of(buf * xrows, xrows), xrows)

    def gather_row(src, buf, j):
        return pltpu.make_async_copy(h_hbm.at[token_tile(src), :],
                                     xbuf.at[token_tile(buf * xrows + j * TOKEN_TILE_ROWS), :], g_sem.at[buf])

    def scatter_row(dst, buf, j):
        return pltpu.make_async_copy(obuf.at[token_tile(buf * xrows + j * TOKEN_TILE_ROWS), :],
                                     y_hbm.at[token_tile(dst), :], s_sem.at[buf])

    def wait_gather(buf):
        pltpu.make_async_copy(h_hbm.at[pl.ds(0, xrows), :], xbuf.at[tile_buf(buf), :], g_sem.at[buf]).wait()

    def wait_scatter(buf):
        pltpu.make_async_copy(obuf.at[tile_buf(buf), :], y_hbm.at[pl.ds(0, xrows), :], s_sem.at[buf]).wait()

    @pl.when(i == 0)
    def _():
        first = [pltpu.make_async_copy(src_hbm.at[1 + tile], first_smem.at[tile], first_sem.at[tile])
                 for tile in range(2)]
        for cp in first:
            cp.start()
        plan_copy(0, 0).start()
        for cp in weight_copies(te_ref[0]):
            cp.start()
        for cp in first:
            cp.wait()
        for tile in range(2):
            def body(j, carry, tile=tile):
                gather_row(first_smem[tile, j], tile, j).start()
                return carry
            lax.fori_loop(0, tm, body, 0, unroll=8)
        obuf[pl.ds((nb - 1) * xrows, xrows), :] = jnp.zeros((xrows, LANES), jnp.uint32)

    @pl.when(i < n_used)
    def _():
        plan_copy(i, p0).wait()
        plan_copy(i + 1, p1).start()
        wait_gather(b0)

        @pl.when(i >= 2)
        def _():
            wait_scatter(b0)

        new_expert = jnp.logical_or(i == 0, te_ref[i] != te_ref[jnp.maximum(i - 1, 0)])

        @pl.when(new_expert)
        def _():
            for cp in weight_copies(te_ref[i]):
                cp.wait()
            wgb[...] = wgs[...].astype(BF16)
            wub[...] = wus[...].astype(BF16)
            wdb[...] = wds[...].astype(BF16)

            @pl.when(nx_ref[i] != te_ref[i])
            def _():
                for cp in weight_copies(nx_ref[i]):
                    cp.start()

        copies = []
        for j in range(tm):
            copies += [(0, j), (1, j)]
        n_up = de // MOE_N_CHUNK
        n_down = d // MOE_N_CHUNK
        n_pieces = 2 * n_up + n_down
        per_piece = -(-len(copies) // n_pieces)

        def issue(piece):
            for kind, j in copies[piece * per_piece:(piece + 1) * per_piece]:
                if kind == 0:
                    gather_row(plan_smem[p0, 0, j], b2, j).start()
                else:
                    scatter_row(plan_smem[p0, 1, j], b2, j).start()

        xbase = pl.multiple_of(b0 * xrows, xrows)
        xb = _unpack_bf16_halves(_load_token_tiles(xbuf, xbase, tm))
        piece = 0
        acts = []
        for n in range(n_up):
            cols = slice(n * MOE_N_CHUNK, (n + 1) * MOE_N_CHUNK)
            g = jnp.dot(xb, wgb[:, cols], preferred_element_type=F32)
            issue(piece)
            u = jnp.dot(xb, wub[:, cols], preferred_element_type=F32)
            issue(piece + 1)
            piece += 2
            acts.append(((g * jax.nn.sigmoid(g)) * u).astype(BF16))
        a = jnp.concatenate(acts, axis=1)
        for n in range(n_down // 2):
            cols_lo = slice(n * MOE_N_CHUNK, (n + 1) * MOE_N_CHUNK)
            cols_hi = slice(d // 2 + n * MOE_N_CHUNK, d // 2 + (n + 1) * MOE_N_CHUNK)
            y_lo = jnp.dot(a, wdb[:, cols_lo], preferred_element_type=F32)
            issue(piece)
            y_hi = jnp.dot(a, wdb[:, cols_hi], preferred_element_type=F32)
            issue(piece + 1)
            piece += 2
            packed = _pack_bf16_pair(y_lo, y_hi)
            for c in range(MOE_N_CHUNK // LANES):
                obuf[pl.ds(xbase + n * (MOE_N_CHUNK // LANES) + c, tm, stride=TOKEN_TILE_ROWS), :] = (
                    packed[:, c * LANES:(c + 1) * LANES])

    @pl.when(i == n_used)
    def _():
        plan_copy(i, p0).wait()
        wait_gather(b0)
        wait_gather(b1)
        wait_scatter(b1)

        @pl.when(i >= 2)
        def _():
            wait_scatter(b0)

        def body(j, carry):
            scatter_row(plan_smem[p0, 1, j], b2, j).start()
            return carry
        lax.fori_loop(0, tm, body, 0, unroll=8)
        wait_scatter(b2)
        obuf[tile_buf(b0), :] = jnp.zeros((xrows, LANES), jnp.uint32)
        for k in range(nb):
            fill = pltpu.make_async_copy(obuf.at[tile_buf(b0), :],
                                         y_hbm.at[pl.ds((n_slots + k * tm) * TOKEN_TILE_ROWS, xrows), :], s_sem.at[b0])
            fill.start()
            fill.wait()


def _moe(layer, h2, src2d, dst2d, tile_expert, next_expert, n_used, w_gate, w_up, w_down):
    t = h2.shape[0] // TOKEN_TILE_ROWS
    n_slots = 2 * t
    steps, tm = dst2d.shape[0] - MOE_BUFS, dst2d.shape[1]
    d, de = w_gate.shape[-2:]
    hbm = pl.BlockSpec(memory_space=pl.ANY)
    grid_spec = pltpu.PrefetchScalarGridSpec(
        num_scalar_prefetch=3,
        grid=(steps,),
        in_specs=[hbm] * 6,
        out_specs=hbm,
        scratch_shapes=[
            pltpu.SMEM((2, tm), jnp.int32),
            pltpu.SMEM((2, 2, tm), jnp.int32),
            pltpu.VMEM((MOE_BUFS * tm * TOKEN_TILE_ROWS, LANES), jnp.uint32),
            pltpu.VMEM((MOE_BUFS * tm * TOKEN_TILE_ROWS, LANES), jnp.uint32),
            pltpu.VMEM((d, de), F32),
            pltpu.VMEM((d, de), F32),
            pltpu.VMEM((de, d), F32),
            pltpu.VMEM((d, de), BF16),
            pltpu.VMEM((d, de), BF16),
            pltpu.VMEM((de, d), BF16),
            pltpu.SemaphoreType.DMA((2,)),
            pltpu.SemaphoreType.DMA((2, 2)),
            pltpu.SemaphoreType.DMA((MOE_BUFS,)),
            pltpu.SemaphoreType.DMA((MOE_BUFS,)),
            pltpu.SemaphoreType.DMA((3,)),
        ],
    )
    return pl.pallas_call(
        functools.partial(_moe_body, layer, n_slots),
        grid_spec=grid_spec,
        out_shape=jax.ShapeDtypeStruct(((n_slots + MOE_BUFS * tm) * TOKEN_TILE_ROWS, LANES), jnp.uint32),
        compiler_params=_cparams(("arbitrary",)),
        name="moe",
    )(tile_expert, next_expert, n_used, h2, src2d, dst2d, w_gate, w_up, w_down)


def _route_plan(ids, counts, t):
    tm = MOE_TM
    nt = (2 * t) // tm + N_EXPERTS
    cnt = counts[0, :N_EXPERTS]
    tiles = (cnt + tm - 1) // tm
    tile_end = jnp.cumsum(tiles)
    off = ((tile_end - tiles) * tm).astype(jnp.int32)
    tile_buf = np.arange(-1, nt) % MOE_BUFS
    dump = (2 * t + tile_buf[:, None] * tm + np.arange(tm)[None, :]).astype(np.int32)
    init = np.concatenate([dump, np.zeros((MOE_BUFS, tm), np.int32)], axis=0)
    dst2d = _plan_scatter(off, ids[:, 0:2].reshape(-1), jnp.asarray(init), t)
    src2d = jnp.where(dst2d < t, dst2d, jnp.where(dst2d < 2 * t, dst2d - t, 0)) * TOKEN_TILE_ROWS
    n_used = tile_end[-1]
    steps = jnp.minimum(jnp.arange(nt + 1, dtype=jnp.int32), n_used - 1)
    tile_expert = jnp.sum(tile_end[None, :] <= steps[:, None], axis=1).astype(jnp.int32)
    eid = jnp.arange(N_EXPERTS, dtype=jnp.int32)
    later_used = (tiles[None, :] > 0) & (eid[None, :] > eid[:, None])
    nxt = jnp.min(jnp.where(later_used, eid[None, :], N_EXPERTS), axis=1)
    next_expert = jnp.where(nxt == N_EXPERTS, eid, nxt)[tile_expert].astype(jnp.int32)
    return src2d, dst2d * TOKEN_TILE_ROWS, tile_expert, next_expert, n_used.reshape(1).astype(jnp.int32)


PLAN_CHUNK = 8192
ROUTE_SHIFT = 20


def _plan_body(t, off_ref, ids_hbm, init_hbm, out_hbm, tab, ids_a, ids_b, init_sem, ids_sem, out_sem):
    chunk = ids_a.shape[0]
    n_chunks = (2 * t) // chunk
    bufs = (ids_a, ids_b)

    def ids_copy(c):
        return pltpu.make_async_copy(ids_hbm.at[pl.ds(c * chunk, chunk)], bufs[c % 2], ids_sem.at[c % 2])

    init = pltpu.make_async_copy(init_hbm, tab, init_sem.at[0])
    init.start()
    ids_copy(0).start()
    init.wait()
    for c in range(n_chunks):
        ids_copy(c).wait()
        if c + 1 < n_chunks:
            ids_copy(c + 1).start()
        buf = bufs[c % 2]
        tok0 = c * (chunk // 2)

        def body(tt, carry, buf=buf, tok0=tok0):
            tok = tok0 + tt
            for k in range(2):
                word = buf[2 * tt + k]
                expert = lax.shift_right_logical(word, ROUTE_SHIFT)
                rank = word & ((1 << ROUTE_SHIFT) - 1)
                tab[off_ref[expert] + rank] = tok + k * t
            return carry
        lax.fori_loop(0, chunk // 2, body, 0, unroll=8)
    out = pltpu.make_async_copy(tab, out_hbm, out_sem.at[0])
    out.start()
    out.wait()


def _plan_scatter(off, ids_flat, dump, t):
    rows, tm = dump.shape
    chunk = min(PLAN_CHUNK, 2 * t)
    assert (2 * t) % chunk == 0
    words = -(-rows * tm // 1024) * 1024
    init = jnp.concatenate([dump.reshape(-1), jnp.zeros((words - rows * tm,), jnp.int32)])
    hbm = pl.BlockSpec(memory_space=pl.ANY)
    grid_spec = pltpu.PrefetchScalarGridSpec(
        num_scalar_prefetch=1,
        grid=(1,),
        in_specs=[hbm, hbm],
        out_specs=hbm,
        scratch_shapes=[
            pltpu.SMEM((words,), jnp.int32),
            pltpu.SMEM((chunk,), jnp.int32),
            pltpu.SMEM((chunk,), jnp.int32),
            pltpu.SemaphoreType.DMA((1,)),
            pltpu.SemaphoreType.DMA((2,)),
            pltpu.SemaphoreType.DMA((1,)),
        ],
    )
    flat = pl.pallas_call(
        functools.partial(_plan_body, t),
        grid_spec=grid_spec,
        out_shape=jax.ShapeDtypeStruct((words,), jnp.int32),
        compiler_params=_cparams(("arbitrary",)),
        name="plan",
    )(off + tm, ids_flat, init)
    return flat[:rows * tm].reshape(rows, tm)


def _final_body(x_ref, y0_ref, y1_ref, wt_ref, pmod_ref, g_ref, o_ref):
    x = _combine(x_ref[...], y0_ref, y1_ref, wt_ref, pmod_ref[0, 5:6, :])
    ms = jnp.mean(x * x, axis=-1, keepdims=True)
    o_ref[...] = x * lax.rsqrt(ms + NORM_EPS) * g_ref[...]


def _final(x, y2, wts, pmod, g_final, seq):
    t, d = x.shape
    tm = FINAL_TM
    tiles_per_seq = seq // tm
    row = lambda i: (i, 0)
    return pl.pallas_call(
        _final_body,
        grid=(t // tm,),
        in_specs=[
            pl.BlockSpec((tm, d), row),
            pl.BlockSpec((tm * TOKEN_TILE_ROWS, LANES), row),
            pl.BlockSpec((tm * TOKEN_TILE_ROWS, LANES), lambda i: (i + t // tm, 0)),
            pl.BlockSpec((tm, LANES), row),
            pl.BlockSpec((1, N_MOD, d), lambda i: (i // tiles_per_seq, 0, 0)),
            pl.BlockSpec((1, d), lambda i: (0, 0)),
        ],
        out_specs=pl.BlockSpec((tm, d), row),
        out_shape=jax.ShapeDtypeStruct((t, d), F32),
        compiler_params=_cparams(("arbitrary",)),
        name="final",
    )(x, y2, y2, wts, pmod, g_final)


def _rotary_tables(positions):
    half = ROT_DIM // 2
    inv_freq = ROPE_THETA ** (-(jnp.arange(0, ROT_DIM, 2, dtype=F32) / ROT_DIM))
    ang = positions.reshape(-1).astype(F32)[:, None] * inv_freq
    cos, sin = jnp.cos(ang), jnp.sin(ang)
    t = ang.shape[0]
    pad = HEAD_DIM - ROT_DIM
    cos_h = jnp.concatenate([cos, cos, jnp.ones((t, pad), F32)], axis=1)
    sa_h = jnp.concatenate([-sin, jnp.zeros((t, HEAD_DIM - half), F32)], axis=1)
    sb_h = jnp.concatenate([jnp.zeros((t, half), F32), sin, jnp.zeros((t, pad), F32)], axis=1)
    rep = LANES // HEAD_DIM
    return jnp.tile(cos_h, (1, rep)), jnp.tile(sa_h, (1, rep)), jnp.tile(sb_h, (1, rep))


def kernel(x, c, positions, w_ada, b_ada, g_norm1, w_in, b_in, sinks, ln_v_g, ln_v_b, w_spatial, b_spatial,
           g_attn_out, g_gmlp_out, w_out, g_norm2, w_router_group, b_router_group, w_router_expert,
           b_router_expert, w_e_gate, w_e_up, w_e_down, g_final):
    batch, seq, d = x.shape
    depth = w_ada.shape[0]
    t = batch * seq
    assert d == D_MODEL and seq % max(PROJ_TM, OUT_TM, FINAL_TM, MIX_BLOCKS * BLOCK) == 0 and (2 * t) % MOE_TM == 0

    c_pad = jnp.zeros((SUBLANES, d), F32).at[:batch].set(c)
    mod = _adaln(c_pad, w_ada, b_ada).reshape(depth, SUBLANES, N_MOD, d)
    cos_t, sa_t, sb_t = _rotary_tables(positions)

    n_route = N_GROUPS + N_EXPERTS
    w_router = jnp.concatenate(
        [w_router_group, w_router_expert, jnp.zeros((depth, d, LANES - n_route), F32)], axis=-1)
    b_router = jnp.concatenate(
        [b_router_group, b_router_expert, jnp.zeros((depth, LANES - n_route), F32)], axis=-1)

    xt = x.reshape(t, d)
    prev = None
    for l in range(depth):
        xt, qkv, z = _proj(xt, prev, mod[l], g_norm1[l][None], w_in[l].astype(BF16), b_in[l][None],
                           cos_t, sa_t, sb_t, seq)
        merged = _mixer(qkv, z, sinks[l], w_spatial[l], b_spatial[l].T, ln_v_g[l][None], ln_v_b[l][None],
                        g_attn_out[l][None], g_gmlp_out[l][None], batch, seq)
        xt, h2, ids, wts, counts = _outproj(merged, xt, mod[l], g_norm2[l][None], w_out[l].astype(BF16),
                                            w_router[l].astype(BF16), b_router[l][None], seq)
        src2d, dst2d, tile_expert, next_expert, n_used = _route_plan(ids, counts, t)
        y2 = _moe(l, h2, src2d, dst2d, tile_expert, next_expert, n_used, w_e_gate, w_e_up, w_e_down)
        prev = (y2, wts, mod[l])
    out = _final(xt, prev[0], prev[1], prev[2], g_final[None], seq)
    return out.reshape(batch, seq, d)
```

```python
import functools

import jax
import jax.numpy as jnp
import numpy as np
from jax import lax
from jax.experimental import pallas as pl
from jax.experimental.pallas import tpu as pltpu

F32 = jnp.float32
BF16 = jnp.bfloat16

D_MODEL = 2048
ATTN_WIDTH = D_MODEL // 2
HEAD_DIM = 64
N_Q_HEADS = ATTN_WIDTH // HEAD_DIM
N_KV_HEADS = 2
GQA_GROUP = N_Q_HEADS // N_KV_HEADS
WINDOW = 128
BLOCK = 128
ROT_DIM = HEAD_DIM // 4
ROPE_THETA = 500000.0
GMLP_WIDTH = D_MODEL - ATTN_WIDTH
GMLP_GROUPS = 8
GMLP_GROUP_DIM = GMLP_WIDTH // GMLP_GROUPS
Q_COLS = N_Q_HEADS * HEAD_DIM
KV_COLS = N_KV_HEADS * HEAD_DIM
QKV_COLS = Q_COLS + 2 * KV_COLS
IN_COLS = QKV_COLS + 2 * GMLP_WIDTH
N_GROUPS = 4
EXPERTS_PER_GROUP = 8
N_EXPERTS = N_GROUPS * EXPERTS_PER_GROUP
D_EXPERT = D_MODEL // 4
NORM_EPS = 1e-6
N_MOD = 6

LANES = 128
SUBLANES = 8
VMEM_LIMIT_BYTES = 56 * 1024 * 1024
TOKEN_TILE_ROWS = D_MODEL // 2 // LANES

ADA_TN = 1024
PROJ_TM = 512
OUT_TM = 512
MIX_BLOCKS = 4
MOE_TM = 256
MOE_N_CHUNK = 256
MOE_BUFS = 3
FINAL_TM = 512


def _cparams(sem):
    return pltpu.CompilerParams(dimension_semantics=sem, vmem_limit_bytes=VMEM_LIMIT_BYTES)


def _adaln_body(c_ref, w_ref, b_ref, o_ref):
    c = c_ref[...]
    ca = (c * jax.nn.sigmoid(c)).astype(BF16)
    w = w_ref[0].astype(BF16)
    o_ref[0] = jnp.dot(ca, w, preferred_element_type=F32) + b_ref[0]


def _adaln(c_pad, w_ada, b_ada):
    depth, d, n = w_ada.shape
    rows = c_pad.shape[0]
    return pl.pallas_call(
        _adaln_body,
        grid=(depth, n // ADA_TN),
        in_specs=[
            pl.BlockSpec((rows, d), lambda l, j: (0, 0)),
            pl.BlockSpec((1, d, ADA_TN), lambda l, j: (l, 0, j)),
            pl.BlockSpec((1, 1, ADA_TN), lambda l, j: (l, 0, j)),
        ],
        out_specs=pl.BlockSpec((1, rows, ADA_TN), lambda l, j: (l, 0, j)),
        out_shape=jax.ShapeDtypeStruct((depth, rows, n), F32),
        compiler_params=_cparams(("arbitrary", "arbitrary")),
        name="adaln",
    )(c_pad, w_ada, b_ada.reshape(depth, 1, n))


def _rms_mod(x, g, sc, sh):
    ms = jnp.mean(x * x, axis=-1, keepdims=True)
    return (x * lax.rsqrt(ms + NORM_EPS) * g) * (1.0 + sc) + sh


def _pack_bf16_halves(h):
    half = h.shape[1] // 2
    return _pack_bf16_pair(h[:, :half], h[:, half:])


def _pack_bf16_pair(lo, hi):
    lo = lax.bitcast_convert_type(lo.astype(BF16).astype(F32), jnp.uint32)
    hi = lax.bitcast_convert_type(hi.astype(BF16).astype(F32), jnp.uint32)
    return lax.shift_right_logical(lo, jnp.uint32(16)) | (hi & jnp.uint32(0xFFFF0000))


def _load_token_tiles(ref, base, n):
    return jnp.concatenate(
        [ref[pl.ds(base + c, n, stride=TOKEN_TILE_ROWS), :] for c in range(TOKEN_TILE_ROWS)], axis=1)


def _unpack_bf16_halves(p):
    lo = lax.bitcast_convert_type(lax.shift_left(p, jnp.uint32(16)), F32).astype(BF16)
    hi = lax.bitcast_convert_type(p & jnp.uint32(0xFFFF0000), F32).astype(BF16)
    return jnp.concatenate([lo, hi], axis=1)


def _combine(x, y0_ref, y1_ref, wt_ref, gate):
    wt = wt_ref[...]
    n = x.shape[0]
    y0 = _unpack_bf16_halves(_load_token_tiles(y0_ref, 0, n)).astype(F32)
    y1 = _unpack_bf16_halves(_load_token_tiles(y1_ref, 0, n)).astype(F32)
    y = wt[:, 0:1] * y0 + wt[:, 1:2] * y1
    return x + (1.0 + gate) * y


def _gelu_exact(z):
    return 0.5 * z * (1.0 + lax.erf(z * (2.0 ** -0.5)))


def _proj_body(has_prev, *refs):
    if has_prev:
        (x_ref, y0_ref, y1_ref, wt_ref, pmod_ref, mod_ref, g_ref, w_ref, b_ref, cos_ref, sa_ref, sb_ref,
         xo_ref, qkv_ref, z_ref) = refs
        x = _combine(x_ref[...], y0_ref, y1_ref, wt_ref, pmod_ref[0, 5:6, :])
        xo_ref[...] = x
    else:
        x_ref, mod_ref, g_ref, w_ref, b_ref, cos_ref, sa_ref, sb_ref, qkv_ref, z_ref = refs
        x = x_ref[...]
    mod = mod_ref[0]
    h = _rms_mod(x, g_ref[...], mod[1:2, :], mod[0:1, :]).astype(BF16)

    qkv = jnp.dot(h, w_ref[:, :QKV_COLS], preferred_element_type=F32) + b_ref[:, :QKV_COLS]
    cos, sa, sb = cos_ref[...], sa_ref[...], sb_ref[...]
    n_rot = (Q_COLS + KV_COLS) // LANES
    for j in range(QKV_COLS // LANES):
        ch = qkv[:, j * LANES:(j + 1) * LANES]
        if j < n_rot:
            ch = ch * cos + pltpu.roll(ch, LANES - ROT_DIM // 2, 1) * sa + pltpu.roll(ch, ROT_DIM // 2, 1) * sb
        if j < Q_COLS // LANES:
            ch = ch * (HEAD_DIM ** -0.5)
        qkv_ref[:, j * LANES:(j + 1) * LANES] = ch.astype(BF16)

    z = jnp.dot(h, w_ref[:, QKV_COLS:], preferred_element_type=F32) + b_ref[:, QKV_COLS:]
    z_ref[...] = _gelu_exact(z)


def _proj(x, prev, mod_l, g1, w_in, b_in, cos_t, sa_t, sb_t, seq):
    t, d = x.shape
    tm = PROJ_TM
    tiles_per_seq = seq // tm
    row = lambda i: (i, 0)
    const = lambda i: (0, 0)
    modmap = lambda i: (i // tiles_per_seq, 0, 0)
    in_specs = [pl.BlockSpec((tm, d), row)]
    args = [x]
    if prev is not None:
        y2, wts, pmod = prev
        in_specs += [pl.BlockSpec((tm * TOKEN_TILE_ROWS, LANES), row),
                     pl.BlockSpec((tm * TOKEN_TILE_ROWS, LANES), lambda i: (i + t // tm, 0)),
                     pl.BlockSpec((tm, LANES), row), pl.BlockSpec((1, N_MOD, d), modmap)]
        args += [y2, y2, wts, pmod]
    in_specs += [
        pl.BlockSpec((1, N_MOD, d), modmap),
        pl.BlockSpec((1, d), const),
        pl.BlockSpec((d, IN_COLS), const, pipeline_mode=pl.Buffered(1)),
        pl.BlockSpec((1, IN_COLS), const),
        pl.BlockSpec((tm, LANES), row),
        pl.BlockSpec((tm, LANES), row),
        pl.BlockSpec((tm, LANES), row),
    ]
    args += [mod_l, g1, w_in, b_in, cos_t, sa_t, sb_t]
    out_specs = [pl.BlockSpec((tm, QKV_COLS), row), pl.BlockSpec((tm, 2 * GMLP_WIDTH), row)]
    out_shape = [jax.ShapeDtypeStruct((t, QKV_COLS), BF16), jax.ShapeDtypeStruct((t, 2 * GMLP_WIDTH), F32)]
    if prev is not None:
        out_specs = [pl.BlockSpec((tm, d), row)] + out_specs
        out_shape = [jax.ShapeDtypeStruct((t, d), F32)] + out_shape
    outs = pl.pallas_call(
        functools.partial(_proj_body, prev is not None),
        grid=(t // tm,),
        in_specs=in_specs,
        out_specs=out_specs,
        out_shape=out_shape,
        compiler_params=_cparams(("arbitrary",)),
        name="proj",
    )(*args)
    if prev is None:
        return (x,) + tuple(outs)
    return tuple(outs)


def _mixer_body(sink_ref, q_ref, kvc_ref, kvp_ref, z_ref, ws_ref, bst_ref, lng_ref, lnb_ref, ga_ref, gg_ref,
                o_ref):
    n = pl.program_id(1)
    for sub in range(MIX_BLOCKS):
        r0 = sub * BLOCK
        kvp = kvp_ref[...] if sub == 0 else kvc_ref[r0 - BLOCK:r0, :]
        has_prev = (n > 0) if sub == 0 else None
        _mixer_block(sink_ref, q_ref[r0:r0 + BLOCK, :], kvc_ref[r0:r0 + BLOCK, :], kvp, has_prev, z_ref, r0,
                     ws_ref, bst_ref, lng_ref, lnb_ref, ga_ref, gg_ref, o_ref)


def _mixer_block(sink_ref, q, kvc, kvp, has_prev, z_ref, r0, ws_ref, bst_ref, lng_ref, lnb_ref, ga_ref, gg_ref,
                 o_ref):
    rows = slice(r0, r0 + BLOCK)
    kcat = jnp.concatenate([kvp[:, :KV_COLS], kvc[:, :KV_COLS]], axis=0)
    vcat = jnp.concatenate([kvp[:, KV_COLS:], kvc[:, KV_COLS:]], axis=0)

    lane = lax.broadcasted_iota(jnp.int32, (BLOCK, LANES), 1)
    lo = lane < HEAD_DIM
    kj = lax.broadcasted_iota(jnp.int32, (2 * BLOCK, BLOCK), 0)
    qi = lax.broadcasted_iota(jnp.int32, (2 * BLOCK, BLOCK), 1)
    rel = qi + BLOCK - kj
    valid = (rel >= 0) & (rel < WINDOW)
    if has_prev is not None:
        valid = valid & (has_prev | (kj >= BLOCK))
    row_lo = lax.broadcasted_iota(jnp.int32, (LANES, BLOCK), 0) < HEAD_DIM
    zero_q = jnp.zeros((BLOCK, LANES), BF16)
    nt_dims = (((1,), (1,)), ((), ()))

    attn = []
    chunks_per_kv = GQA_GROUP * HEAD_DIM // LANES
    for hk in range(N_KV_HEADS):
        ksl = kcat[:, hk * HEAD_DIM:(hk + 1) * HEAD_DIM]
        vsl = vcat[:, hk * HEAD_DIM:(hk + 1) * HEAD_DIM]
        kk = jnp.concatenate([ksl, ksl], axis=1)
        vt = jnp.concatenate([vsl, vsl], axis=1).astype(F32).T.astype(BF16)
        for j in range(chunks_per_kv):
            c = hk * chunks_per_kv + j
            qc = q[:, c * LANES:(c + 1) * LANES]
            probs, rdenoms = [], []
            for half in range(2):
                qm = jnp.where(lo, qc, zero_q) if half == 0 else jnp.where(lo, zero_q, qc)
                s = lax.dot_general(kk, qm, nt_dims, preferred_element_type=F32)
                s = jnp.where(valid, s, -jnp.inf)
                sink = sink_ref[2 * c + half]
                m = jnp.maximum(jnp.max(s, axis=0, keepdims=True), sink)
                p = jnp.exp(s - m)
                denom = jnp.sum(p, axis=0, keepdims=True) + jnp.exp(sink - m)
                probs.append(p.astype(BF16))
                rdenoms.append(1.0 / denom)
            pn = jnp.concatenate(probs, axis=1)
            ot = jnp.dot(vt, pn, preferred_element_type=F32)
            sel = jnp.where(row_lo, ot[:, :BLOCK] * rdenoms[0], ot[:, BLOCK:] * rdenoms[1])
            attn.append(sel.T)

    ss = attn[0] * attn[0]
    for a in attn[1:]:
        ss = ss + a * a
    scale = lax.rsqrt(jnp.sum(ss, axis=-1, keepdims=True) * (1.0 / ATTN_WIDTH) + NORM_EPS)
    for c, a in enumerate(attn):
        o_ref[rows, c * LANES:(c + 1) * LANES] = (a * scale * ga_ref[:, c * LANES:(c + 1) * LANES]).astype(BF16)

    u = z_ref[rows, :GMLP_WIDTH]
    v = z_ref[rows, GMLP_WIDTH:]
    mu = jnp.mean(v, axis=-1, keepdims=True)
    vc = v - mu
    var = jnp.mean(vc * vc, axis=-1, keepdims=True)
    vn = ((vc * lax.rsqrt(var + NORM_EPS)) * lng_ref[...] + lnb_ref[...]).astype(BF16)
    ti = lax.broadcasted_iota(jnp.int32, (BLOCK, BLOCK), 0)
    si = lax.broadcasted_iota(jnp.int32, (BLOCK, BLOCK), 1)
    tril = si <= ti
    bst = bst_ref[...]
    gm = []
    for h in range(GMLP_GROUPS):
        w = jnp.where(tril, ws_ref[h], 0.0).astype(BF16)
        sp = jnp.dot(w, vn[:, h * LANES:(h + 1) * LANES], preferred_element_type=F32) + bst[:, h:h + 1]
        gm.append(u[:, h * LANES:(h + 1) * LANES] * sp)
    ss = gm[0] * gm[0]
    for a in gm[1:]:
        ss = ss + a * a
    scale = lax.rsqrt(jnp.sum(ss, axis=-1, keepdims=True) * (1.0 / GMLP_WIDTH) + NORM_EPS)
    for c, a in enumerate(gm):
        o_ref[rows, ATTN_WIDTH + c * LANES:ATTN_WIDTH + (c + 1) * LANES] = (
            a * scale * gg_ref[:, c * LANES:(c + 1) * LANES]).astype(BF16)


def _mixer(qkv, z, sinks, w_sp, bst, ln_g, ln_b, g_attn, g_gmlp, batch, seq):
    t = qkv.shape[0]
    nb = seq // BLOCK
    steps = nb // MIX_BLOCKS
    rows = MIX_BLOCKS * BLOCK
    kv_blk = Q_COLS // (2 * KV_COLS)
    cur = lambda b, n, s: (b * steps + n, 0)
    const2 = lambda b, n, s: (0, 0)
    grid_spec = pltpu.PrefetchScalarGridSpec(
        num_scalar_prefetch=1,
        grid=(batch, steps),
        in_specs=[
            pl.BlockSpec((rows, Q_COLS), cur),
            pl.BlockSpec((rows, 2 * KV_COLS), lambda b, n, s: (b * steps + n, kv_blk)),
            pl.BlockSpec((BLOCK, 2 * KV_COLS),
                         lambda b, n, s: (b * nb + jnp.maximum(MIX_BLOCKS * n - 1, 0), kv_blk)),
            pl.BlockSpec((rows, 2 * GMLP_WIDTH), cur),
            pl.BlockSpec((GMLP_GROUPS, BLOCK, BLOCK), lambda b, n, s: (0, 0, 0)),
            pl.BlockSpec((BLOCK, GMLP_GROUPS), const2),
            pl.BlockSpec((1, GMLP_WIDTH), const2),
            pl.BlockSpec((1, GMLP_WIDTH), const2),
            pl.BlockSpec((1, ATTN_WIDTH), const2),
            pl.BlockSpec((1, GMLP_WIDTH), const2),
        ],
        out_specs=pl.BlockSpec((rows, D_MODEL), cur),
    )
    return pl.pallas_call(
        _mixer_body,
        grid_spec=grid_spec,
        out_shape=jax.ShapeDtypeStruct((t, D_MODEL), BF16),
        compiler_params=_cparams(("arbitrary", "arbitrary")),
        name="mixer",
    )(sinks, qkv, qkv, qkv, z, w_sp, bst, ln_g, ln_b, g_attn, g_gmlp)


def _first_index_of_max(vals, lane, width):
    mx = jnp.max(vals, axis=-1, keepdims=True)
    idx = jnp.min(jnp.where(vals == mx, lane, width), axis=-1, keepdims=True)
    return mx, idx


def _outproj_body(m_ref, x_ref, mod_ref, g_ref, w_ref, wr_ref, br_ref,
                  xo_ref, h_ref, ids_ref, wt_ref, cnt_ref, carry_ref):
    i = pl.program_id(0)
    tm = x_ref.shape[0]

    @pl.when(i == 0)
    def _():
        carry_ref[...] = jnp.zeros_like(carry_ref)

    mod = mod_ref[0]
    y = jnp.dot(m_ref[...], w_ref[...], preferred_element_type=F32)
    x = x_ref[...] + (1.0 + mod[2:3, :]) * y
    xo_ref[...] = x
    h = _rms_mod(x, g_ref[...], mod[4:5, :], mod[3:4, :])
    packed = _pack_bf16_halves(h)
    for c in range(TOKEN_TILE_ROWS):
        h_ref[pl.ds(c, tm, stride=TOKEN_TILE_ROWS), :] = packed[:, c * LANES:(c + 1) * LANES]

    logits = jnp.dot(h.astype(BF16), wr_ref[...], preferred_element_type=F32) + br_ref[...]
    lane = lax.broadcasted_iota(jnp.int32, (tm, LANES), 1)
    neg = -jnp.inf
    gl = jnp.where(lane < N_GROUPS, logits, neg)
    ge = jnp.exp(gl - jnp.max(gl, axis=-1, keepdims=True))
    gprob = ge / jnp.sum(ge, axis=-1, keepdims=True)
    g_p, g_idx = _first_index_of_max(gprob, lane, LANES)

    base = N_GROUPS + EXPERTS_PER_GROUP * g_idx
    in_grp = (lane >= base) & (lane < base + EXPERTS_PER_GROUP)
    el = jnp.where(in_grp, logits, neg)
    ee = jnp.exp(el - jnp.max(el, axis=-1, keepdims=True))
    eprob = jnp.where(in_grp, ee / jnp.sum(ee, axis=-1, keepdims=True), -1.0)
    p1, i1 = _first_index_of_max(eprob, lane, LANES)
    eprob2 = jnp.where(lane == i1, -1.0, eprob)
    p2, i2 = _first_index_of_max(eprob2, lane, LANES)
    psum = p1 + p2
    w1 = g_p * (p1 / psum)
    w2 = g_p * (p2 / psum)
    e1 = i1 - N_GROUPS
    e2 = i2 - N_GROUPS

    oh1 = lane == e1
    oh2 = lane == e2
    onehot = jnp.where(oh1 | oh2, 1.0, 0.0)
    ri = lax.broadcasted_iota(jnp.int32, (tm, tm), 0)
    ci = lax.broadcasted_iota(jnp.int32, (tm, tm), 1)
    lower = jnp.where(ci < ri, 1.0, 0.0).astype(BF16)
    before = jnp.dot(lower, onehot.astype(BF16), preferred_element_type=F32) + carry_ref[...]
    r1 = jnp.sum(jnp.where(oh1, before, 0.0), axis=-1, keepdims=True)
    r2 = jnp.sum(jnp.where(oh2, before, 0.0), axis=-1, keepdims=True)
    carry = carry_ref[...] + jnp.sum(onehot, axis=0, keepdims=True)
    carry_ref[...] = carry
    cnt_ref[...] = jnp.broadcast_to(carry, cnt_ref.shape).astype(jnp.int32)

    ids = jnp.where(lane == 0, e1, jnp.where(lane == 1, e2, 0))
    ranks = jnp.where(lane == 0, r1, jnp.where(lane == 1, r2, 0.0)).astype(jnp.int32)
    ids_ref[...] = lax.shift_left(ids, ROUTE_SHIFT) | ranks
    wt_ref[...] = jnp.where(lane == 0, w1, jnp.where(lane == 1, w2, 0.0))


def _outproj(merged, x, mod_l, g2, w_out, w_router, b_router, seq):
    t, d = x.shape
    tm = OUT_TM
    tiles_per_seq = seq // tm
    row = lambda i: (i, 0)
    const = lambda i: (0, 0)
    return pl.pallas_call(
        _outproj_body,
        grid=(t // tm,),
        in_specs=[
            pl.BlockSpec((tm, d), row),
            pl.BlockSpec((tm, d), row),
            pl.BlockSpec((1, N_MOD, d), lambda i: (i // tiles_per_seq, 0, 0)),
            pl.BlockSpec((1, d), const),
            pl.BlockSpec((d, d), const, pipeline_mode=pl.Buffered(1)),
            pl.BlockSpec((d, LANES), const),
            pl.BlockSpec((1, LANES), const),
        ],
        out_specs=[
            pl.BlockSpec((tm, d), row),
            pl.BlockSpec((tm * TOKEN_TILE_ROWS, LANES), row),
            pl.BlockSpec((tm, LANES), row),
            pl.BlockSpec((tm, LANES), row),
            pl.BlockSpec((SUBLANES, LANES), const),
        ],
        out_shape=[
            jax.ShapeDtypeStruct((t, d), F32),
            jax.ShapeDtypeStruct((t * TOKEN_TILE_ROWS, LANES), jnp.uint32),
            jax.ShapeDtypeStruct((t, LANES), jnp.int32),
            jax.ShapeDtypeStruct((t, LANES), F32),
            jax.ShapeDtypeStruct((SUBLANES, LANES), jnp.int32),
        ],
        scratch_shapes=[pltpu.VMEM((1, LANES), F32)],
        compiler_params=_cparams(("arbitrary",)),
        name="outproj",
    )(merged, x, mod_l, g2, w_out, w_router, b_router)


def _moe_body(layer, n_slots, te_ref, nx_ref, nu_ref, h_hbm, src_hbm, dst_hbm, wg_hbm, wu_hbm, wd_hbm, y_hbm,
              first_smem, plan_smem, xbuf, obuf, wgs, wus, wds, wgb, wub, wdb,
              first_sem, plan_sem, g_sem, s_sem, w_sem):
    i = pl.program_id(0)
    n_used = nu_ref[0]
    nb = MOE_BUFS
    tm = plan_smem.shape[2]
    d, de = wgb.shape

    def weight_copies(e):
        return (pltpu.make_async_copy(wg_hbm.at[layer, e], wgs, w_sem.at[0]),
                pltpu.make_async_copy(wu_hbm.at[layer, e], wus, w_sem.at[1]),
                pltpu.make_async_copy(wd_hbm.at[layer, e], wds, w_sem.at[2]))
    b0 = i % nb
    b1 = (i + 1) % nb
    b2 = (i + 2) % nb
    p0 = i % 2
    p1 = 1 - p0

    class _PlanCopy:
        def __init__(self, k, slot):
            self.copies = (
                pltpu.make_async_copy(src_hbm.at[k + 3], plan_smem.at[slot, 0], plan_sem.at[slot, 0]),
                pltpu.make_async_copy(dst_hbm.at[k], plan_smem.at[slot, 1], plan_sem.at[slot, 1]))

        def start(self):
            for cp in self.copies:
                cp.start()

        def wait(self):
            for cp in self.copies:
                cp.wait()

    plan_copy = _PlanCopy

    xrows = tm * TOKEN_TILE_ROWS

    def token_tile(row):
        return pl.ds(pl.multiple_of(row, TOKEN_TILE_ROWS), TOKEN_TILE_ROWS)

    def tile_buf(buf):
        return pl.ds(pl.multiple_of(buf * xrows, xrows), xrows)

    def gather_row(src, buf, j):
        return pltpu.make_async_copy(h_hbm.at[token_tile(src), :],
                                     xbuf.at[token_tile(buf * xrows + j * TOKEN_TILE_ROWS), :], g_sem.at[buf])

    def scatter_row(dst, buf, j):
        return pltpu.make_async_copy(obuf.at[token_tile(buf * xrows + j * TOKEN_TILE_ROWS), :],
                                     y_hbm.at[token_tile(dst), :], s_sem.at[buf])

    def wait_gather(buf):
        pltpu.make_async_copy(h_hbm.at[pl.ds(0, xrows), :], xbuf.at[tile_buf(buf), :], g_sem.at[buf]).wait()

    def wait_scatter(buf):
        pltpu.make_async_copy(obuf.at[tile_buf(buf), :], y_hbm.at[pl.ds(0, xrows), :], s_sem.at[buf]).wait()

    @pl.when(i == 0)
    def _():
        first = [pltpu.make_async_copy(src_hbm.at[1 + tile], first_smem.at[tile], first_sem.at[tile])
                 for tile in range(2)]
        for cp in first:
            cp.start()
        plan_copy(0, 0).start()
        for cp in weight_copies(te_ref[0]):
            cp.start()
        for cp in first:
            cp.wait()
        for tile in range(2):
            def body(j, carry, tile=tile):
                gather_row(first_smem[tile, j], tile, j).start()
                return carry
            lax.fori_loop(0, tm, body, 0, unroll=8)
        obuf[pl.ds((nb - 1) * xrows, xrows), :] = jnp.zeros((xrows, LANES), jnp.uint32)

    @pl.when(i < n_used)
    def _():
        plan_copy(i, p0).wait()
        plan_copy(i + 1, p1).start()
        wait_gather(b0)

        @pl.when(i >= 2)
        def _():
            wait_scatter(b0)

        new_expert = jnp.logical_or(i == 0, te_ref[i] != te_ref[jnp.maximum(i - 1, 0)])

        @pl.when(new_expert)
        def _():
            for cp in weight_copies(te_ref[i]):
                cp.wait()
            wgb[...] = wgs[...].astype(BF16)
            wub[...] = wus[...].astype(BF16)
            wdb[...] = wds[...].astype(BF16)

            @pl.when(nx_ref[i] != te_ref[i])
            def _():
                for cp in weight_copies(nx_ref[i]):
                    cp.start()

        copies = []
        for j in range(tm):
            copies += [(0, j), (1, j)]
        n_up = de // MOE_N_CHUNK
        n_down = d // MOE_N_CHUNK
        n_pieces = 2 * n_up + n_down
        per_piece = -(-len(copies) // n_pieces)

        def issue(piece):
            for kind, j in copies[piece * per_piece:(piece + 1) * per_piece]:
                if kind == 0:
                    gather_row(plan_smem[p0, 0, j], b2, j).start()
                else:
                    scatter_row(plan_smem[p0, 1, j], b2, j).start()

        xbase = pl.multiple_of(b0 * xrows, xrows)
        xb = _unpack_bf16_halves(_load_token_tiles(xbuf, xbase, tm))
        piece = 0
        acts = []
        for n in range(n_up):
            cols = slice(n * MOE_N_CHUNK, (n + 1) * MOE_N_CHUNK)
            g = jnp.dot(xb, wgb[:, cols], preferred_element_type=F32)
            issue(piece)
            u = jnp.dot(xb, wub[:, cols], preferred_element_type=F32)
            issue(piece + 1)
            piece += 2
            acts.append(((g * jax.nn.sigmoid(g)) * u).astype(BF16))
        a = jnp.concatenate(acts, axis=1)
        for n in range(n_down // 2):
            cols_lo = slice(n * MOE_N_CHUNK, (n + 1) * MOE_N_CHUNK)
            cols_hi = slice(d // 2 + n * MOE_N_CHUNK, d // 2 + (n + 1) * MOE_N_CHUNK)
            y_lo = jnp.dot(a, wdb[:, cols_lo], preferred_element_type=F32)
            issue(piece)
            y_hi = jnp.dot(a, wdb[:, cols_hi], preferred_element_type=F32)
            issue(piece + 1)
            piece += 2
            packed = _pack_bf16_pair(y_lo, y_hi)
            for c in range(MOE_N_CHUNK // LANES):
                obuf[pl.ds(xbase + n * (MOE_N_CHUNK // LANES) + c, tm, stride=TOKEN_TILE_ROWS), :] = (
                    packed[:, c * LANES:(c + 1) * LANES])

    @pl.when(i == n_used)
    def _():
        plan_copy(i, p0).wait()
        wait_gather(b0)
        wait_gather(b1)
        wait_scatter(b1)

        @pl.when(i >= 2)
        def _():
            wait_scatter(b0)

        def body(j, carry):
            scatter_row(plan_smem[p0, 1, j], b2, j).start()
            return carry
        lax.fori_loop(0, tm, body, 0, unroll=8)
        wait_scatter(b2)
        obuf[tile_buf(b0), :] = jnp.zeros((xrows, LANES), jnp.uint32)
        for k in range(nb):
            fill = pltpu.make_async_copy(obuf.at[tile_buf(b0), :],
                                         y_hbm.at[pl.ds((n_slots + k * tm) * TOKEN_TILE_ROWS, xrows), :], s_sem.at[b0])
            fill.start()
            fill.wait()


def _moe(layer, h2, src2d, dst2d, tile_expert, next_expert, n_used, w_gate, w_up, w_down):
    t = h2.shape[0] // TOKEN_TILE_ROWS
    n_slots = 2 * t
    steps, tm = dst2d.shape[0] - MOE_BUFS, dst2d.shape[1]
    d, de = w_gate.shape[-2:]
    hbm = pl.BlockSpec(memory_space=pl.ANY)
    grid_spec = pltpu.PrefetchScalarGridSpec(
        num_scalar_prefetch=3,
        grid=(steps,),
        in_specs=[hbm] * 6,
        out_specs=hbm,
        scratch_shapes=[
            pltpu.SMEM((2, tm), jnp.int32),
            pltpu.SMEM((2, 2, tm), jnp.int32),
            pltpu.VMEM((MOE_BUFS * tm * TOKEN_TILE_ROWS, LANES), jnp.uint32),
            pltpu.VMEM((MOE_BUFS * tm * TOKEN_TILE_ROWS, LANES), jnp.uint32),
            pltpu.VMEM((d, de), F32),
            pltpu.VMEM((d, de), F32),
            pltpu.VMEM((de, d), F32),
            pltpu.VMEM((d, de), BF16),
            pltpu.VMEM((d, de), BF16),
            pltpu.VMEM((de, d), BF16),
            pltpu.SemaphoreType.DMA((2,)),
            pltpu.SemaphoreType.DMA((2, 2)),
            pltpu.SemaphoreType.DMA((MOE_BUFS,)),
            pltpu.SemaphoreType.DMA((MOE_BUFS,)),
            pltpu.SemaphoreType.DMA((3,)),
        ],
    )
    return pl.pallas_call(
        functools.partial(_moe_body, layer, n_slots),
        grid_spec=grid_spec,
        out_shape=jax.ShapeDtypeStruct(((n_slots + MOE_BUFS * tm) * TOKEN_TILE_ROWS, LANES), jnp.uint32),
        compiler_params=_cparams(("arbitrary",)),
        name="moe",
    )(tile_expert, next_expert, n_used, h2, src2d, dst2d, w_gate, w_up, w_down)


def _route_plan(ids, counts, t):
    tm = MOE_TM
    nt = (2 * t) // tm + N_EXPERTS
    cnt = counts[0, :N_EXPERTS]
    tiles = (cnt + tm - 1) // tm
    tile_end = jnp.cumsum(tiles)
    off = ((tile_end - tiles) * tm).astype(jnp.int32)
    tile_buf = np.arange(-1, nt) % MOE_BUFS
    dump = (2 * t + tile_buf[:, None] * tm + np.arange(tm)[None, :]).astype(np.int32)
    init = np.concatenate([dump, np.zeros((MOE_BUFS, tm), np.int32)], axis=0)
    dst2d = _plan_scatter(off, ids[:, 0:2].reshape(-1), jnp.asarray(init), t)
    src2d = jnp.where(dst2d < t, dst2d, jnp.where(dst2d < 2 * t, dst2d - t, 0)) * TOKEN_TILE_ROWS
    n_used = tile_end[-1]
    steps = jnp.minimum(jnp.arange(nt + 1, dtype=jnp.int32), n_used - 1)
    tile_expert = jnp.sum(tile_end[None, :] <= steps[:, None], axis=1).astype(jnp.int32)
    eid = jnp.arange(N_EXPERTS, dtype=jnp.int32)
    later_used = (tiles[None, :] > 0) & (eid[None, :] > eid[:, None])
    nxt = jnp.min(jnp.where(later_used, eid[None, :], N_EXPERTS), axis=1)
    next_expert = jnp.where(nxt == N_EXPERTS, eid, nxt)[tile_expert].astype(jnp.int32)
    return src2d, dst2d * TOKEN_TILE_ROWS, tile_expert, next_expert, n_used.reshape(1).astype(jnp.int32)


PLAN_CHUNK = 8192
ROUTE_SHIFT = 20


def _plan_body(t, off_ref, ids_hbm, init_hbm, out_hbm, tab, ids_a, ids_b, init_sem, ids_sem, out_sem):
    chunk = ids_a.shape[0]
    n_chunks = (2 * t) // chunk
    bufs = (ids_a, ids_b)

    def ids_copy(c):
        return pltpu.make_async_copy(ids_hbm.at[pl.ds(c * chunk, chunk)], bufs[c % 2], ids_sem.at[c % 2])

    init = pltpu.make_async_copy(init_hbm, tab, init_sem.at[0])
    init.start()
    ids_copy(0).start()
    init.wait()
    for c in range(n_chunks):
        ids_copy(c).wait()
        if c + 1 < n_chunks:
            ids_copy(c + 1).start()
        buf = bufs[c % 2]
        tok0 = c * (chunk // 2)

        def body(tt, carry, buf=buf, tok0=tok0):
            tok = tok0 + tt
            for k in range(2):
                word = buf[2 * tt + k]
                expert = lax.shift_right_logical(word, ROUTE_SHIFT)
                rank = word & ((1 << ROUTE_SHIFT) - 1)
                tab[off_ref[expert] + rank] = tok + k * t
            return carry
        lax.fori_loop(0, chunk // 2, body, 0, unroll=8)
    out = pltpu.make_async_copy(tab, out_hbm, out_sem.at[0])
    out.start()
    out.wait()


def _plan_scatter(off, ids_flat, dump, t):
    rows, tm = dump.shape
    chunk = min(PLAN_CHUNK, 2 * t)
    assert (2 * t) % chunk == 0
    words = -(-rows * tm // 1024) * 1024
    init = jnp.concatenate([dump.reshape(-1), jnp.zeros((words - rows * tm,), jnp.int32)])
    hbm = pl.BlockSpec(memory_space=pl.ANY)
    grid_spec = pltpu.PrefetchScalarGridSpec(
        num_scalar_prefetch=1,
        grid=(1,),
        in_specs=[hbm, hbm],
        out_specs=hbm,
        scratch_shapes=[
            pltpu.SMEM((words,), jnp.int32),
            pltpu.SMEM((chunk,), jnp.int32),
            pltpu.SMEM((chunk,), jnp.int32),
            pltpu.SemaphoreType.DMA((1,)),
            pltpu.SemaphoreType.DMA((2,)),
            pltpu.SemaphoreType.DMA((1,)),
        ],
    )
    flat = pl.pallas_call(
        functools.partial(_plan_body, t),
        grid_spec=grid_spec,
        out_shape=jax.ShapeDtypeStruct((words,), jnp.int32),
        compiler_params=_cparams(("arbitrary",)),
        name="plan",
    )(off + tm, ids_flat, init)
    return flat[:rows * tm].reshape(rows, tm)


def _final_body(x_ref, y0_ref, y1_ref, wt_ref, pmod_ref, g_ref, o_ref):
    x = _combine(x_ref[...], y0_ref, y1_ref, wt_ref, pmod_ref[0, 5:6, :])
    ms = jnp.mean(x * x, axis=-1, keepdims=True)
    o_ref[...] = x * lax.rsqrt(ms + NORM_EPS) * g_ref[...]


def _final(x, y2, wts, pmod, g_final, seq):
    t, d = x.shape
    tm = FINAL_TM
    tiles_per_seq = seq // tm
    row = lambda i: (i, 0)
    return pl.pallas_call(
        _final_body,
        grid=(t // tm,),
        in_specs=[
            pl.BlockSpec((tm, d), row),
            pl.BlockSpec((tm * TOKEN_TILE_ROWS, LANES), row),
            pl.BlockSpec((tm * TOKEN_TILE_ROWS, LANES), lambda i: (i + t // tm, 0)),
            pl.BlockSpec((tm, LANES), row),
            pl.BlockSpec((1, N_MOD, d), lambda i: (i // tiles_per_seq, 0, 0)),
            pl.BlockSpec((1, d), lambda i: (0, 0)),
        ],
        out_specs=pl.BlockSpec((tm, d), row),
        out_shape=jax.ShapeDtypeStruct((t, d), F32),
        compiler_params=_cparams(("arbitrary",)),
        name="final",
    )(x, y2, y2, wts, pmod, g_final)


def _rotary_tables(positions):
    half = ROT_DIM // 2
    inv_freq = ROPE_THETA ** (-(jnp.arange(0, ROT_DIM, 2, dtype=F32) / ROT_DIM))
    ang = positions.reshape(-1).astype(F32)[:, None] * inv_freq
    cos, sin = jnp.cos(ang), jnp.sin(ang)
    t = ang.shape[0]
    pad = HEAD_DIM - ROT_DIM
    cos_h = jnp.concatenate([cos, cos, jnp.ones((t, pad), F32)], axis=1)
    sa_h = jnp.concatenate([-sin, jnp.zeros((t, HEAD_DIM - half), F32)], axis=1)
    sb_h = jnp.concatenate([jnp.zeros((t, half), F32), sin, jnp.zeros((t, pad), F32)], axis=1)
    rep = LANES // HEAD_DIM
    return jnp.tile(cos_h, (1, rep)), jnp.tile(sa_h, (1, rep)), jnp.tile(sb_h, (1, rep))


def kernel(x, c, positions, w_ada, b_ada, g_norm1, w_in, b_in, sinks, ln_v_g, ln_v_b, w_spatial, b_spatial,
           g_attn_out, g_gmlp_out, w_out, g_norm2, w_router_group, b_router_group, w_router_expert,
           b_router_expert, w_e_gate, w_e_up, w_e_down, g_final):
    batch, seq, d = x.shape
    depth = w_ada.shape[0]
    t = batch * seq
    assert d == D_MODEL and seq % max(PROJ_TM, OUT_TM, FINAL_TM, MIX_BLOCKS * BLOCK) == 0 and (2 * t) % MOE_TM == 0

    c_pad = jnp.zeros((SUBLANES, d), F32).at[:batch].set(c)
    mod = _adaln(c_pad, w_ada, b_ada).reshape(depth, SUBLANES, N_MOD, d)
    cos_t, sa_t, sb_t = _rotary_tables(positions)

    n_route = N_GROUPS + N_EXPERTS
    w_router = jnp.concatenate(
        [w_router_group, w_router_expert, jnp.zeros((depth, d, LANES - n_route), F32)], axis=-1)
    b_router = jnp.concatenate(
        [b_router_group, b_router_expert, jnp.zeros((depth, LANES - n_route), F32)], axis=-1)

    xt = x.reshape(t, d)
    prev = None
    for l in range(depth):
        xt, qkv, z = _proj(xt, prev, mod[l], g_norm1[l][None], w_in[l].astype(BF16), b_in[l][None],
                           cos_t, sa_t, sb_t, seq)
        merged = _mixer(qkv, z, sinks[l], w_spatial[l], b_spatial[l].T, ln_v_g[l][None], ln_v_b[l][None],
                        g_attn_out[l][None], g_gmlp_out[l][None], batch, seq)
        xt, h2, ids, wts, counts = _outproj(merged, xt, mod[l], g_norm2[l][None], w_out[l].astype(BF16),
                                            w_router[l].astype(BF16), b_router[l][None], seq)
        src2d, dst2d, tile_expert, next_expert, n_used = _route_plan(ids, counts, t)
        y2 = _moe(l, h2, src2d, dst2d, tile_expert, next_expert, n_used, w_e_gate, w_e_up, w_e_down)
        prev = (y2, wts, mod[l])
    out = _final(xt, prev[0], prev[1], prev[2], g_final[None], seq)
    return out.reshape(batch, seq, d)
```

```python
import functools

import jax
import jax.numpy as jnp
import numpy as np
from jax import lax
from jax.experimental import pallas as pl
from jax.experimental.pallas import tpu as pltpu

F32 = jnp.float32
BF16 = jnp.bfloat16

D_MODEL = 2048
ATTN_WIDTH = D_MODEL // 2
HEAD_DIM = 64
N_Q_HEADS = ATTN_WIDTH // HEAD_DIM
N_KV_HEADS = 2
GQA_GROUP = N_Q_HEADS // N_KV_HEADS
WINDOW = 128
BLOCK = 128
ROT_DIM = HEAD_DIM // 4
ROPE_THETA = 500000.0
GMLP_WIDTH = D_MODEL - ATTN_WIDTH
GMLP_GROUPS = 8
GMLP_GROUP_DIM = GMLP_WIDTH // GMLP_GROUPS
Q_COLS = N_Q_HEADS * HEAD_DIM
KV_COLS = N_KV_HEADS * HEAD_DIM
QKV_COLS = Q_COLS + 2 * KV_COLS
IN_COLS = QKV_COLS + 2 * GMLP_WIDTH
N_GROUPS = 4
EXPERTS_PER_GROUP = 8
N_EXPERTS = N_GROUPS * EXPERTS_PER_GROUP
D_EXPERT = D_MODEL // 4
NORM_EPS = 1e-6
N_MOD = 6
LOG2_E = 1.4426950408889634

LANES = 128
SUBLANES = 8
VMEM_LIMIT_BYTES = 56 * 1024 * 1024
TOKEN_TILE_ROWS = D_MODEL // 2 // LANES

ADA_TN = 1024
PROJ_TM = 512
OUT_TM = 512
MIX_BLOCKS = 4
MOE_TM = 256
MOE_N_CHUNK = 256
MOE_BUFS = 3
FINAL_TM = 512


def _cparams(sem):
    return pltpu.CompilerParams(dimension_semantics=sem, vmem_limit_bytes=VMEM_LIMIT_BYTES)


def _adaln_body(c_ref, w_ref, b_ref, o_ref):
    c = c_ref[...]
    ca = (c * jax.nn.sigmoid(c)).astype(BF16)
    w = w_ref[0].astype(BF16)
    o_ref[0] = jnp.dot(ca, w, preferred_element_type=F32) + b_ref[0]


def _adaln(c_pad, w_ada, b_ada):
    depth, d, n = w_ada.shape
    rows = c_pad.shape[0]
    return pl.pallas_call(
        _adaln_body,
        grid=(depth, n // ADA_TN),
        in_specs=[
            pl.BlockSpec((rows, d), lambda l, j: (0, 0)),
            pl.BlockSpec((1, d, ADA_TN), lambda l, j: (l, 0, j)),
            pl.BlockSpec((1, 1, ADA_TN), lambda l, j: (l, 0, j)),
        ],
        out_specs=pl.BlockSpec((1, rows, ADA_TN), lambda l, j: (l, 0, j)),
        out_shape=jax.ShapeDtypeStruct((depth, rows, n), F32),
        compiler_params=_cparams(("arbitrary", "arbitrary")),
        name="adaln",
    )(c_pad, w_ada, b_ada.reshape(depth, 1, n))


def _rms_mod(x, g, sc, sh):
    ms = jnp.mean(x * x, axis=-1, keepdims=True)
    return (x * lax.rsqrt(ms + NORM_EPS) * g) * (1.0 + sc) + sh


def _pack_bf16_halves(h):
    half = h.shape[1] // 2
    return _pack_bf16_pair(h[:, :half], h[:, half:])


def _pack_bf16_pair(lo, hi):
    lo = lax.bitcast_convert_type(lo.astype(BF16).astype(F32), jnp.uint32)
    hi = lax.bitcast_convert_type(hi.astype(BF16).astype(F32), jnp.uint32)
    return lax.shift_right_logical(lo, jnp.uint32(16)) | (hi & jnp.uint32(0xFFFF0000))


def _load_token_tiles(ref, base, n):
    return jnp.concatenate(
        [ref[pl.ds(base + c, n, stride=TOKEN_TILE_ROWS), :] for c in range(TOKEN_TILE_ROWS)], axis=1)


def _unpack_bf16_halves(p):
    lo = lax.bitcast_convert_type(lax.shift_left(p, jnp.uint32(16)), F32).astype(BF16)
    hi = lax.bitcast_convert_type(p & jnp.uint32(0xFFFF0000), F32).astype(BF16)
    return jnp.concatenate([lo, hi], axis=1)


def _combine(x, y0_ref, y1_ref, wt_ref, gate):
    wt = wt_ref[...]
    n = x.shape[0]
    y0 = _unpack_bf16_halves(_load_token_tiles(y0_ref, 0, n)).astype(F32)
    y1 = _unpack_bf16_halves(_load_token_tiles(y1_ref, 0, n)).astype(F32)
    y = wt[:, 0:1] * y0 + wt[:, 1:2] * y1
    return x + (1.0 + gate) * y


def _gelu_exact(z):
    return 0.5 * z * (1.0 + lax.erf(z * (2.0 ** -0.5)))


def _proj_body(has_prev, *refs):
    if has_prev:
        (x_ref, y0_ref, y1_ref, wt_ref, pmod_ref, mod_ref, g_ref, w_ref, b_ref, cos_ref, sa_ref, sb_ref,
         xo_ref, qkv_ref, z_ref) = refs
        x = _combine(x_ref[...], y0_ref, y1_ref, wt_ref, pmod_ref[0, 5:6, :])
        xo_ref[...] = x
    else:
        x_ref, mod_ref, g_ref, w_ref, b_ref, cos_ref, sa_ref, sb_ref, qkv_ref, z_ref = refs
        x = x_ref[...]
    mod = mod_ref[0]
    h = _rms_mod(x, g_ref[...], mod[1:2, :], mod[0:1, :]).astype(BF16)

    qkv = jnp.dot(h, w_ref[:, :QKV_COLS], preferred_element_type=F32) + b_ref[:, :QKV_COLS]
    cos, sa, sb = cos_ref[...], sa_ref[...], sb_ref[...]
    n_rot = (Q_COLS + KV_COLS) // LANES
    for j in range(QKV_COLS // LANES):
        ch = qkv[:, j * LANES:(j + 1) * LANES]
        if j < n_rot:
            ch = ch * cos + pltpu.roll(ch, LANES - ROT_DIM // 2, 1) * sa + pltpu.roll(ch, ROT_DIM // 2, 1) * sb
        if j < Q_COLS // LANES:
            ch = ch * (HEAD_DIM ** -0.5 * LOG2_E)
        qkv_ref[:, j * LANES:(j + 1) * LANES] = ch.astype(BF16)

    z = jnp.dot(h, w_ref[:, QKV_COLS:], preferred_element_type=F32) + b_ref[:, QKV_COLS:]
    z_ref[...] = _gelu_exact(z)


def _proj(x, prev, mod_l, g1, w_in, b_in, cos_t, sa_t, sb_t, seq):
    t, d = x.shape
    tm = PROJ_TM
    tiles_per_seq = seq // tm
    row = lambda i: (i, 0)
    const = lambda i: (0, 0)
    modmap = lambda i: (i // tiles_per_seq, 0, 0)
    in_specs = [pl.BlockSpec((tm, d), row)]
    args = [x]
    if prev is not None:
        y2, wts, pmod = prev
        in_specs += [pl.BlockSpec((tm * TOKEN_TILE_ROWS, LANES), row),
                     pl.BlockSpec((tm * TOKEN_TILE_ROWS, LANES), lambda i: (i + t // tm, 0)),
                     pl.BlockSpec((tm, LANES), row), pl.BlockSpec((1, N_MOD, d), modmap)]
        args += [y2, y2, wts, pmod]
    in_specs += [
        pl.BlockSpec((1, N_MOD, d), modmap),
        pl.BlockSpec((1, d), const),
        pl.BlockSpec((d, IN_COLS), const, pipeline_mode=pl.Buffered(1)),
        pl.BlockSpec((1, IN_COLS), const),
        pl.BlockSpec((tm, LANES), row),
        pl.BlockSpec((tm, LANES), row),
        pl.BlockSpec((tm, LANES), row),
    ]
    args += [mod_l, g1, w_in, b_in, cos_t, sa_t, sb_t]
    out_specs = [pl.BlockSpec((tm, QKV_COLS), row), pl.BlockSpec((tm, 2 * GMLP_WIDTH), row)]
    out_shape = [jax.ShapeDtypeStruct((t, QKV_COLS), BF16), jax.ShapeDtypeStruct((t, 2 * GMLP_WIDTH), F32)]
    if prev is not None:
        out_specs = [pl.BlockSpec((tm, d), row)] + out_specs
        out_shape = [jax.ShapeDtypeStruct((t, d), F32)] + out_shape
    outs = pl.pallas_call(
        functools.partial(_proj_body, prev is not None),
        grid=(t // tm,),
        in_specs=in_specs,
        out_specs=out_specs,
        out_shape=out_shape,
        compiler_params=_cparams(("arbitrary",)),
        name="proj",
    )(*args)
    if prev is None:
        return (x,) + tuple(outs)
    return tuple(outs)


def _mixer_body(sink_ref, q_ref, kvc_ref, kvp_ref, z_ref, ws_ref, bst_ref, lng_ref, lnb_ref, ga_ref, gg_ref,
                o_ref):
    n = pl.program_id(1)
    for sub in range(MIX_BLOCKS):
        r0 = sub * BLOCK
        kvp = kvp_ref[...] if sub == 0 else kvc_ref[r0 - BLOCK:r0, :]
        has_prev = (n > 0) if sub == 0 else None
        _mixer_block(sink_ref, q_ref[r0:r0 + BLOCK, :], kvc_ref[r0:r0 + BLOCK, :], kvp, has_prev, z_ref, r0,
                     ws_ref, bst_ref, lng_ref, lnb_ref, ga_ref, gg_ref, o_ref)


def _mixer_block(sink_ref, q, kvc, kvp, has_prev, z_ref, r0, ws_ref, bst_ref, lng_ref, lnb_ref, ga_ref, gg_ref,
                 o_ref):
    rows = slice(r0, r0 + BLOCK)
    kcat = jnp.concatenate([kvp[:, :KV_COLS], kvc[:, :KV_COLS]], axis=0)
    vcat = jnp.concatenate([kvp[:, KV_COLS:], kvc[:, KV_COLS:]], axis=0)

    lane = lax.broadcasted_iota(jnp.int32, (BLOCK, LANES), 1)
    lo = lane < HEAD_DIM
    kj = lax.broadcasted_iota(jnp.int32, (2 * BLOCK, BLOCK), 0)
    qi = lax.broadcasted_iota(jnp.int32, (2 * BLOCK, BLOCK), 1)
    rel = qi + BLOCK - kj
    valid = (rel >= 0) & (rel < WINDOW)
    if has_prev is not None:
        valid = valid & (has_prev | (kj >= BLOCK))
    row_lo = lax.broadcasted_iota(jnp.int32, (LANES, BLOCK), 0) < HEAD_DIM
    zero_q = jnp.zeros((BLOCK, LANES), BF16)
    nt_dims = (((1,), (1,)), ((), ()))

    attn = []
    chunks_per_kv = GQA_GROUP * HEAD_DIM // LANES
    for hk in range(N_KV_HEADS):
        ksl = kcat[:, hk * HEAD_DIM:(hk + 1) * HEAD_DIM]
        vsl = vcat[:, hk * HEAD_DIM:(hk + 1) * HEAD_DIM]
        kk = jnp.concatenate([ksl, ksl], axis=1)
        vt = jnp.concatenate([vsl, vsl], axis=1).astype(F32).T.astype(BF16)
        for j in range(chunks_per_kv):
            c = hk * chunks_per_kv + j
            qc = q[:, c * LANES:(c + 1) * LANES]
            probs, rdenoms = [], []
            for half in range(2):
                qm = jnp.where(lo, qc, zero_q) if half == 0 else jnp.where(lo, zero_q, qc)
                s = lax.dot_general(kk, qm, nt_dims, preferred_element_type=F32)
                s = jnp.where(valid, s, -jnp.inf)
                sink = sink_ref[2 * c + half] * LOG2_E
                m = jnp.maximum(jnp.max(s, axis=0, keepdims=True), sink)
                p = jnp.exp2(s - m)
                denom = jnp.sum(p, axis=0, keepdims=True) + jnp.exp2(sink - m)
                probs.append(p.astype(BF16))
                rdenoms.append(1.0 / denom)
            pn = jnp.concatenate(probs, axis=1)
            ot = jnp.dot(vt, pn, preferred_element_type=F32)
            sel = jnp.where(row_lo, ot[:, :BLOCK] * rdenoms[0], ot[:, BLOCK:] * rdenoms[1])
            attn.append(sel.T)

    ss = attn[0] * attn[0]
    for a in attn[1:]:
        ss = ss + a * a
    scale = lax.rsqrt(jnp.sum(ss, axis=-1, keepdims=True) * (1.0 / ATTN_WIDTH) + NORM_EPS)
    for c, a in enumerate(attn):
        o_ref[rows, c * LANES:(c + 1) * LANES] = (a * scale * ga_ref[:, c * LANES:(c + 1) * LANES]).astype(BF16)

    u = z_ref[rows, :GMLP_WIDTH]
    v = z_ref[rows, GMLP_WIDTH:]
    mu = jnp.mean(v, axis=-1, keepdims=True)
    vc = v - mu
    var = jnp.mean(vc * vc, axis=-1, keepdims=True)
    vn = ((vc * lax.rsqrt(var + NORM_EPS)) * lng_ref[...] + lnb_ref[...]).astype(BF16)
    ti = lax.broadcasted_iota(jnp.int32, (BLOCK, BLOCK), 0)
    si = lax.broadcasted_iota(jnp.int32, (BLOCK, BLOCK), 1)
    tril = si <= ti
    bst = bst_ref[...]
    gm = []
    for h in range(GMLP_GROUPS):
        w = jnp.where(tril, ws_ref[h], 0.0).astype(BF16)
        sp = jnp.dot(w, vn[:, h * LANES:(h + 1) * LANES], preferred_element_type=F32) + bst[:, h:h + 1]
        gm.append(u[:, h * LANES:(h + 1) * LANES] * sp)
    ss = gm[0] * gm[0]
    for a in gm[1:]:
        ss = ss + a * a
    scale = lax.rsqrt(jnp.sum(ss, axis=-1, keepdims=True) * (1.0 / GMLP_WIDTH) + NORM_EPS)
    for c, a in enumerate(gm):
        o_ref[rows, ATTN_WIDTH + c * LANES:ATTN_WIDTH + (c + 1) * LANES] = (
            a * scale * gg_ref[:, c * LANES:(c + 1) * LANES]).astype(BF16)


def _mixer(qkv, z, sinks, w_sp, bst, ln_g, ln_b, g_attn, g_gmlp, batch, seq):
    t = qkv.shape[0]
    nb = seq // BLOCK
    steps = nb // MIX_BLOCKS
    rows = MIX_BLOCKS * BLOCK
    kv_blk = Q_COLS // (2 * KV_COLS)
    cur = lambda b, n, s: (b * steps + n, 0)
    const2 = lambda b, n, s: (0, 0)
    grid_spec = pltpu.PrefetchScalarGridSpec(
        num_scalar_prefetch=1,
        grid=(batch, steps),
        in_specs=[
            pl.BlockSpec((rows, Q_COLS), cur),
            pl.BlockSpec((rows, 2 * KV_COLS), lambda b, n, s: (b * steps + n, kv_blk)),
            pl.BlockSpec((BLOCK, 2 * KV_COLS),
                         lambda b, n, s: (b * nb + jnp.maximum(MIX_BLOCKS * n - 1, 0), kv_blk)),
            pl.BlockSpec((rows, 2 * GMLP_WIDTH), cur),
            pl.BlockSpec((GMLP_GROUPS, BLOCK, BLOCK), lambda b, n, s: (0, 0, 0)),
            pl.BlockSpec((BLOCK, GMLP_GROUPS), const2),
            pl.BlockSpec((1, GMLP_WIDTH), const2),
            pl.BlockSpec((1, GMLP_WIDTH), const2),
            pl.BlockSpec((1, ATTN_WIDTH), const2),
            pl.BlockSpec((1, GMLP_WIDTH), const2),
        ],
        out_specs=pl.BlockSpec((rows, D_MODEL), cur),
    )
    return pl.pallas_call(
        _mixer_body,
        grid_spec=grid_spec,
        out_shape=jax.ShapeDtypeStruct((t, D_MODEL), BF16),
        compiler_params=_cparams(("arbitrary", "arbitrary")),
        name="mixer",
    )(sinks, qkv, qkv, qkv, z, w_sp, bst, ln_g, ln_b, g_attn, g_gmlp)


def _first_index_of_max(vals, lane, width):
    mx = jnp.max(vals, axis=-1, keepdims=True)
    idx = jnp.min(jnp.where(vals == mx, lane, width), axis=-1, keepdims=True)
    return mx, idx


def _outproj_body(m_ref, x_ref, mod_ref, g_ref, w_ref, wr_ref, br_ref,
                  xo_ref, h_ref, ids_ref, wt_ref, cnt_ref, carry_ref):
    i = pl.program_id(0)
    tm = x_ref.shape[0]

    @pl.when(i == 0)
    def _():
        carry_ref[...] = jnp.zeros_like(carry_ref)

    mod = mod_ref[0]
    y = jnp.dot(m_ref[...], w_ref[...], preferred_element_type=F32)
    x = x_ref[...] + (1.0 + mod[2:3, :]) * y
    xo_ref[...] = x
    h = _rms_mod(x, g_ref[...], mod[4:5, :], mod[3:4, :])
    packed = _pack_bf16_halves(h)
    for c in range(TOKEN_TILE_ROWS):
        h_ref[pl.ds(c, tm, stride=TOKEN_TILE_ROWS), :] = packed[:, c * LANES:(c + 1) * LANES]

    logits = jnp.dot(h.astype(BF16), wr_ref[...], preferred_element_type=F32) + br_ref[...]
    lane = lax.broadcasted_iota(jnp.int32, (tm, LANES), 1)
    neg = -jnp.inf
    gl = jnp.where(lane < N_GROUPS, logits, neg)
    ge = jnp.exp(gl - jnp.max(gl, axis=-1, keepdims=True))
    gprob = ge / jnp.sum(ge, axis=-1, keepdims=True)
    g_p, g_idx = _first_index_of_max(gprob, lane, LANES)

    base = N_GROUPS + EXPERTS_PER_GROUP * g_idx
    in_grp = (lane >= base) & (lane < base + EXPERTS_PER_GROUP)
    el = jnp.where(in_grp, logits, neg)
    ee = jnp.exp(el - jnp.max(el, axis=-1, keepdims=True))
    eprob = jnp.where(in_grp, ee / jnp.sum(ee, axis=-1, keepdims=True), -1.0)
    p1, i1 = _first_index_of_max(eprob, lane, LANES)
    eprob2 = jnp.where(lane == i1, -1.0, eprob)
    p2, i2 = _first_index_of_max(eprob2, lane, LANES)
    psum = p1 + p2
    w1 = g_p * (p1 / psum)
    w2 = g_p * (p2 / psum)
    e1 = i1 - N_GROUPS
    e2 = i2 - N_GROUPS

    oh1 = lane == e1
    oh2 = lane == e2
    onehot = jnp.where(oh1 | oh2, 1.0, 0.0)
    ri = lax.broadcasted_iota(jnp.int32, (tm, tm), 0)
    ci = lax.broadcasted_iota(jnp.int32, (tm, tm), 1)
    lower = jnp.where(ci < ri, 1.0, 0.0).astype(BF16)
    before = jnp.dot(lower, onehot.astype(BF16), preferred_element_type=F32) + carry_ref[...]
    r1 = jnp.sum(jnp.where(oh1, before, 0.0), axis=-1, keepdims=True)
    r2 = jnp.sum(jnp.where(oh2, before, 0.0), axis=-1, keepdims=True)
    carry = carry_ref[...] + jnp.sum(onehot, axis=0, keepdims=True)
    carry_ref[...] = carry
    cnt_ref[...] = jnp.broadcast_to(carry, cnt_ref.shape).astype(jnp.int32)

    ids = jnp.where(lane == 0, e1, jnp.where(lane == 1, e2, 0))
    ranks = jnp.where(lane == 0, r1, jnp.where(lane == 1, r2, 0.0)).astype(jnp.int32)
    ids_ref[...] = lax.shift_left(ids, ROUTE_SHIFT) | ranks
    wt_ref[...] = jnp.where(lane == 0, w1, jnp.where(lane == 1, w2, 0.0))


def _outproj(merged, x, mod_l, g2, w_out, w_router, b_router, seq):
    t, d = x.shape
    tm = OUT_TM
    tiles_per_seq = seq // tm
    row = lambda i: (i, 0)
    const = lambda i: (0, 0)
    return pl.pallas_call(
        _outproj_body,
        grid=(t // tm,),
        in_specs=[
            pl.BlockSpec((tm, d), row),
            pl.BlockSpec((tm, d), row),
            pl.BlockSpec((1, N_MOD, d), lambda i: (i // tiles_per_seq, 0, 0)),
            pl.BlockSpec((1, d), const),
            pl.BlockSpec((d, d), const, pipeline_mode=pl.Buffered(1)),
            pl.BlockSpec((d, LANES), const),
            pl.BlockSpec((1, LANES), const),
        ],
        out_specs=[
            pl.BlockSpec((tm, d), row),
            pl.BlockSpec((tm * TOKEN_TILE_ROWS, LANES), row),
            pl.BlockSpec((tm, LANES), row),
            pl.BlockSpec((tm, LANES), row),
            pl.BlockSpec((SUBLANES, LANES), const),
        ],
        out_shape=[
            jax.ShapeDtypeStruct((t, d), F32),
            jax.ShapeDtypeStruct((t * TOKEN_TILE_ROWS, LANES), jnp.uint32),
            jax.ShapeDtypeStruct((t, LANES), jnp.int32),
            jax.ShapeDtypeStruct((t, LANES), F32),
            jax.ShapeDtypeStruct((SUBLANES, LANES), jnp.int32),
        ],
        scratch_shapes=[pltpu.VMEM((1, LANES), F32)],
        compiler_params=_cparams(("arbitrary",)),
        name="outproj",
    )(merged, x, mod_l, g2, w_out, w_router, b_router)


def _moe_body(layer, n_slots, te_ref, nx_ref, nu_ref, h_hbm, src_hbm, dst_hbm, wg_hbm, wu_hbm, wd_hbm, y_hbm,
              first_smem, plan_smem, xbuf, obuf, wgs, wus, wds, wgb, wub, wdb,
              first_sem, plan_sem, g_sem, s_sem, w_sem):
    i = pl.program_id(0)
    n_used = nu_ref[0]
    nb = MOE_BUFS
    tm = plan_smem.shape[2]
    d, de = wgb.shape

    def weight_copies(e):
        return (pltpu.make_async_copy(wg_hbm.at[layer, e], wgs, w_sem.at[0]),
                pltpu.make_async_copy(wu_hbm.at[layer, e], wus, w_sem.at[1]),
                pltpu.make_async_copy(wd_hbm.at[layer, e], wds, w_sem.at[2]))
    b0 = i % nb
    b1 = (i + 1) % nb
    b2 = (i + 2) % nb
    p0 = i % 2
    p1 = 1 - p0

    class _PlanCopy:
        def __init__(self, k, slot):
            self.copies = (
                pltpu.make_async_copy(src_hbm.at[k + 3], plan_smem.at[slot, 0], plan_sem.at[slot, 0]),
                pltpu.make_async_copy(dst_hbm.at[k], plan_smem.at[slot, 1], plan_sem.at[slot, 1]))

        def start(self):
            for cp in self.copies:
                cp.start()

        def wait(self):
            for cp in self.copies:
                cp.wait()

    plan_copy = _PlanCopy

    xrows = tm * TOKEN_TILE_ROWS

    def token_tile(row):
        return pl.ds(pl.multiple_of(row, TOKEN_TILE_ROWS), TOKEN_TILE_ROWS)

    def tile_buf(buf):
        return pl.ds(pl.multiple_of(buf * xrows, xrows), xrows)

    def gather_row(src, buf, j):
        return pltpu.make_async_copy(h_hbm.at[token_tile(src), :],
                                     xbuf.at[token_tile(buf * xrows + j * TOKEN_TILE_ROWS), :], g_sem.at[buf])

    def scatter_row(dst, buf, j):
        return pltpu.make_async_copy(obuf.at[token_tile(buf * xrows + j * TOKEN_TILE_ROWS), :],
                                     y_hbm.at[token_tile(dst), :], s_sem.at[buf])

    def wait_gather(buf):
        pltpu.make_async_copy(h_hbm.at[pl.ds(0, xrows), :], xbuf.at[tile_buf(buf), :], g_sem.at[buf]).wait()

    def wait_scatter(buf):
        pltpu.make_async_copy(obuf.at[tile_buf(buf), :], y_hbm.at[pl.ds(0, xrows), :], s_sem.at[buf]).wait()

    @pl.when(i == 0)
    def _():
        first = [pltpu.make_async_copy(src_hbm.at[1 + tile], first_smem.at[tile], first_sem.at[tile])
                 for tile in range(2)]
        for cp in first:
            cp.start()
        plan_copy(0, 0).start()
        for cp in weight_copies(te_ref[0]):
            cp.start()
        for cp in first:
            cp.wait()
        for tile in range(2):
            def body(j, carry, tile=tile):
                gather_row(first_smem[tile, j], tile, j).start()
                return carry
            lax.fori_loop(0, tm, body, 0, unroll=8)
        obuf[pl.ds((nb - 1) * xrows, xrows), :] = jnp.zeros((xrows, LANES), jnp.uint32)

    @pl.when(i < n_used)
    def _():
        plan_copy(i, p0).wait()
        plan_copy(i + 1, p1).start()
        wait_gather(b0)

        @pl.when(i >= 2)
        def _():
            wait_scatter(b0)

        new_expert = jnp.logical_or(i == 0, te_ref[i] != te_ref[jnp.maximum(i - 1, 0)])

        @pl.when(new_expert)
        def _():
            for cp in weight_copies(te_ref[i]):
                cp.wait()
            wgb[...] = wgs[...].astype(BF16)
            wub[...] = wus[...].astype(BF16)
            wdb[...] = wds[...].astype(BF16)

            @pl.when(nx_ref[i] != te_ref[i])
            def _():
                for cp in weight_copies(nx_ref[i]):
                    cp.start()

        copies = []
        for j in range(tm):
            copies += [(0, j), (1, j)]
        n_up = de // MOE_N_CHUNK
        n_down = d // MOE_N_CHUNK
        n_pieces = 2 * n_up + n_down
        per_piece = -(-len(copies) // n_pieces)

        def issue(piece):
            for kind, j in copies[piece * per_piece:(piece + 1) * per_piece]:
                if kind == 0:
                    gather_row(plan_smem[p0, 0, j], b2, j).start()
                else:
                    scatter_row(plan_smem[p0, 1, j], b2, j).start()

        xbase = pl.multiple_of(b0 * xrows, xrows)
        xb = _unpack_bf16_halves(_load_token_tiles(xbuf, xbase, tm))
        piece = 0
        acts = []
        for n in range(n_up):
            cols = slice(n * MOE_N_CHUNK, (n + 1) * MOE_N_CHUNK)
            g = jnp.dot(xb, wgb[:, cols], preferred_element_type=F32)
            issue(piece)
            u = jnp.dot(xb, wub[:, cols], preferred_element_type=F32)
            issue(piece + 1)
            piece += 2
            acts.append(((g * jax.nn.sigmoid(g)) * u).astype(BF16))
        a = jnp.concatenate(acts, axis=1)
        for n in range(n_down // 2):
            cols_lo = slice(n * MOE_N_CHUNK, (n + 1) * MOE_N_CHUNK)
            cols_hi = slice(d // 2 + n * MOE_N_CHUNK, d // 2 + (n + 1) * MOE_N_CHUNK)
            y_lo = jnp.dot(a, wdb[:, cols_lo], preferred_element_type=F32)
            issue(piece)
            y_hi = jnp.dot(a, wdb[:, cols_hi], preferred_element_type=F32)
            issue(piece + 1)
            piece += 2
            packed = _pack_bf16_pair(y_lo, y_hi)
            for c in range(MOE_N_CHUNK // LANES):
                obuf[pl.ds(xbase + n * (MOE_N_CHUNK // LANES) + c, tm, stride=TOKEN_TILE_ROWS), :] = (
                    packed[:, c * LANES:(c + 1) * LANES])

    @pl.when(i == n_used)
    def _():
        plan_copy(i, p0).wait()
        wait_gather(b0)
        wait_gather(b1)
        wait_scatter(b1)

        @pl.when(i >= 2)
        def _():
            wait_scatter(b0)

        def body(j, carry):
            scatter_row(plan_smem[p0, 1, j], b2, j).start()
            return carry
        lax.fori_loop(0, tm, body, 0, unroll=8)
        wait_scatter(b2)
        obuf[tile_buf(b0), :] = jnp.zeros((xrows, LANES), jnp.uint32)
        for k in range(nb):
            fill = pltpu.make_async_copy(obuf.at[tile_buf(b0), :],
                                         y_hbm.at[pl.ds((n_slots + k * tm) * TOKEN_TILE_ROWS, xrows), :], s_sem.at[b0])
            fill.start()
            fill.wait()


def _moe(layer, h2, src2d, dst2d, tile_expert, next_expert, n_used, w_gate, w_up, w_down):
    t = h2.shape[0] // TOKEN_TILE_ROWS
    n_slots = 2 * t
    steps, tm = dst2d.shape[0] - MOE_BUFS, dst2d.shape[1]
    d, de = w_gate.shape[-2:]
    hbm = pl.BlockSpec(memory_space=pl.ANY)
    grid_spec = pltpu.PrefetchScalarGridSpec(
        num_scalar_prefetch=3,
        grid=(steps,),
        in_specs=[hbm] * 6,
        out_specs=hbm,
        scratch_shapes=[
            pltpu.SMEM((2, tm), jnp.int32),
            pltpu.SMEM((2, 2, tm), jnp.int32),
            pltpu.VMEM((MOE_BUFS * tm * TOKEN_TILE_ROWS, LANES), jnp.uint32),
            pltpu.VMEM((MOE_BUFS * tm * TOKEN_TILE_ROWS, LANES), jnp.uint32),
            pltpu.VMEM((d, de), F32),
            pltpu.VMEM((d, de), F32),
            pltpu.VMEM((de, d), F32),
            pltpu.VMEM((d, de), BF16),
            pltpu.VMEM((d, de), BF16),
            pltpu.VMEM((de, d), BF16),
            pltpu.SemaphoreType.DMA((2,)),
            pltpu.SemaphoreType.DMA((2, 2)),
            pltpu.SemaphoreType.DMA((MOE_BUFS,)),
            pltpu.SemaphoreType.DMA((MOE_BUFS,)),
            pltpu.SemaphoreType.DMA((3,)),
        ],
    )
    return pl.pallas_call(
        functools.partial(_moe_body, layer, n_slots),
        grid_spec=grid_spec,
        out_shape=jax.ShapeDtypeStruct(((n_slots + MOE_BUFS * tm) * TOKEN_TILE_ROWS, LANES), jnp.uint32),
        compiler_params=_cparams(("arbitrary",)),
        name="moe",
    )(tile_expert, next_expert, n_used, h2, src2d, dst2d, w_gate, w_up, w_down)


def _route_plan(ids, counts, t):
    tm = MOE_TM
    nt = (2 * t) // tm + N_EXPERTS
    cnt = counts[0, :N_EXPERTS]
    tiles = (cnt + tm - 1) // tm
    tile_end = jnp.cumsum(tiles)
    off = ((tile_end - tiles) * tm).astype(jnp.int32)
    tile_buf = np.arange(-1, nt) % MOE_BUFS
    dump = (2 * t + tile_buf[:, None] * tm + np.arange(tm)[None, :]).astype(np.int32)
    init = np.concatenate([dump, np.zeros((MOE_BUFS, tm), np.int32)], axis=0)
    dst2d = _plan_scatter(off, ids[:, 0:2].reshape(-1), jnp.asarray(init), t)
    src2d = jnp.where(dst2d < t, dst2d, jnp.where(dst2d < 2 * t, dst2d - t, 0)) * TOKEN_TILE_ROWS
    n_used = tile_end[-1]
    steps = jnp.minimum(jnp.arange(nt + 1, dtype=jnp.int32), n_used - 1)
    tile_expert = jnp.sum(tile_end[None, :] <= steps[:, None], axis=1).astype(jnp.int32)
    eid = jnp.arange(N_EXPERTS, dtype=jnp.int32)
    later_used = (tiles[None, :] > 0) & (eid[None, :] > eid[:, None])
    nxt = jnp.min(jnp.where(later_used, eid[None, :], N_EXPERTS), axis=1)
    next_expert = jnp.where(nxt == N_EXPERTS, eid, nxt)[tile_expert].astype(jnp.int32)
    return src2d, dst2d * TOKEN_TILE_ROWS, tile_expert, next_expert, n_used.reshape(1).astype(jnp.int32)


PLAN_CHUNK = 8192
ROUTE_SHIFT = 20


def _plan_body(t, off_ref, ids_hbm, init_hbm, out_hbm, tab, ids_a, ids_b, init_sem, ids_sem, out_sem):
    chunk = ids_a.shape[0]
    n_chunks = (2 * t) // chunk
    bufs = (ids_a, ids_b)

    def ids_copy(c):
        return pltpu.make_async_copy(ids_hbm.at[pl.ds(c * chunk, chunk)], bufs[c % 2], ids_sem.at[c % 2])

    init = pltpu.make_async_copy(init_hbm, tab, init_sem.at[0])
    init.start()
    ids_copy(0).start()
    init.wait()
    for c in range(n_chunks):
        ids_copy(c).wait()
        if c + 1 < n_chunks:
            ids_copy(c + 1).start()
        buf = bufs[c % 2]
        tok0 = c * (chunk // 2)

        def body(tt, carry, buf=buf, tok0=tok0):
            tok = tok0 + tt
            for k in range(2):
                word = buf[2 * tt + k]
                expert = lax.shift_right_logical(word, ROUTE_SHIFT)
                rank = word & ((1 << ROUTE_SHIFT) - 1)
                tab[off_ref[expert] + rank] = tok + k * t
            return carry
        lax.fori_loop(0, chunk // 2, body, 0, unroll=8)
    out = pltpu.make_async_copy(tab, out_hbm, out_sem.at[0])
    out.start()
    out.wait()


def _plan_scatter(off, ids_flat, dump, t):
    rows, tm = dump.shape
    chunk = min(PLAN_CHUNK, 2 * t)
    assert (2 * t) % chunk == 0
    words = -(-rows * tm // 1024) * 1024
    init = jnp.concatenate([dump.reshape(-1), jnp.zeros((words - rows * tm,), jnp.int32)])
    hbm = pl.BlockSpec(memory_space=pl.ANY)
    grid_spec = pltpu.PrefetchScalarGridSpec(
        num_scalar_prefetch=1,
        grid=(1,),
        in_specs=[hbm, hbm],
        out_specs=hbm,
        scratch_shapes=[
            pltpu.SMEM((words,), jnp.int32),
            pltpu.SMEM((chunk,), jnp.int32),
            pltpu.SMEM((chunk,), jnp.int32),
            pltpu.SemaphoreType.DMA((1,)),
            pltpu.SemaphoreType.DMA((2,)),
            pltpu.SemaphoreType.DMA((1,)),
        ],
    )
    flat = pl.pallas_call(
        functools.partial(_plan_body, t),
        grid_spec=grid_spec,
        out_shape=jax.ShapeDtypeStruct((words,), jnp.int32),
        compiler_params=_cparams(("arbitrary",)),
        name="plan",
    )(off + tm, ids_flat, init)
    return flat[:rows * tm].reshape(rows, tm)


def _final_body(x_ref, y0_ref, y1_ref, wt_ref, pmod_ref, g_ref, o_ref):
    x = _combine(x_ref[...], y0_ref, y1_ref, wt_ref, pmod_ref[0, 5:6, :])
    ms = jnp.mean(x * x, axis=-1, keepdims=True)
    o_ref[...] = x * lax.rsqrt(ms + NORM_EPS) * g_ref[...]


def _final(x, y2, wts, pmod, g_final, seq):
    t, d = x.shape
    tm = FINAL_TM
    tiles_per_seq = seq // tm
    row = lambda i: (i, 0)
    return pl.pallas_call(
        _final_body,
        grid=(t // tm,),
        in_specs=[
            pl.BlockSpec((tm, d), row),
            pl.BlockSpec((tm * TOKEN_TILE_ROWS, LANES), row),
            pl.BlockSpec((tm * TOKEN_TILE_ROWS, LANES), lambda i: (i + t // tm, 0)),
            pl.BlockSpec((tm, LANES), row),
            pl.BlockSpec((1, N_MOD, d), lambda i: (i // tiles_per_seq, 0, 0)),
            pl.BlockSpec((1, d), lambda i: (0, 0)),
        ],
        out_specs=pl.BlockSpec((tm, d), row),
        out_shape=jax.ShapeDtypeStruct((t, d), F32),
        compiler_params=_cparams(("arbitrary",)),
        name="final",
    )(x, y2, y2, wts, pmod, g_final)


def _rotary_tables(positions):
    half = ROT_DIM // 2
    inv_freq = ROPE_THETA ** (-(jnp.arange(0, ROT_DIM, 2, dtype=F32) / ROT_DIM))
    ang = positions.reshape(-1).astype(F32)[:, None] * inv_freq
    cos, sin = jnp.cos(ang), jnp.sin(ang)
    t = ang.shape[0]
    pad = HEAD_DIM - ROT_DIM
    cos_h = jnp.concatenate([cos, cos, jnp.ones((t, pad), F32)], axis=1)
    sa_h = jnp.concatenate([-sin, jnp.zeros((t, HEAD_DIM - half), F32)], axis=1)
    sb_h = jnp.concatenate([jnp.zeros((t, half), F32), sin, jnp.zeros((t, pad), F32)], axis=1)
    rep = LANES // HEAD_DIM
    return jnp.tile(cos_h, (1, rep)), jnp.tile(sa_h, (1, rep)), jnp.tile(sb_h, (1, rep))


def kernel(x, c, positions, w_ada, b_ada, g_norm1, w_in, b_in, sinks, ln_v_g, ln_v_b, w_spatial, b_spatial,
           g_attn_out, g_gmlp_out, w_out, g_norm2, w_router_group, b_router_group, w_router_expert,
           b_router_expert, w_e_gate, w_e_up, w_e_down, g_final):
    batch, seq, d = x.shape
    depth = w_ada.shape[0]
    t = batch * seq
    assert d == D_MODEL and seq % max(PROJ_TM, OUT_TM, FINAL_TM, MIX_BLOCKS * BLOCK) == 0 and (2 * t) % MOE_TM == 0

    c_pad = jnp.zeros((SUBLANES, d), F32).at[:batch].set(c)
    mod = _adaln(c_pad, w_ada, b_ada).reshape(depth, SUBLANES, N_MOD, d)
    cos_t, sa_t, sb_t = _rotary_tables(positions)

    n_route = N_GROUPS + N_EXPERTS
    w_router = jnp.concatenate(
        [w_router_group, w_router_expert, jnp.zeros((depth, d, LANES - n_route), F32)], axis=-1)
    b_router = jnp.concatenate(
        [b_router_group, b_router_expert, jnp.zeros((depth, LANES - n_route), F32)], axis=-1)

    xt = x.reshape(t, d)
    prev = None
    for l in range(depth):
        xt, qkv, z = _proj(xt, prev, mod[l], g_norm1[l][None], w_in[l].astype(BF16), b_in[l][None],
                           cos_t, sa_t, sb_t, seq)
        merged = _mixer(qkv, z, sinks[l], w_spatial[l], b_spatial[l].T, ln_v_g[l][None], ln_v_b[l][None],
                        g_attn_out[l][None], g_gmlp_out[l][None], batch, seq)
        xt, h2, ids, wts, counts = _outproj(merged, xt, mod[l], g_norm2[l][None], w_out[l].astype(BF16),
                                            w_router[l].astype(BF16), b_router[l][None], seq)
        src2d, dst2d, tile_expert, next_expert, n_used = _route_plan(ids, counts, t)
        y2 = _moe(l, h2, src2d, dst2d, tile_expert, next_expert, n_used, w_e_gate, w_e_up, w_e_down)
        prev = (y2, wts, mod[l])
    out = _final(xt, prev[0], prev[1], prev[2], g_final[None], seq)
    return out.reshape(batch, seq, d)
```
